```python
import math
import jax, jax.numpy as jnp
from jax import lax
import numpy as np

D_MODEL = 1024
BATCH = 4
SEQ = 8192
DEPTH = 4

N_META = 16
POOL_WIDTH = D_MODEL // 2
POOL_WINDOWS = (2, 4, 8, 16)
POOL_GROUPS = len(POOL_WINDOWS)
POOL_GROUP_DIM = POOL_WIDTH // POOL_GROUPS
N_HEADS = 8
QK_NOPE_DIM = 64
QK_ROPE_DIM = 32
QK_HEAD_DIM = QK_NOPE_DIM + QK_ROPE_DIM
V_HEAD_DIM = 64
MLA_WIDTH = N_HEADS * V_HEAD_DIM
KV_LORA_RANK = 256
Q_LORA_RANK = 768
ROPE_THETA = 10000.0
NORM_EPS = 1e-6
Q_BLOCK = 128
MASK_VALUE = -1e30

IN_SPLITS = (POOL_WIDTH, POOL_WIDTH, Q_LORA_RANK, KV_LORA_RANK, QK_ROPE_DIM, MLA_WIDTH, D_MODEL, D_MODEL)
D_IN = sum(IN_SPLITS)
IN_SPLIT_POINTS = tuple(int(v) for v in np.cumsum(IN_SPLITS)[:-1])

kernel_name = "hybrid_pool_mla_gated_trunk"


def rmsnorm(x, gain):
    xf = x.astype(jnp.float32)
    inv = lax.rsqrt(jnp.mean(xf * xf, axis=-1, keepdims=True) + NORM_EPS)
    return (xf * inv * gain.astype(jnp.float32)).astype(x.dtype)


def apply_rope(x, pos):
    half = x.shape[-1] // 2
    inv_freq = ROPE_THETA ** (-jnp.arange(half, dtype=jnp.float32) / half)
    ang = pos.astype(jnp.float32)[..., None] * inv_freq
    cos = jnp.cos(ang)[:, :, None, :]
    sin = jnp.sin(ang)[:, :, None, :]
    xf = x.astype(jnp.float32)
    x1, x2 = xf[..., :half], xf[..., half:]
    return jnp.concatenate([x1 * cos - x2 * sin, x2 * cos + x1 * sin], axis=-1).astype(x.dtype)


def pool_mix(u, w_group, scale):
    B, L, _ = u.shape
    ug = u.reshape(B, L, POOL_GROUPS, POOL_GROUP_DIM).astype(jnp.float32)
    csum = jnp.cumsum(ug, axis=1)
    t1 = jnp.arange(1, L + 1, dtype=jnp.float32)
    means = []
    for g, w in enumerate(POOL_WINDOWS):
        s = csum[:, :, g]
        lag = jnp.pad(s, ((0, 0), (w, 0), (0, 0)))[:, :L]
        cnt = jnp.minimum(t1, float(w))[None, :, None]
        means.append((s - lag) / cnt)
    mixed = (jnp.stack(means, axis=2) - ug).astype(u.dtype)
    y = jnp.einsum('blgc,gcd->blgd', mixed, w_group)
    return y.reshape(B, L, POOL_WIDTH) * scale


def causal_block_attention(q, k, v):
    B, L = q.shape[0], q.shape[1]
    pad_front = (-N_META) % Q_BLOCK
    pad_back = (-(L + pad_front)) % Q_BLOCK
    padw = ((0, 0), (pad_front, pad_back), (0, 0), (0, 0))
    q, k, v = jnp.pad(q, padw), jnp.pad(k, padw), jnp.pad(v, padw)
    n_blocks = q.shape[1] // Q_BLOCK
    scale = 1.0 / math.sqrt(QK_HEAD_DIM)
    outs = []
    for i in range(n_blocks):
        q0 = i * Q_BLOCK
        kend = q0 + Q_BLOCK
        s = jnp.einsum('bqhd,bkhd->bhqk', q[:, q0:kend], k[:, :kend]).astype(jnp.float32) * scale
        qi = jnp.arange(q0, kend)[:, None]
        ki = jnp.arange(kend)[None, :]
        valid = (ki <= qi) & (ki >= pad_front)
        s = jnp.where(valid, s, MASK_VALUE)
        p = jax.nn.softmax(s, axis=-1).astype(v.dtype)
        outs.append(jnp.einsum('bhqk,bkhd->bqhd', p, v[:, :kend]))
    o = jnp.concatenate(outs, axis=1)
    return o[:, pad_front:pad_front + L]


def mla(c_q_raw, c_kv_raw, k_rope_raw, pos, g_qa, g_kva, w_q_b, w_kv_b, g_qn, g_kn):
    B, L, _ = c_q_raw.shape
    c_q = rmsnorm(c_q_raw, g_qa)
    c_kv = rmsnorm(c_kv_raw, g_kva)
    q = (c_q @ w_q_b).reshape(B, L, N_HEADS, QK_HEAD_DIM)
    kv = (c_kv @ w_kv_b).reshape(B, L, N_HEADS, QK_NOPE_DIM + V_HEAD_DIM)
    k_nope, v = kv[..., :QK_NOPE_DIM], kv[..., QK_NOPE_DIM:]
    k_pe = jnp.broadcast_to(k_rope_raw[:, :, None, :], (B, L, N_HEADS, QK_ROPE_DIM))
    k = jnp.concatenate([k_nope, k_pe], axis=-1)
    q = rmsnorm(q, g_qn)
    k = rmsnorm(k, g_kn)
    q = jnp.concatenate([q[..., :QK_NOPE_DIM], apply_rope(q[..., QK_NOPE_DIM:], pos)], axis=-1)
    k = jnp.concatenate([k[..., :QK_NOPE_DIM], apply_rope(k[..., QK_NOPE_DIM:], pos)], axis=-1)
    o = causal_block_attention(q, k, v)
    return o.reshape(B, L, MLA_WIDTH)


def setup_inputs(seed: int = 0) -> dict:
    key = jax.random.key(seed)
    ks = jax.random.split(key, 18)
    f32 = jnp.float32

    def nrm(k, shape, scale):
        return jax.random.normal(k, shape, f32) * scale

    def gain(k, shape):
        return 1.0 + 0.02 * jax.random.normal(k, shape, f32)

    x = jax.random.normal(ks[0], (BATCH, SEQ, D_MODEL), f32)
    offset = jax.random.randint(ks[1], (BATCH, 1), 0, 4096, dtype=jnp.int32)
    positions = offset + jnp.arange(SEQ, dtype=jnp.int32)[None, :]
    return {
        "x": x,
        "positions": positions,
        "meta_tokens": nrm(ks[2], (N_META, D_MODEL), 1.0),
        "norm_gain": gain(ks[3], (DEPTH, D_MODEL)),
        "w_in": nrm(ks[4], (DEPTH, D_MODEL, D_IN), D_MODEL ** -0.5),
        "pool_w_group": nrm(ks[5], (DEPTH, POOL_GROUPS, POOL_GROUP_DIM, POOL_GROUP_DIM), POOL_GROUP_DIM ** -0.5),
        "pool_scale": gain(ks[6], (DEPTH, POOL_WIDTH)),
        "pool_w_up": nrm(ks[7], (DEPTH, POOL_WIDTH, D_MODEL), POOL_WIDTH ** -0.5),
        "q_a_norm_gain": gain(ks[8], (DEPTH, Q_LORA_RANK)),
        "kv_a_norm_gain": gain(ks[9], (DEPTH, KV_LORA_RANK)),
        "w_q_b": nrm(ks[10], (DEPTH, Q_LORA_RANK, N_HEADS * QK_HEAD_DIM), Q_LORA_RANK ** -0.5),
        "w_kv_b": nrm(ks[11], (DEPTH, KV_LORA_RANK, N_HEADS * (QK_NOPE_DIM + V_HEAD_DIM)), KV_LORA_RANK ** -0.5),
        "q_norm_gain": gain(ks[12], (DEPTH, QK_HEAD_DIM)),
        "k_norm_gain": gain(ks[13], (DEPTH, QK_HEAD_DIM)),
        "mla_w_up": nrm(ks[14], (DEPTH, MLA_WIDTH, D_MODEL), MLA_WIDTH ** -0.5),
        "w_out": nrm(ks[15], (DEPTH, D_MODEL, D_MODEL), (D_MODEL * 2 * DEPTH) ** -0.5),
    }


def reference(x, positions, meta_tokens, norm_gain, w_in, pool_w_group, pool_scale, pool_w_up,
              q_a_norm_gain, kv_a_norm_gain, w_q_b, w_kv_b, q_norm_gain, k_norm_gain, mla_w_up, w_out):
    B = x.shape[0]
    meta = jnp.broadcast_to(meta_tokens[None].astype(x.dtype), (B, N_META, D_MODEL))
    h_res = jnp.concatenate([meta, x], axis=1)
    meta_pos = jnp.broadcast_to(jnp.arange(N_META, dtype=jnp.int32)[None], (B, N_META))
    pos = jnp.concatenate([meta_pos, positions + N_META], axis=1)

    for l in range(DEPTH):
        h = rmsnorm(h_res, norm_gain[l])
        proj = h @ w_in[l]
        u_pool, z_pool, c_q, c_kv, k_rope, z_mla, g_pool, g_mla = jnp.split(proj, IN_SPLIT_POINTS, axis=-1)
        y_pool = (pool_mix(u_pool, pool_w_group[l], pool_scale[l]) * jax.nn.silu(z_pool)) @ pool_w_up[l]
        o_mla = mla(c_q, c_kv, k_rope, pos, q_a_norm_gain[l], kv_a_norm_gain[l], w_q_b[l], w_kv_b[l],
                    q_norm_gain[l], k_norm_gain[l])
        y_mla = (o_mla * jax.nn.silu(z_mla)) @ mla_w_up[l]
        merged = jax.nn.sigmoid(g_pool) * y_pool + jax.nn.sigmoid(g_mla) * y_mla
        h_res = h_res + merged @ w_out[l]

    return h_res[:, N_META:]
```

```python
import functools
import math

import jax
import jax.numpy as jnp
from jax import lax
from jax.experimental import pallas as pl
from jax.experimental.pallas import tpu as pltpu

F32 = jnp.float32
BF16 = jnp.bfloat16

D_MODEL = 1024
DEPTH = 4
N_META = 16
POOL_WIDTH = 512
POOL_WINDOWS = (2, 4, 8, 16)
POOL_GROUP_DIM = 128
N_HEADS = 8
QK_NOPE_DIM = 64
QK_ROPE_DIM = 32
QK_HEAD_DIM = 96
V_HEAD_DIM = 64
MLA_WIDTH = 512
KV_LORA_RANK = 256
Q_LORA_RANK = 768
ROPE_THETA = 10000.0
NORM_EPS = 1e-6
MASK_VALUE = -1e30

LANES = 128
HEAD_PAD = LANES
PAD_FRONT = 112
ROW_TILE = 640
ATT_TILE = 640
HALO = 16
VMEM_LIMIT = 56 * 1024 * 1024

C_U, C_Z, C_Q, C_KV, C_ZM, C_GP, C_GM, C_KR, C_END = 0, 512, 1024, 1792, 2048, 2560, 3584, 4608, 4736


def _rms(x, gain, n):
    inv = lax.rsqrt(jnp.sum(x * x, axis=-1, keepdims=True) * (1.0 / n) + NORM_EPS)
    return x * inv * gain


def _rope(x, c, sa, sb):
    return x * c + pltpu.roll(x, 16, axis=1) * sa + pltpu.roll(x, LANES - 16, axis=1) * sb


def _rope_table_kernel(pos_ref, freq_ref, c_ref, sa_ref, sb_ref):
    ang = pos_ref[0].astype(F32) * freq_ref[...]
    lane = lax.broadcasted_iota(jnp.int32, ang.shape, 1)
    cos, sin = jnp.cos(ang), jnp.sin(ang)
    first = (lane >= QK_NOPE_DIM) & (lane < QK_NOPE_DIM + 16)
    second = (lane >= QK_NOPE_DIM + 16) & (lane < QK_HEAD_DIM)
    c_ref[0] = jnp.where(first | second, cos, jnp.where(lane < QK_NOPE_DIM, 1.0, 0.0))
    sa_ref[0] = jnp.where(second, sin, 0.0)
    sb_ref[0] = jnp.where(first, -sin, 0.0)


def _pre_kernel(h_ref, c_ref, sa_ref, sb_ref, ng_ref, win_ref, wg_ref, ps_ref, wpu_ref, gqa_ref, gkva_ref,
                wqb_ref, wkvb_ref, gqn_ref, gkn_ref,
                q_ref, k_ref, v_ref, sz_ref, sg_ref, pc_ref, halo_ref, *, q_scale):
    i = pl.program_id(1)
    t = h_ref.shape[1]
    x = h_ref[0]
    hn = _rms(x, ng_ref[...], D_MODEL).astype(BF16)

    def proj(lo, hi):
        return jnp.dot(hn, win_ref[:, lo:hi], preferred_element_type=F32)

    @pl.when(i == 0)
    def _():
        halo_ref[...] = jnp.zeros_like(halo_ref)

    u = proj(C_U, C_Z)
    uext = jnp.concatenate([halo_ref[...], u], axis=0)
    halo_ref[...] = u[t - HALO:, :]
    row = i * t + lax.broadcasted_iota(jnp.int32, (t, 1), 0)
    avail = jnp.maximum(row - (PAD_FRONT - 1), 1).astype(F32)
    ys = []
    for g, w in enumerate(POOL_WINDOWS):
        a = uext[:, g * LANES:(g + 1) * LANES]
        ug = a[HALO:, :]
        step = 1
        while step < w:
            a = a + pltpu.roll(a, step, axis=0)
            step *= 2
        inv_cnt = 1.0 / jnp.minimum(avail, float(w))
        mixed = (a[HALO:, :] * inv_cnt - ug).astype(BF16)
        ys.append(jnp.dot(mixed, wg_ref[g], preferred_element_type=F32))
    ymix = jnp.concatenate(ys, axis=1) * ps_ref[...]
    zp = proj(C_Z, C_Q)
    pooled = (ymix * (zp * jax.nn.sigmoid(zp))).astype(BF16)
    y_pool = jnp.dot(pooled, wpu_ref[...], preferred_element_type=F32)
    pc_ref[0] = jax.nn.sigmoid(proj(C_GP, C_GM)) * y_pool

    zm = proj(C_ZM, C_GP)
    sz_ref[0] = zm * jax.nn.sigmoid(zm)
    sg_ref[0] = jax.nn.sigmoid(proj(C_GM, C_KR))

    c, sa, sb = c_ref[0], sa_ref[0], sb_ref[0]
    cqn = _rms(proj(C_Q, C_KV), gqa_ref[...], Q_LORA_RANK).astype(BF16)
    qp = jnp.dot(cqn, wqb_ref[...], preferred_element_type=F32)
    for hd in range(N_HEADS):
        qn = _rms(qp[:, hd * HEAD_PAD:(hd + 1) * HEAD_PAD], gqn_ref[...], QK_HEAD_DIM)
        q_ref[0, hd] = (_rope(qn, c, sa, sb) * q_scale).astype(BF16)

    ckvn = _rms(proj(C_KV, C_ZM), gkva_ref[...], KV_LORA_RANK).astype(BF16)
    kv = jnp.dot(ckvn, wkvb_ref[...], preferred_element_type=F32)
    krp = proj(C_KR, C_END)
    lane = lax.broadcasted_iota(jnp.int32, (t, LANES), 1)
    low = lane < QK_NOPE_DIM
    for pair in range(N_HEADS // 2):
        kb = kv[:, pair * LANES:(pair + 1) * LANES]
        vb = kv[:, MLA_WIDTH + pair * LANES:MLA_WIDTH + (pair + 1) * LANES]
        for odd in range(2):
            hd = 2 * pair + odd
            src = pltpu.roll(kb, QK_NOPE_DIM, axis=1) if odd else kb
            kh = _rms(jnp.where(low, src, krp), gkn_ref[...], QK_HEAD_DIM)
            k_ref[0, hd] = _rope(kh, c, sa, sb).astype(BF16)
            v_ref[0, hd] = jnp.where(low != bool(odd), vb, 0.0).astype(BF16)


def _attn_kernel(q_ref, k_ref, v_ref, o_ref, m_ref, l_ref, acc_ref):
    qi = pl.program_id(2)
    tq = q_ref.shape[2]
    tk = tq
    outs = []
    for hh in range(2):
        q = q_ref[0, hh]
        m_ref[...] = jnp.full_like(m_ref, -jnp.inf)
        l_ref[...] = jnp.zeros_like(l_ref)
        acc_ref[...] = jnp.zeros_like(acc_ref)

        def step(j, masked, hh=hh, q=q):
            start = j * tk if isinstance(j, int) else pl.multiple_of(j * tk, tk)
            k = k_ref[0, hh, pl.ds(start, tk), :]
            v = v_ref[0, hh, pl.ds(start, tk), :]
            s = lax.dot_general(q, k, (((1,), (1,)), ((), ())), preferred_element_type=F32)
            if masked:
                qpos = qi * tq + lax.broadcasted_iota(jnp.int32, s.shape, 0)
                kpos = j * tk + lax.broadcasted_iota(jnp.int32, s.shape, 1)
                s = jnp.where((kpos <= qpos) & (kpos >= PAD_FRONT), s, MASK_VALUE)
            m_prev = m_ref[...]
            m_new = jnp.maximum(m_prev, jnp.max(s, axis=-1, keepdims=True))
            alpha = jnp.exp2(m_prev - m_new)
            p = jnp.exp2(s - m_new)
            l_ref[...] = alpha * l_ref[...] + jnp.sum(p, axis=-1, keepdims=True)
            acc_ref[...] = alpha * acc_ref[...] + jnp.dot(p.astype(BF16), v, preferred_element_type=F32)
            m_ref[...] = m_new

        step(0, True)

        def body(j, carry):
            step(j, False)
            return carry

        lax.fori_loop(1, qi, body, 0)

        @pl.when(qi > 0)
        def _():
            step(qi, True)

        outs.append(acc_ref[...] / l_ref[...])
    lane = lax.broadcasted_iota(jnp.int32, outs[0].shape, 1)
    o_ref[0] = jnp.where(lane < V_HEAD_DIM, outs[0], outs[1])


def _post_kernel(h_ref, o_ref, sz_ref, sg_ref, pc_ref, wmu_ref, wout_ref, out_ref):
    om = (o_ref[0] * sz_ref[0]).astype(BF16)
    y_mla = jnp.dot(om, wmu_ref[...], preferred_element_type=F32)
    merged = (pc_ref[0] + sg_ref[0] * y_mla).astype(BF16)
    out_ref[0] = h_ref[0] + jnp.dot(merged, wout_ref[...], preferred_element_type=F32)


def _const_spec(shape):
    return pl.BlockSpec(shape, lambda *_: (0,) * len(shape))


def _row_spec(t, width):
    return pl.BlockSpec((1, t, width), lambda b, i: (b, i, 0))


def _rope_tables(pos_pad, freq_lane):
    b, lp = pos_pad.shape
    t = ROW_TILE
    out = jax.ShapeDtypeStruct((b, lp, LANES), F32)
    return pl.pallas_call(
        _rope_table_kernel,
        grid=(b, lp // t),
        in_specs=[_row_spec(t, 1), _const_spec((1, LANES))],
        out_specs=[_row_spec(t, LANES)] * 3,
        out_shape=[out] * 3,
        name="rope_tables",
    )(pos_pad[:, :, None], freq_lane)


def _pre(h, tabs, lw):
    b, lp, _ = h.shape
    t = ROW_TILE
    head_shape = jax.ShapeDtypeStruct((b, N_HEADS, lp, HEAD_PAD), BF16)
    head_spec = pl.BlockSpec((1, N_HEADS, t, HEAD_PAD), lambda bb, i: (bb, 0, i, 0))
    q_scale = math.log2(math.e) / math.sqrt(QK_HEAD_DIM)
    weights = [lw["ng"], lw["win"], lw["wg"], lw["ps"], lw["wpu"], lw["gqa"], lw["gkva"], lw["wqb"], lw["wkvb"],
               lw["gqn"], lw["gkn"]]
    return pl.pallas_call(
        functools.partial(_pre_kernel, q_scale=q_scale),
        grid=(b, lp // t),
        in_specs=[_row_spec(t, D_MODEL)] + [_row_spec(t, LANES)] * 3 + [_const_spec(w.shape) for w in weights],
        out_specs=[head_spec, head_spec, head_spec, _row_spec(t, MLA_WIDTH), _row_spec(t, D_MODEL),
                   _row_spec(t, D_MODEL)],
        out_shape=[head_shape, head_shape, head_shape,
                   jax.ShapeDtypeStruct((b, lp, MLA_WIDTH), F32),
                   jax.ShapeDtypeStruct((b, lp, D_MODEL), F32),
                   jax.ShapeDtypeStruct((b, lp, D_MODEL), F32)],
        scratch_shapes=[pltpu.VMEM((HALO, POOL_WIDTH), F32)],
        compiler_params=pltpu.CompilerParams(dimension_semantics=("parallel", "arbitrary"),
                                             vmem_limit_bytes=VMEM_LIMIT),
        name="pre",
    )(h, *tabs, *weights)


def _attention(q, k, v):
    b, nh, lp, _ = q.shape
    t = ATT_TILE
    return pl.pallas_call(
        _attn_kernel,
        grid=(b, nh // 2, lp // t),
        in_specs=[pl.BlockSpec((1, 2, t, HEAD_PAD), lambda bb, p, i: (bb, p, i, 0)),
                  pl.BlockSpec((1, 2, lp, HEAD_PAD), lambda bb, p, i: (bb, p, 0, 0)),
                  pl.BlockSpec((1, 2, lp, HEAD_PAD), lambda bb, p, i: (bb, p, 0, 0))],
        out_specs=pl.BlockSpec((1, t, LANES), lambda bb, p, i: (bb, i, p)),
        out_shape=jax.ShapeDtypeStruct((b, lp, MLA_WIDTH), F32),
        scratch_shapes=[pltpu.VMEM((t, 1), F32), pltpu.VMEM((t, 1), F32), pltpu.VMEM((t, LANES), F32)],
        compiler_params=pltpu.CompilerParams(dimension_semantics=("parallel", "parallel", "arbitrary"),
                                             vmem_limit_bytes=VMEM_LIMIT),
        name="attn",
    )(q, k, v)


def _post(h, o, sz, sg, pc, lw):
    b, lp, _ = h.shape
    t = ROW_TILE
    return pl.pallas_call(
        _post_kernel,
        grid=(b, lp // t),
        in_specs=[_row_spec(t, D_MODEL), _row_spec(t, MLA_WIDTH), _row_spec(t, MLA_WIDTH), _row_spec(t, D_MODEL),
                  _row_spec(t, D_MODEL), _const_spec(lw["wmu"].shape), _const_spec(lw["wout"].shape)],
        out_specs=_row_spec(t, D_MODEL),
        out_shape=jax.ShapeDtypeStruct(h.shape, F32),
        input_output_aliases={0: 0},
        compiler_params=pltpu.CompilerParams(dimension_semantics=("parallel", "parallel"),
                                             vmem_limit_bytes=VMEM_LIMIT),
        name="post",
    )(h, o, sz, sg, pc, lw["wmu"], lw["wout"])


def _pad_lanes(v, width):
    return jnp.pad(v, (0, width - v.shape[0]))[None, :]


def _layer_weights(l, norm_gain, w_in, pool_w_group, pool_scale, pool_w_up, q_a_norm_gain, kv_a_norm_gain, w_q_b,
                   w_kv_b, q_norm_gain, k_norm_gain, mla_w_up, w_out):
    wi = w_in[l]
    u, z, cq, ckv, kr, zm, gp, gm = jnp.split(wi, (512, 1024, 1792, 2048, 2080, 2592, 3616), axis=1)
    kr_block = jnp.pad(kr, ((0, 0), (QK_NOPE_DIM, LANES - QK_HEAD_DIM)))
    win = jnp.concatenate([u, z, cq, ckv, zm, gp, gm, kr_block], axis=1).astype(BF16)
    wqb = w_q_b[l].reshape(Q_LORA_RANK, N_HEADS, QK_HEAD_DIM)
    wqb = jnp.pad(wqb, ((0, 0), (0, 0), (0, HEAD_PAD - QK_HEAD_DIM))).reshape(Q_LORA_RANK, N_HEADS * HEAD_PAD)
    wkv = w_kv_b[l].reshape(KV_LORA_RANK, N_HEADS, QK_NOPE_DIM + V_HEAD_DIM)
    wkvb = jnp.concatenate([wkv[:, :, :QK_NOPE_DIM].reshape(KV_LORA_RANK, -1),
                            wkv[:, :, QK_NOPE_DIM:].reshape(KV_LORA_RANK, -1)], axis=1)
    return {
        "ng": norm_gain[l][None, :],
        "win": win,
        "wg": pool_w_group[l].astype(BF16),
        "ps": pool_scale[l][None, :],
        "wpu": pool_w_up[l].astype(BF16),
        "gqa": q_a_norm_gain[l][None, :],
        "gkva": kv_a_norm_gain[l][None, :],
        "wqb": wqb.astype(BF16),
        "wkvb": wkvb.astype(BF16),
        "gqn": _pad_lanes(q_norm_gain[l], HEAD_PAD),
        "gkn": _pad_lanes(k_norm_gain[l], HEAD_PAD),
        "wmu": mla_w_up[l].astype(BF16),
        "wout": w_out[l].astype(BF16),
    }


def kernel(x, positions, meta_tokens, norm_gain, w_in, pool_w_group, pool_scale, pool_w_up, q_a_norm_gain,
           kv_a_norm_gain, w_q_b, w_kv_b, q_norm_gain, k_norm_gain, mla_w_up, w_out):
    b, seq, _ = x.shape
    lp = PAD_FRONT + N_META + seq
    assert lp % ROW_TILE == 0 and lp % ATT_TILE == 0

    meta = jnp.broadcast_to(meta_tokens[None].astype(x.dtype), (b, N_META, D_MODEL))
    h = jnp.concatenate([jnp.zeros((b, PAD_FRONT, D_MODEL), x.dtype), meta, x], axis=1)
    meta_pos = jnp.broadcast_to(jnp.arange(N_META, dtype=jnp.int32)[None], (b, N_META))
    pos = jnp.concatenate([jnp.zeros((b, PAD_FRONT), jnp.int32), meta_pos, positions + N_META], axis=1)

    half = QK_ROPE_DIM // 2
    inv_freq = ROPE_THETA ** (-jnp.arange(half, dtype=F32) / half)
    freq_lane = jnp.concatenate([jnp.zeros((QK_NOPE_DIM,), F32), inv_freq, inv_freq,
                                 jnp.zeros((LANES - QK_HEAD_DIM,), F32)])[None, :]
    tabs = _rope_tables(pos, freq_lane)

    for l in range(DEPTH):
        lw = _layer_weights(l, norm_gain, w_in, pool_w_group, pool_scale, pool_w_up, q_a_norm_gain, kv_a_norm_gain,
                            w_q_b, w_kv_b, q_norm_gain, k_norm_gain, mla_w_up, w_out)
        q, k, v, sz, sg, pc = _pre(h, tabs, lw)
        o = _attention(q, k, v)
        h = _post(h, o, sz, sg, pc, lw)

    return h[:, PAD_FRONT + N_META:]
```

```python
import functools
import math

import jax
import jax.numpy as jnp
from jax import lax
from jax.experimental import pallas as pl
from jax.experimental.pallas import tpu as pltpu

F32 = jnp.float32
BF16 = jnp.bfloat16

D_MODEL = 1024
DEPTH = 4
N_META = 16
POOL_WIDTH = 512
POOL_WINDOWS = (2, 4, 8, 16)
POOL_GROUP_DIM = 128
N_HEADS = 8
QK_NOPE_DIM = 64
QK_ROPE_DIM = 32
QK_HEAD_DIM = 96
V_HEAD_DIM = 64
MLA_WIDTH = 512
KV_LORA_RANK = 256
Q_LORA_RANK = 768
ROPE_THETA = 10000.0
NORM_EPS = 1e-6
MASK_VALUE = -1e30

LANES = 128
HEAD_PAD = LANES
PAD_FRONT = 0
PAD_BACK = 240
ROW_TILE = 768
ATT_TILE = 256
SCORE_BOUND = 32.0
ATT_HEADS = 8
HALO = 16
VMEM_LIMIT = 56 * 1024 * 1024

C_U, C_Z, C_Q, C_KV, C_ZM, C_GP, C_GM, C_KR, C_END = 0, 512, 1024, 1792, 2048, 2560, 3584, 4608, 4736


def _rms(x, gain, n):
    inv = lax.rsqrt(jnp.sum(x * x, axis=-1, keepdims=True) * (1.0 / n) + NORM_EPS)
    return x * inv * gain


def _rope(x, c, sa, sb):
    return x * c + pltpu.roll(x, 16, axis=1) * sa + pltpu.roll(x, LANES - 16, axis=1) * sb


def _rope_table_kernel(pos_ref, freq_ref, c_ref, sa_ref, sb_ref):
    ang = pos_ref[0].astype(F32) * freq_ref[...]
    lane = lax.broadcasted_iota(jnp.int32, ang.shape, 1)
    cos, sin = jnp.cos(ang), jnp.sin(ang)
    first = (lane >= QK_NOPE_DIM) & (lane < QK_NOPE_DIM + 16)
    second = (lane >= QK_NOPE_DIM + 16) & (lane < QK_HEAD_DIM)
    c_ref[0] = jnp.where(first | second, cos, jnp.where(lane < QK_NOPE_DIM, 1.0, 0.0))
    sa_ref[0] = jnp.where(second, sin, 0.0)
    sb_ref[0] = jnp.where(first, -sin, 0.0)


def _pre_kernel(h_ref, c_ref, sa_ref, sb_ref, ng_ref, win_ref, wg_ref, ps_ref, wpu_ref, gqa_ref, gkva_ref,
                wqb_ref, wkvb_ref, gqn_ref, gkn_ref,
                q_ref, k_ref, v_ref, sz_ref, sg_ref, pc_ref, halo_ref, *, q_scale):
    i = pl.program_id(1)
    t = h_ref.shape[1]
    x = h_ref[0]
    hn = _rms(x, ng_ref[...], D_MODEL).astype(BF16)

    def proj(lo, hi):
        return jnp.dot(hn, win_ref[:, lo:hi], preferred_element_type=F32)

    @pl.when(i == 0)
    def _():
        halo_ref[...] = jnp.zeros_like(halo_ref)

    u = proj(C_U, C_Z)
    uext = jnp.concatenate([halo_ref[...], u], axis=0)
    halo_ref[...] = u[t - HALO:, :]
    row = i * t + lax.broadcasted_iota(jnp.int32, (t, 1), 0)
    avail = jnp.maximum(row - (PAD_FRONT - 1), 1).astype(F32)
    ys = []
    for g, w in enumerate(POOL_WINDOWS):
        a = uext[:, g * LANES:(g + 1) * LANES]
        ug = a[HALO:, :]
        step = 1
        while step < w:
            a = a + pltpu.roll(a, step, axis=0)
            step *= 2
        inv_cnt = 1.0 / jnp.minimum(avail, float(w))
        mixed = (a[HALO:, :] * inv_cnt - ug).astype(BF16)
        ys.append(jnp.dot(mixed, wg_ref[g], preferred_element_type=F32))
    ymix = jnp.concatenate(ys, axis=1) * ps_ref[...]
    zp = proj(C_Z, C_Q)
    pooled = (ymix * (zp * jax.nn.sigmoid(zp))).astype(BF16)
    y_pool = jnp.dot(pooled, wpu_ref[...], preferred_element_type=F32)
    pc_ref[0] = jax.nn.sigmoid(proj(C_GP, C_GM)) * y_pool

    zm = proj(C_ZM, C_GP)
    sz_ref[0] = zm * jax.nn.sigmoid(zm)
    sg_ref[0] = jax.nn.sigmoid(proj(C_GM, C_KR))

    c, sa, sb = c_ref[0], sa_ref[0], sb_ref[0]
    cqn = _rms(proj(C_Q, C_KV), gqa_ref[...], Q_LORA_RANK).astype(BF16)
    qp = jnp.dot(cqn, wqb_ref[...], preferred_element_type=F32)
    for hd in range(N_HEADS):
        qn = _rms(qp[:, hd * HEAD_PAD:(hd + 1) * HEAD_PAD], gqn_ref[...], QK_HEAD_DIM)
        q_ref[0, hd] = (_rope(qn, c, sa, sb) * q_scale).T.astype(BF16)

    ckvn = _rms(proj(C_KV, C_ZM), gkva_ref[...], KV_LORA_RANK).astype(BF16)
    kv = jnp.dot(ckvn, wkvb_ref[...], preferred_element_type=F32)
    krp = proj(C_KR, C_END)
    lane = lax.broadcasted_iota(jnp.int32, (t, LANES), 1)
    low = lane < QK_NOPE_DIM
    for pair in range(N_HEADS // 2):
        kb = kv[:, pair * LANES:(pair + 1) * LANES]
        vbt = kv[:, MLA_WIDTH + pair * LANES:MLA_WIDTH + (pair + 1) * LANES].T.astype(BF16)
        for ch in range(t // ATT_TILE):
            v_ref[0, pair, ch] = vbt[:, ch * ATT_TILE:(ch + 1) * ATT_TILE]
        for odd in range(2):
            hd = 2 * pair + odd
            src = pltpu.roll(kb, QK_NOPE_DIM, axis=1) if odd else kb
            kh = _rms(jnp.where(low, src, krp), gkn_ref[...], QK_HEAD_DIM)
            k_ref[0, hd] = _rope(kh, c, sa, sb).astype(BF16)


def _attn_kernel(bounded_ref, qt_ref, k_ref, vt_ref, o_ref):
    qi = pl.program_id(2)
    tq = qt_ref.shape[3]
    ck = vt_ref.shape[4]
    heads = range(qt_ref.shape[1])

    def chunk_scores(c):
        start = pl.multiple_of(c * ck, ck)
        return [jnp.dot(k_ref[0, hh, pl.ds(start, ck), :], qt_ref[0, hh], preferred_element_type=F32)
                for hh in heads]

    def causal(c, shape):
        kpos = c * ck + lax.broadcasted_iota(jnp.int32, shape, 0)
        qpos = qi * tq + lax.broadcasted_iota(jnp.int32, shape, 1)
        return kpos <= qpos

    def weighted_values(c, hh, p):
        vt = vt_ref[0, hh // 2, c][(hh % 2) * V_HEAD_DIM:(hh % 2 + 1) * V_HEAD_DIM, :]
        return jnp.dot(vt, p.astype(BF16), preferred_element_type=F32)

    def step_bounded(c, masked, state):
        scores = chunk_scores(c)
        new_state = []
        for hh in heads:
            l_prev, acc_prev = state[hh]
            p = jnp.exp2(scores[hh])
            if masked:
                p = jnp.where(causal(c, p.shape), p, 0.0)
            l_new = l_prev + jnp.sum(p.reshape(ck // 8, 8, tq), axis=0)
            new_state.append((l_new, acc_prev + weighted_values(c, hh, p)))
        return tuple(new_state)

    def step_general(c, masked, state):
        scores = chunk_scores(c)
        new_state = []
        for hh in heads:
            s = scores[hh]
            m_prev, l_prev, acc_prev = state[hh]
            if masked:
                s = jnp.where(causal(c, s.shape), s, MASK_VALUE)
            m_new = jnp.maximum(m_prev, jnp.max(s, axis=0, keepdims=True))
            alpha = jnp.exp2(m_prev - m_new)
            p = jnp.exp2(s - m_new)
            l_new = alpha * l_prev + jnp.sum(p, axis=0, keepdims=True)
            new_state.append((m_new, l_new, alpha * acc_prev + weighted_values(c, hh, p)))
        return tuple(new_state)

    def run(step, init, denom):
        state = lax.fori_loop(0, qi, lambda c, st: step(c, False, st), init)
        final = step(qi, True, state)
        out_t = jnp.concatenate([st[-1] / denom(st) for st in final], axis=0)
        o_ref[0] = out_t.T

    bounded = bounded_ref[0] != 0

    @pl.when(bounded)
    def _():
        init = tuple((jnp.zeros((8, tq), F32), jnp.zeros((V_HEAD_DIM, tq), F32)) for _ in heads)
        run(step_bounded, init, lambda st: jnp.sum(st[0], axis=0, keepdims=True))

    @pl.when(jnp.logical_not(bounded))
    def _():
        init = tuple((jnp.full((1, tq), -jnp.inf, F32), jnp.zeros((1, tq), F32),
                      jnp.zeros((V_HEAD_DIM, tq), F32)) for _ in heads)
        run(step_general, init, lambda st: st[1])


def _post_kernel(h_ref, o_ref, sz_ref, sg_ref, pc_ref, wmu_ref, wout_ref, out_ref):
    om = (o_ref[0] * sz_ref[0]).astype(BF16)
    y_mla = jnp.dot(om, wmu_ref[...], preferred_element_type=F32)
    merged = (pc_ref[0] + sg_ref[0] * y_mla).astype(BF16)
    out_ref[0] = h_ref[0] + jnp.dot(merged, wout_ref[...], preferred_element_type=F32)


def _const_spec(shape):
    return pl.BlockSpec(shape, lambda *_: (0,) * len(shape))


def _row_spec(t, width):
    return pl.BlockSpec((1, t, width), lambda b, i: (b, i, 0))


def _rope_tables(pos_pad, freq_lane):
    b, lp = pos_pad.shape
    t = ROW_TILE
    out = jax.ShapeDtypeStruct((b, lp, LANES), F32)
    return pl.pallas_call(
        _rope_table_kernel,
        grid=(b, lp // t),
        in_specs=[_row_spec(t, 1), _const_spec((1, LANES))],
        out_specs=[_row_spec(t, LANES)] * 3,
        out_shape=[out] * 3,
        name="rope_tables",
    )(pos_pad[:, :, None], freq_lane)


def _pre(h, tabs, lw):
    b, lp, _ = h.shape
    t = ROW_TILE
    nch = t // ATT_TILE
    k_shape = jax.ShapeDtypeStruct((b, N_HEADS, lp, HEAD_PAD), BF16)
    k_spec = pl.BlockSpec((1, N_HEADS, t, HEAD_PAD), lambda bb, i: (bb, 0, i, 0))
    qt_shape = jax.ShapeDtypeStruct((b, N_HEADS, HEAD_PAD, lp), BF16)
    qt_spec = pl.BlockSpec((1, N_HEADS, HEAD_PAD, t), lambda bb, i: (bb, 0, 0, i))
    vt_shape = jax.ShapeDtypeStruct((b, N_HEADS // 2, lp // ATT_TILE, 2 * V_HEAD_DIM, ATT_TILE), BF16)
    vt_spec = pl.BlockSpec((1, N_HEADS // 2, nch, 2 * V_HEAD_DIM, ATT_TILE), lambda bb, i: (bb, 0, i, 0, 0))
    q_scale = math.log2(math.e) / math.sqrt(QK_HEAD_DIM)
    weights = [lw["ng"], lw["win"], lw["wg"], lw["ps"], lw["wpu"], lw["gqa"], lw["gkva"], lw["wqb"], lw["wkvb"],
               lw["gqn"], lw["gkn"]]
    return pl.pallas_call(
        functools.partial(_pre_kernel, q_scale=q_scale),
        grid=(b, lp // t),
        in_specs=[_row_spec(t, D_MODEL)] + [_row_spec(t, LANES)] * 3 + [_const_spec(w.shape) for w in weights],
        out_specs=[qt_spec, k_spec, vt_spec, _row_spec(t, MLA_WIDTH), _row_spec(t, D_MODEL),
                   _row_spec(t, D_MODEL)],
        out_shape=[qt_shape, k_shape, vt_shape,
                   jax.ShapeDtypeStruct((b, lp, MLA_WIDTH), F32),
                   jax.ShapeDtypeStruct((b, lp, D_MODEL), F32),
                   jax.ShapeDtypeStruct((b, lp, D_MODEL), F32)],
        scratch_shapes=[pltpu.VMEM((HALO, POOL_WIDTH), F32)],
        compiler_params=pltpu.CompilerParams(dimension_semantics=("parallel", "arbitrary"),
                                             vmem_limit_bytes=VMEM_LIMIT),
        name="pre",
    )(h, *tabs, *weights)


def _scores_bounded(q_gain, k_gain):
    limit = (QK_HEAD_DIM * jnp.max(jnp.abs(q_gain)) * jnp.max(jnp.abs(k_gain))
             * (math.log2(math.e) / math.sqrt(QK_HEAD_DIM)) * 1.02)
    return (limit <= SCORE_BOUND).astype(jnp.int32).reshape(1)


def _attention(bounded, qt, k, vt):
    b, nh, lp, _ = k.shape
    t = ATT_TILE
    ah = ATT_HEADS
    resident = pl.Buffered(1)
    grid_spec = pltpu.PrefetchScalarGridSpec(
        num_scalar_prefetch=1,
        grid=(b, nh // ah, lp // t),
        in_specs=[pl.BlockSpec((1, ah, HEAD_PAD, t), lambda bb, p, i, f: (bb, p, 0, i)),
                  pl.BlockSpec((1, ah, lp, HEAD_PAD), lambda bb, p, i, f: (bb, p, 0, 0), pipeline_mode=resident),
                  pl.BlockSpec((1, ah // 2, lp // t, 2 * V_HEAD_DIM, t), lambda bb, p, i, f: (bb, p, 0, 0, 0),
                               pipeline_mode=resident)],
        out_specs=pl.BlockSpec((1, t, ah * V_HEAD_DIM), lambda bb, p, i, f: (bb, i, p)),
    )
    return pl.pallas_call(
        _attn_kernel,
        grid_spec=grid_spec,
        out_shape=jax.ShapeDtypeStruct((b, lp, MLA_WIDTH), F32),
        compiler_params=pltpu.CompilerParams(dimension_semantics=("parallel", "parallel", "arbitrary"),
                                             vmem_limit_bytes=VMEM_LIMIT),
        name="attn",
    )(bounded, qt, k, vt)


def _post(h, o, sz, sg, pc, lw):
    b, lp, _ = h.shape
    t = ROW_TILE
    return pl.pallas_call(
        _post_kernel,
        grid=(b, lp // t),
        in_specs=[_row_spec(t, D_MODEL), _row_spec(t, MLA_WIDTH), _row_spec(t, MLA_WIDTH), _row_spec(t, D_MODEL),
                  _row_spec(t, D_MODEL), _const_spec(lw["wmu"].shape), _const_spec(lw["wout"].shape)],
        out_specs=_row_spec(t, D_MODEL),
        out_shape=jax.ShapeDtypeStruct(h.shape, F32),
        input_output_aliases={0: 0},
        compiler_params=pltpu.CompilerParams(dimension_semantics=("parallel", "parallel"),
                                             vmem_limit_bytes=VMEM_LIMIT),
        name="post",
    )(h, o, sz, sg, pc, lw["wmu"], lw["wout"])


def _pad_lanes(v, width):
    return jnp.pad(v, (0, width - v.shape[0]))[None, :]


def _layer_weights(l, norm_gain, w_in, pool_w_group, pool_scale, pool_w_up, q_a_norm_gain, kv_a_norm_gain, w_q_b,
                   w_kv_b, q_norm_gain, k_norm_gain, mla_w_up, w_out):
    wi = w_in[l]
    u, z, cq, ckv, kr, zm, gp, gm = jnp.split(wi, (512, 1024, 1792, 2048, 2080, 2592, 3616), axis=1)
    kr_block = jnp.pad(kr, ((0, 0), (QK_NOPE_DIM, LANES - QK_HEAD_DIM)))
    win = jnp.concatenate([u, z, cq, ckv, zm, gp, gm, kr_block], axis=1).astype(BF16)
    wqb = w_q_b[l].reshape(Q_LORA_RANK, N_HEADS, QK_HEAD_DIM)
    wqb = jnp.pad(wqb, ((0, 0), (0, 0), (0, HEAD_PAD - QK_HEAD_DIM))).reshape(Q_LORA_RANK, N_HEADS * HEAD_PAD)
    wkv = w_kv_b[l].reshape(KV_LORA_RANK, N_HEADS, QK_NOPE_DIM + V_HEAD_DIM)
    wkvb = jnp.concatenate([wkv[:, :, :QK_NOPE_DIM].reshape(KV_LORA_RANK, -1),
                            wkv[:, :, QK_NOPE_DIM:].reshape(KV_LORA_RANK, -1)], axis=1)
    return {
        "ng": norm_gain[l][None, :],
        "win": win,
        "wg": pool_w_group[l].astype(BF16),
        "ps": pool_scale[l][None, :],
        "wpu": pool_w_up[l].astype(BF16),
        "gqa": q_a_norm_gain[l][None, :],
        "gkva": kv_a_norm_gain[l][None, :],
        "wqb": wqb.astype(BF16),
        "wkvb": wkvb.astype(BF16),
        "gqn": _pad_lanes(q_norm_gain[l], HEAD_PAD),
        "gkn": _pad_lanes(k_norm_gain[l], HEAD_PAD),
        "wmu": mla_w_up[l].astype(BF16),
        "wout": w_out[l].astype(BF16),
    }


def kernel(x, positions, meta_tokens, norm_gain, w_in, pool_w_group, pool_scale, pool_w_up, q_a_norm_gain,
           kv_a_norm_gain, w_q_b, w_kv_b, q_norm_gain, k_norm_gain, mla_w_up, w_out):
    b, seq, _ = x.shape
    lp = PAD_FRONT + N_META + seq + PAD_BACK
    assert lp % ROW_TILE == 0 and ROW_TILE % ATT_TILE == 0

    meta = jnp.broadcast_to(meta_tokens[None].astype(x.dtype), (b, N_META, D_MODEL))
    h = jnp.concatenate([jnp.zeros((b, PAD_FRONT, D_MODEL), x.dtype), meta, x,
                         jnp.zeros((b, PAD_BACK, D_MODEL), x.dtype)], axis=1)
    meta_pos = jnp.broadcast_to(jnp.arange(N_META, dtype=jnp.int32)[None], (b, N_META))
    pos = jnp.concatenate([jnp.zeros((b, PAD_FRONT), jnp.int32), meta_pos, positions + N_META,
                           jnp.zeros((b, PAD_BACK), jnp.int32)], axis=1)

    half = QK_ROPE_DIM // 2
    inv_freq = ROPE_THETA ** (-jnp.arange(half, dtype=F32) / half)
    freq_lane = jnp.concatenate([jnp.zeros((QK_NOPE_DIM,), F32), inv_freq, inv_freq,
                                 jnp.zeros((LANES - QK_HEAD_DIM,), F32)])[None, :]
    tabs = _rope_tables(pos, freq_lane)

    for l in range(DEPTH):
        lw = _layer_weights(l, norm_gain, w_in, pool_w_group, pool_scale, pool_w_up, q_a_norm_gain, kv_a_norm_gain,
                            w_q_b, w_kv_b, q_norm_gain, k_norm_gain, mla_w_up, w_out)
        q, k, v, sz, sg, pc = _pre(h, tabs, lw)
        o = _attention(_scores_bounded(q_norm_gain[l], k_norm_gain[l]), q, k, v)
        h = _post(h, o, sz, sg, pc, lw)

    return h[:, PAD_FRONT + N_META:PAD_FRONT + N_META + seq]
```

```python
import functools
import math

import jax
import jax.numpy as jnp
from jax import lax
from jax.experimental import pallas as pl
from jax.experimental.pallas import tpu as pltpu

F32 = jnp.float32
BF16 = jnp.bfloat16

D_MODEL = 1024
DEPTH = 4
N_META = 16
POOL_WIDTH = 512
POOL_WINDOWS = (2, 4, 8, 16)
POOL_GROUP_DIM = 128
N_HEADS = 8
QK_NOPE_DIM = 64
QK_ROPE_DIM = 32
QK_HEAD_DIM = 96
V_HEAD_DIM = 64
MLA_WIDTH = 512
KV_LORA_RANK = 256
Q_LORA_RANK = 768
ROPE_THETA = 10000.0
NORM_EPS = 1e-6
MASK_VALUE = -1e30

LANES = 128
HEAD_PAD = LANES
PAD_FRONT = 0
PAD_BACK = 240
ROW_TILE = 768
ATT_TILE = 256
KV_GROUP = 2
SCORE_BOUND = 32.0
ATT_HEADS = 8
HALO = 16
VMEM_LIMIT = 56 * 1024 * 1024

C_U, C_Z, C_Q, C_KV, C_ZM, C_GP, C_GM, C_KR, C_END = 0, 512, 1024, 1792, 2048, 2560, 3584, 4608, 4736


def _rms(x, gain, n):
    inv = lax.rsqrt(jnp.sum(x * x, axis=-1, keepdims=True) * (1.0 / n) + NORM_EPS)
    return x * inv * gain


def _head_lane(d):
    half = QK_ROPE_DIM // 2
    if d >= QK_NOPE_DIM:
        r = d - QK_NOPE_DIM
        return r if r < half else LANES // 2 + (r - half)
    return half + d if d < LANES // 2 - half else LANES // 2 + half + (d - (LANES // 2 - half))


HEAD_LANES = [_head_lane(d) for d in range(QK_HEAD_DIM)]


def _head_norm_rope(x, ones_ref, gain, c, s):
    sq = x * x
    hi = sq.astype(BF16)
    lo = (sq - hi.astype(F32)).astype(BF16)
    ss = (jnp.dot(hi, ones_ref[...], preferred_element_type=F32)
          + jnp.dot(lo, ones_ref[...], preferred_element_type=F32))
    xn = x * lax.rsqrt(ss * (1.0 / QK_HEAD_DIM) + NORM_EPS) * gain
    out = []
    for j in range(x.shape[1] // LANES):
        blk = xn[:, j * LANES:(j + 1) * LANES]
        out.append(blk * c + pltpu.roll(blk, LANES // 2, axis=1) * s)
    return out


def _rope_table_kernel(pos_ref, freq_ref, c_ref, s_ref):
    ang = pos_ref[0].astype(F32) * freq_ref[...]
    lane = lax.broadcasted_iota(jnp.int32, ang.shape, 1)
    c_ref[0] = jnp.cos(ang)
    s_ref[0] = jnp.where(lane < LANES // 2, -jnp.sin(ang), jnp.sin(ang))


def _pre_kernel(h_ref, c_ref, s_ref, ng_ref, win_ref, wg_ref, ps_ref, wpu_ref, gqa_ref, gkva_ref,
                wqb_ref, wkvb_ref, gqn_ref, gkn_ref, ones_ref,
                q_ref, k_ref, v_ref, sz_ref, sg_ref, pc_ref, halo_ref):
    i = pl.program_id(1)
    t = h_ref.shape[1]
    x = h_ref[0]
    hn = _rms(x, ng_ref[...], D_MODEL).astype(BF16)

    def proj(lo, hi):
        return jnp.dot(hn, win_ref[:, lo:hi], preferred_element_type=F32)

    @pl.when(i == 0)
    def _():
        halo_ref[...] = jnp.zeros_like(halo_ref)

    u = proj(C_U, C_Z)
    uext = jnp.concatenate([halo_ref[...], u], axis=0)
    halo_ref[...] = u[t - HALO:, :]
    row = i * t + lax.broadcasted_iota(jnp.int32, (t, 1), 0)
    avail = jnp.maximum(row - (PAD_FRONT - 1), 1).astype(F32)
    ys = []
    for g, w in enumerate(POOL_WINDOWS):
        a = uext[:, g * LANES:(g + 1) * LANES]
        ug = a[HALO:, :]
        step = 1
        while step < w:
            a = a + pltpu.roll(a, step, axis=0)
            step *= 2
        inv_cnt = 1.0 / jnp.minimum(avail, float(w))
        mixed = (a[HALO:, :] * inv_cnt - ug).astype(BF16)
        ys.append(jnp.dot(mixed, wg_ref[g], preferred_element_type=F32))
    ymix = jnp.concatenate(ys, axis=1) * ps_ref[...]
    zp = proj(C_Z, C_Q)
    pooled = (ymix * (zp * jax.nn.sigmoid(zp))).astype(BF16)
    y_pool = jnp.dot(pooled, wpu_ref[...], preferred_element_type=F32)
    pc_ref[0] = (jax.nn.sigmoid(proj(C_GP, C_GM)) * y_pool).astype(pc_ref.dtype)

    zm = proj(C_ZM, C_GP)
    sz_ref[0] = (zm * jax.nn.sigmoid(zm)).astype(sz_ref.dtype)
    sg_ref[0] = jax.nn.sigmoid(proj(C_GM, C_KR)).astype(sg_ref.dtype)

    c, s = c_ref[0], s_ref[0]
    cqn = _rms(proj(C_Q, C_KV), gqa_ref[...], Q_LORA_RANK).astype(BF16)
    qp = jnp.dot(cqn, wqb_ref[...], preferred_element_type=F32)
    ckvn = _rms(proj(C_KV, C_ZM), gkva_ref[...], KV_LORA_RANK).astype(BF16)
    kv = jnp.dot(ckvn, wkvb_ref[...], preferred_element_type=F32)
    krp = proj(C_KR, C_END)
    krp2 = jnp.concatenate([krp, krp], axis=1)
    width = 2 * HEAD_PAD
    for pair in range(N_HEADS // 2):
        lo = pair * width
        qs = _head_norm_rope(qp[:, lo:lo + width], ones_ref, gqn_ref[...], c, s)
        ks = _head_norm_rope(kv[:, lo:lo + width] + krp2, ones_ref, gkn_ref[...], c, s)
        for odd in range(2):
            q_ref[0, 2 * pair + odd] = qs[odd].T.astype(BF16)
            k_ref[0, 2 * pair + odd] = ks[odd].astype(BF16)
        v0 = N_HEADS * HEAD_PAD + pair * LANES
        vbt = kv[:, v0:v0 + LANES].T.astype(BF16)
        for ch in range(t // ATT_TILE):
            v_ref[0, pair, ch] = vbt[:, ch * ATT_TILE:(ch + 1) * ATT_TILE]


def _attn_kernel(bounded_ref, qt_ref, k_ref, vt_ref, o_ref, m_ref, l_ref, acc_ref):
    qi = pl.program_id(2)
    tq = qt_ref.shape[3]
    ck = vt_ref.shape[4]
    heads = range(qt_ref.shape[1])

    def span_scores(c0, n):
        start = pl.multiple_of(c0 * ck, ck)
        return [jnp.dot(k_ref[0, hh, pl.ds(start, n * ck), :], qt_ref[0, hh], preferred_element_type=F32)
                for hh in heads]

    def causal(c0, shape):
        kpos = c0 * ck + lax.broadcasted_iota(jnp.int32, shape, 0)
        qpos = qi * tq + lax.broadcasted_iota(jnp.int32, shape, 1)
        return kpos <= qpos

    def weighted_values(c0, n, hh, p):
        pb = p.astype(BF16)
        lo = (hh % 2) * V_HEAD_DIM
        out = None
        for c in range(n):
            vt = vt_ref[0, hh // 2, c0 + c][lo:lo + V_HEAD_DIM, :]
            d = jnp.dot(vt, pb[c * ck:(c + 1) * ck, :], preferred_element_type=F32)
            out = d if out is None else out + d
        return out

    def step_bounded(c0, n, masked):
        scores = span_scores(c0, n)
        for hh in heads:
            p = jnp.exp2(scores[hh])
            if masked:
                p = jnp.where(causal(c0, p.shape), p, 0.0)
            l_ref[hh] += jnp.sum(p.reshape(n * ck // 8, 8, tq), axis=0)
            acc_ref[hh] += weighted_values(c0, n, hh, p)

    def step_general(c0, n, masked):
        scores = span_scores(c0, n)
        for hh in heads:
            s = scores[hh]
            if masked:
                s = jnp.where(causal(c0, s.shape), s, MASK_VALUE)
            m_prev = m_ref[hh]
            m_new = jnp.maximum(m_prev, jnp.max(s, axis=0, keepdims=True))
            alpha = jnp.exp2(m_prev - m_new)
            p = jnp.exp2(s - m_new)
            l_ref[hh, 0:1, :] = alpha * l_ref[hh, 0:1, :] + jnp.sum(p, axis=0, keepdims=True)
            acc_ref[hh] = alpha * acc_ref[hh] + weighted_values(c0, n, hh, p)
            m_ref[hh] = m_new

    def run(step):
        m_ref[...] = jnp.full_like(m_ref, -jnp.inf)
        l_ref[...] = jnp.zeros_like(l_ref)
        acc_ref[...] = jnp.zeros_like(acc_ref)

        def body(g, carry):
            step(g * KV_GROUP, KV_GROUP, False)
            return carry

        lax.fori_loop(0, qi // KV_GROUP, body, 0)
        for r in range(KV_GROUP):
            @pl.when(qi % KV_GROUP == r)
            def _(r=r):
                step(qi - r, r + 1, True)
                out_t = jnp.concatenate([acc_ref[hh] / jnp.sum(l_ref[hh], axis=0, keepdims=True) for hh in heads],
                                        axis=0)
                o_ref[0] = out_t.T.astype(o_ref.dtype)

    bounded = bounded_ref[0] != 0

    @pl.when(bounded)
    def _():
        run(step_bounded)

    @pl.when(jnp.logical_not(bounded))
    def _():
        run(step_general)


def _post_kernel(h_ref, o_ref, sz_ref, sg_ref, pc_ref, wmu_ref, wout_ref, out_ref):
    om = (o_ref[0].astype(F32) * sz_ref[0].astype(F32)).astype(BF16)
    y_mla = jnp.dot(om, wmu_ref[...], preferred_element_type=F32)
    merged = (pc_ref[0].astype(F32) + sg_ref[0].astype(F32) * y_mla).astype(BF16)
    out_ref[0] = h_ref[0] + jnp.dot(merged, wout_ref[...], preferred_element_type=F32)


def _const_spec(shape):
    return pl.BlockSpec(shape, lambda *_: (0,) * len(shape))


def _row_spec(t, width):
    return pl.BlockSpec((1, t, width), lambda b, i: (b, i, 0))


def _rope_tables(pos_pad, freq_lane):
    b, lp = pos_pad.shape
    t = ROW_TILE
    out = jax.ShapeDtypeStruct((b, lp, LANES), F32)
    return pl.pallas_call(
        _rope_table_kernel,
        grid=(b, lp // t),
        in_specs=[_row_spec(t, 1), _const_spec((1, LANES))],
        out_specs=[_row_spec(t, LANES)] * 2,
        out_shape=[out] * 2,
        name="rope_tables",
    )(pos_pad[:, :, None], freq_lane)


def _pre(h, tabs, lw):
    b, lp, _ = h.shape
    t = ROW_TILE
    nch = t // ATT_TILE
    k_shape = jax.ShapeDtypeStruct((b, N_HEADS, lp, HEAD_PAD), BF16)
    k_spec = pl.BlockSpec((1, N_HEADS, t, HEAD_PAD), lambda bb, i: (bb, 0, i, 0))
    qt_shape = jax.ShapeDtypeStruct((b, N_HEADS, HEAD_PAD, lp), BF16)
    qt_spec = pl.BlockSpec((1, N_HEADS, HEAD_PAD, t), lambda bb, i: (bb, 0, 0, i))
    vt_shape = jax.ShapeDtypeStruct((b, N_HEADS // 2, lp // ATT_TILE, 2 * V_HEAD_DIM, ATT_TILE), BF16)
    vt_spec = pl.BlockSpec((1, N_HEADS // 2, nch, 2 * V_HEAD_DIM, ATT_TILE), lambda bb, i: (bb, 0, i, 0, 0))
    weights = [lw["ng"], lw["win"], lw["wg"], lw["ps"], lw["wpu"], lw["gqa"], lw["gkva"], lw["wqb"], lw["wkvb"],
               lw["gqn"], lw["gkn"], lw["ones"]]
    return pl.pallas_call(
        _pre_kernel,
        grid=(b, lp // t),
        in_specs=[_row_spec(t, D_MODEL)] + [_row_spec(t, LANES)] * 2 + [_const_spec(w.shape) for w in weights],
        out_specs=[qt_spec, k_spec, vt_spec, _row_spec(t, MLA_WIDTH), _row_spec(t, D_MODEL),
                   _row_spec(t, D_MODEL)],
        out_shape=[qt_shape, k_shape, vt_shape,
                   jax.ShapeDtypeStruct((b, lp, MLA_WIDTH), BF16),
                   jax.ShapeDtypeStruct((b, lp, D_MODEL), BF16),
                   jax.ShapeDtypeStruct((b, lp, D_MODEL), BF16)],
        scratch_shapes=[pltpu.VMEM((HALO, POOL_WIDTH), F32)],
        compiler_params=pltpu.CompilerParams(dimension_semantics=("parallel", "arbitrary"),
                                             vmem_limit_bytes=VMEM_LIMIT),
        name="pre",
    )(h, *tabs, *weights)


def _scores_bounded(q_gain, k_gain):
    limit = (QK_HEAD_DIM * jnp.max(jnp.abs(q_gain)) * jnp.max(jnp.abs(k_gain))
             * (math.log2(math.e) / math.sqrt(QK_HEAD_DIM)) * 1.02)
    return (limit <= SCORE_BOUND).astype(jnp.int32).reshape(1)


def _attention(bounded, qt, k, vt):
    b, nh, lp, _ = k.shape
    t = ATT_TILE
    ah = ATT_HEADS
    resident = pl.Buffered(1)
    grid_spec = pltpu.PrefetchScalarGridSpec(
        num_scalar_prefetch=1,
        grid=(b, nh // ah, lp // t),
        in_specs=[pl.BlockSpec((1, ah, HEAD_PAD, t), lambda bb, p, i, f: (bb, p, 0, i)),
                  pl.BlockSpec((1, ah, lp, HEAD_PAD), lambda bb, p, i, f: (bb, p, 0, 0), pipeline_mode=resident),
                  pl.BlockSpec((1, ah // 2, lp // t, 2 * V_HEAD_DIM, t), lambda bb, p, i, f: (bb, p, 0, 0, 0),
                               pipeline_mode=resident)],
        out_specs=pl.BlockSpec((1, t, ah * V_HEAD_DIM), lambda bb, p, i, f: (bb, i, p)),
        scratch_shapes=[pltpu.VMEM((ah, 1, t), F32),
                        pltpu.VMEM((ah, 8, t), F32),
                        pltpu.VMEM((ah, V_HEAD_DIM, t), F32)],
    )
    return pl.pallas_call(
        _attn_kernel,
        grid_spec=grid_spec,
        out_shape=jax.ShapeDtypeStruct((b, lp, MLA_WIDTH), BF16),
        compiler_params=pltpu.CompilerParams(dimension_semantics=("parallel", "parallel", "arbitrary"),
                                             vmem_limit_bytes=VMEM_LIMIT),
        name="attn",
    )(bounded, qt, k, vt)


def _post(h, o, sz, sg, pc, lw):
    b, lp, _ = h.shape
    t = ROW_TILE
    return pl.pallas_call(
        _post_kernel,
        grid=(b, lp // t),
        in_specs=[_row_spec(t, D_MODEL), _row_spec(t, MLA_WIDTH), _row_spec(t, MLA_WIDTH), _row_spec(t, D_MODEL),
                  _row_spec(t, D_MODEL), _const_spec(lw["wmu"].shape), _const_spec(lw["wout"].shape)],
        out_specs=_row_spec(t, D_MODEL),
        out_shape=jax.ShapeDtypeStruct(h.shape, F32),
        input_output_aliases={0: 0},
        compiler_params=pltpu.CompilerParams(dimension_semantics=("parallel", "parallel"),
                                             vmem_limit_bytes=VMEM_LIMIT),
        name="post",
    )(h, o, sz, sg, pc, lw["wmu"], lw["wout"])


def _to_head_lanes(a):
    lanes = jnp.asarray(HEAD_LANES[:a.shape[-1]])
    return jnp.zeros(a.shape[:-1] + (HEAD_PAD,), a.dtype).at[..., lanes].set(a)


def _layer_weights(l, norm_gain, w_in, pool_w_group, pool_scale, pool_w_up, q_a_norm_gain, kv_a_norm_gain, w_q_b,
                   w_kv_b, q_norm_gain, k_norm_gain, mla_w_up, w_out):
    wi = w_in[l]
    u, z, cq, ckv, kr, zm, gp, gm = jnp.split(wi, (512, 1024, 1792, 2048, 2080, 2592, 3616), axis=1)
    kr_block = _to_head_lanes(jnp.concatenate([jnp.zeros((D_MODEL, QK_NOPE_DIM), kr.dtype), kr], axis=1))
    win = jnp.concatenate([u, z, cq, ckv, zm, gp, gm, kr_block], axis=1).astype(BF16)
    wqb = _to_head_lanes(w_q_b[l].reshape(Q_LORA_RANK, N_HEADS, QK_HEAD_DIM)).reshape(Q_LORA_RANK, -1)
    wkv = w_kv_b[l].reshape(KV_LORA_RANK, N_HEADS, QK_NOPE_DIM + V_HEAD_DIM)
    wkvb = jnp.concatenate([_to_head_lanes(wkv[:, :, :QK_NOPE_DIM]).reshape(KV_LORA_RANK, -1),
                            wkv[:, :, QK_NOPE_DIM:].reshape(KV_LORA_RANK, -1)], axis=1)
    q_scale = math.log2(math.e) / math.sqrt(QK_HEAD_DIM)
    block = jnp.arange(2 * HEAD_PAD) // HEAD_PAD
    return {
        "ones": (block[:, None] == block[None, :]).astype(BF16),
        "ng": norm_gain[l][None, :],
        "win": win,
        "wg": pool_w_group[l].astype(BF16),
        "ps": pool_scale[l][None, :],
        "wpu": pool_w_up[l].astype(BF16),
        "gqa": q_a_norm_gain[l][None, :],
        "gkva": kv_a_norm_gain[l][None, :],
        "wqb": wqb.astype(BF16),
        "wkvb": wkvb.astype(BF16),
        "gqn": jnp.tile(_to_head_lanes(q_norm_gain[l] * q_scale), 2)[None, :],
        "gkn": jnp.tile(_to_head_lanes(k_norm_gain[l]), 2)[None, :],
        "wmu": mla_w_up[l].astype(BF16),
        "wout": w_out[l].astype(BF16),
    }


def kernel(x, positions, meta_tokens, norm_gain, w_in, pool_w_group, pool_scale, pool_w_up, q_a_norm_gain,
           kv_a_norm_gain, w_q_b, w_kv_b, q_norm_gain, k_norm_gain, mla_w_up, w_out):
    b, seq, _ = x.shape
    lp = PAD_FRONT + N_META + seq + PAD_BACK
    assert lp % ROW_TILE == 0 and ROW_TILE % ATT_TILE == 0

    meta = jnp.broadcast_to(meta_tokens[None].astype(x.dtype), (b, N_META, D_MODEL))
    h = jnp.concatenate([jnp.zeros((b, PAD_FRONT, D_MODEL), x.dtype), meta, x,
                         jnp.zeros((b, PAD_BACK, D_MODEL), x.dtype)], axis=1)
    meta_pos = jnp.broadcast_to(jnp.arange(N_META, dtype=jnp.int32)[None], (b, N_META))
    pos = jnp.concatenate([jnp.zeros((b, PAD_FRONT), jnp.int32), meta_pos, positions + N_META,
                           jnp.zeros((b, PAD_BACK), jnp.int32)], axis=1)

    half = QK_ROPE_DIM // 2
    inv_freq = ROPE_THETA ** (-jnp.arange(half, dtype=F32) / half)
    freq_lane = _to_head_lanes(jnp.concatenate([jnp.zeros((QK_NOPE_DIM,), F32), inv_freq, inv_freq]))[None, :]
    tabs = _rope_tables(pos, freq_lane)

    for l in range(DEPTH):
        lw = _layer_weights(l, norm_gain, w_in, pool_w_group, pool_scale, pool_w_up, q_a_norm_gain, kv_a_norm_gain,
                            w_q_b, w_kv_b, q_norm_gain, k_norm_gain, mla_w_up, w_out)
        q, k, v, sz, sg, pc = _pre(h, tabs, lw)
        o = _attention(_scores_bounded(q_norm_gain[l], k_norm_gain[l]), q, k, v)
        h = _post(h, o, sz, sg, pc, lw)

    return h[:, PAD_FRONT + N_META:PAD_FRONT + N_META + seq]
```

```python
import functools
import math

import jax
import jax.numpy as jnp
from jax import lax
from jax.experimental import pallas as pl
from jax.experimental.pallas import tpu as pltpu

F32 = jnp.float32
BF16 = jnp.bfloat16

D_MODEL = 1024
DEPTH = 4
N_META = 16
POOL_WIDTH = 512
POOL_WINDOWS = (2, 4, 8, 16)
POOL_GROUP_DIM = 128
N_HEADS = 8
QK_NOPE_DIM = 64
QK_ROPE_DIM = 32
QK_HEAD_DIM = 96
V_HEAD_DIM = 64
MLA_WIDTH = 512
KV_LORA_RANK = 256
Q_LORA_RANK = 768
ROPE_THETA = 10000.0
NORM_EPS = 1e-6
MASK_VALUE = -1e30

LANES = 128
HEAD_PAD = LANES
PAD_FRONT = 0
PAD_BACK = 240
ROW_TILE = 768
ATT_TILE = 256
KV_GROUP = 4
SCORE_BOUND = 32.0
ATT_HEADS = 8
HALO = 16
VMEM_LIMIT = 56 * 1024 * 1024

C_U, C_Z, C_Q, C_KV, C_ZM, C_GP, C_GM, C_KR, C_END = 0, 512, 1024, 1792, 2048, 2560, 3584, 4608, 4736


def _rms(x, gain, n):
    inv = lax.rsqrt(jnp.sum(x * x, axis=-1, keepdims=True) * (1.0 / n) + NORM_EPS)
    return x * inv * gain


def _head_lane(d):
    half = QK_ROPE_DIM // 2
    if d >= QK_NOPE_DIM:
        r = d - QK_NOPE_DIM
        return r if r < half else LANES // 2 + (r - half)
    return half + d if d < LANES // 2 - half else LANES // 2 + half + (d - (LANES // 2 - half))


HEAD_LANES = [_head_lane(d) for d in range(QK_HEAD_DIM)]


def _head_norm_rope(x, ones_ref, gain, c, s):
    sq = x * x
    hi = sq.astype(BF16)
    lo = (sq - hi.astype(F32)).astype(BF16)
    ss = (jnp.dot(hi, ones_ref[...], preferred_element_type=F32)
          + jnp.dot(lo, ones_ref[...], preferred_element_type=F32))
    xn = x * lax.rsqrt(ss * (1.0 / QK_HEAD_DIM) + NORM_EPS) * gain
    out = []
    for j in range(x.shape[1] // LANES):
        blk = xn[:, j * LANES:(j + 1) * LANES]
        out.append(blk * c + pltpu.roll(blk, LANES // 2, axis=1) * s)
    return out


def _rope_table_kernel(pos_ref, freq_ref, c_ref, s_ref):
    ang = pos_ref[0].astype(F32) * freq_ref[...]
    lane = lax.broadcasted_iota(jnp.int32, ang.shape, 1)
    c_ref[0] = jnp.cos(ang)
    s_ref[0] = jnp.where(lane < LANES // 2, -jnp.sin(ang), jnp.sin(ang))


def _pre_kernel(h_ref, c_ref, s_ref, ng_ref, win_ref, wg_ref, ps_ref, wpu_ref, gqa_ref, gkva_ref,
                wqb_ref, wkvb_ref, gqn_ref, gkn_ref, ones_ref,
                q_ref, k_ref, v_ref, sz_ref, sg_ref, pc_ref, halo_ref):
    i = pl.program_id(1)
    t = h_ref.shape[1]
    x = h_ref[0]
    hn = _rms(x, ng_ref[...], D_MODEL).astype(BF16)

    def proj(lo, hi):
        return jnp.dot(hn, win_ref[:, lo:hi], preferred_element_type=F32)

    @pl.when(i == 0)
    def _():
        halo_ref[...] = jnp.zeros_like(halo_ref)

    u = proj(C_U, C_Z)
    uext = jnp.concatenate([halo_ref[...], u], axis=0)
    halo_ref[...] = u[t - HALO:, :]
    row = i * t + lax.broadcasted_iota(jnp.int32, (t, 1), 0)
    avail = jnp.maximum(row - (PAD_FRONT - 1), 1).astype(F32)
    mixed = []
    for g, w in enumerate(POOL_WINDOWS):
        a = uext[:, g * LANES:(g + 1) * LANES]
        ug = a[HALO:, :]
        step = 1
        while step < w:
            a = a + pltpu.roll(a, step, axis=0)
            step *= 2
        inv_cnt = 1.0 / jnp.minimum(avail, float(w))
        mixed.append((a[HALO:, :] * inv_cnt - ug).astype(BF16))
    ymix = jnp.dot(jnp.concatenate(mixed, axis=1), wg_ref[...], preferred_element_type=F32) * ps_ref[...]
    zp = proj(C_Z, C_Q)
    pooled = (ymix * (zp * jax.nn.sigmoid(zp))).astype(BF16)
    y_pool = jnp.dot(pooled, wpu_ref[...], preferred_element_type=F32)
    pc_ref[0] = (jax.nn.sigmoid(proj(C_GP, C_GM)) * y_pool).astype(pc_ref.dtype)

    zm = proj(C_ZM, C_GP)
    sz_ref[0] = (zm * jax.nn.sigmoid(zm)).astype(sz_ref.dtype)
    sg_ref[0] = jax.nn.sigmoid(proj(C_GM, C_KR)).astype(sg_ref.dtype)

    c, s = c_ref[0], s_ref[0]
    cqn = _rms(proj(C_Q, C_KV), gqa_ref[...], Q_LORA_RANK).astype(BF16)
    qp = jnp.dot(cqn, wqb_ref[...], preferred_element_type=F32)
    ckvn = _rms(proj(C_KV, C_ZM), gkva_ref[...], KV_LORA_RANK).astype(BF16)
    kv = jnp.dot(ckvn, wkvb_ref[...], preferred_element_type=F32)
    krp = proj(C_KR, C_END)
    krp2 = jnp.concatenate([krp, krp], axis=1)
    width = 2 * HEAD_PAD
    for pair in range(N_HEADS // 2):
        lo = pair * width
        qs = _head_norm_rope(qp[:, lo:lo + width], ones_ref, gqn_ref[...], c, s)
        ks = _head_norm_rope(kv[:, lo:lo + width] + krp2, ones_ref, gkn_ref[...], c, s)
        for odd in range(2):
            q_ref[0, 2 * pair + odd] = qs[odd].T.astype(BF16)
            k_ref[0, 2 * pair + odd] = ks[odd].astype(BF16)
        v0 = N_HEADS * HEAD_PAD + pair * LANES
        vbt = kv[:, v0:v0 + LANES].T.astype(BF16)
        for ch in range(t // ATT_TILE):
            v_ref[0, pair, ch] = vbt[:, ch * ATT_TILE:(ch + 1) * ATT_TILE]


def _attn_kernel(bounded_ref, qt_ref, k_ref, vt_ref, o_ref, m_ref, l_ref, acc_ref):
    qi = pl.program_id(2)
    tq = qt_ref.shape[3]
    ck = vt_ref.shape[4]
    heads = range(qt_ref.shape[1])

    def span_scores(c0, n):
        start = pl.multiple_of(c0 * ck, ck)
        return [jnp.dot(k_ref[0, hh, pl.ds(start, n * ck), :], qt_ref[0, hh], preferred_element_type=F32)
                for hh in heads]

    def causal(c0, shape):
        kpos = c0 * ck + lax.broadcasted_iota(jnp.int32, shape, 0)
        qpos = qi * tq + lax.broadcasted_iota(jnp.int32, shape, 1)
        return kpos <= qpos

    def weighted_values(c0, n, hh, p):
        pb = p.astype(BF16)
        lo = (hh % 2) * V_HEAD_DIM
        out = None
        for c in range(n):
            vt = vt_ref[0, hh // 2, c0 + c][lo:lo + V_HEAD_DIM, :]
            d = jnp.dot(vt, pb[c * ck:(c + 1) * ck, :], preferred_element_type=F32)
            out = d if out is None else out + d
        return out

    def step_bounded(c0, n, masked):
        scores = span_scores(c0, n)
        for hh in heads:
            p = jnp.exp2(scores[hh])
            if masked:
                p = jnp.where(causal(c0, p.shape), p, 0.0)
            l_ref[hh] += jnp.sum(p.reshape(n * ck // 8, 8, tq), axis=0)
            acc_ref[hh] += weighted_values(c0, n, hh, p)

    def step_general(c0, n, masked):
        scores = span_scores(c0, n)
        for hh in heads:
            s = scores[hh]
            if masked:
                s = jnp.where(causal(c0, s.shape), s, MASK_VALUE)
            m_prev = m_ref[hh]
            m_new = jnp.maximum(m_prev, jnp.max(s, axis=0, keepdims=True))
            alpha = jnp.exp2(m_prev - m_new)
            p = jnp.exp2(s - m_new)
            l_ref[hh, 0:1, :] = alpha * l_ref[hh, 0:1, :] + jnp.sum(p, axis=0, keepdims=True)
            acc_ref[hh] = alpha * acc_ref[hh] + weighted_values(c0, n, hh, p)
            m_ref[hh] = m_new

    def run(step):
        m_ref[...] = jnp.full_like(m_ref, -jnp.inf)
        l_ref[...] = jnp.zeros_like(l_ref)
        acc_ref[...] = jnp.zeros_like(acc_ref)

        def body(g, carry):
            step(g * KV_GROUP, KV_GROUP, False)
            return carry

        lax.fori_loop(0, qi // KV_GROUP, body, 0)
        for r in range(KV_GROUP):
            @pl.when(qi % KV_GROUP == r)
            def _(r=r):
                step(qi - r, r + 1, True)
                out_t = jnp.concatenate([acc_ref[hh] / jnp.sum(l_ref[hh], axis=0, keepdims=True) for hh in heads],
                                        axis=0)
                o_ref[0] = out_t.T.astype(o_ref.dtype)

    bounded = bounded_ref[0] != 0

    @pl.when(bounded)
    def _():
        run(step_bounded)

    @pl.when(jnp.logical_not(bounded))
    def _():
        run(step_general)


def _post_kernel(h_ref, o_ref, sz_ref, sg_ref, pc_ref, wmu_ref, wout_ref, out_ref):
    om = (o_ref[0].astype(F32) * sz_ref[0].astype(F32)).astype(BF16)
    y_mla = jnp.dot(om, wmu_ref[...], preferred_element_type=F32)
    merged = (pc_ref[0].astype(F32) + sg_ref[0].astype(F32) * y_mla).astype(BF16)
    out_ref[0] = h_ref[0] + jnp.dot(merged, wout_ref[...], preferred_element_type=F32)


def _const_spec(shape):
    return pl.BlockSpec(shape, lambda *_: (0,) * len(shape))


def _row_spec(t, width):
    return pl.BlockSpec((1, t, width), lambda b, i: (b, i, 0))


def _rope_tables(pos_pad, freq_lane):
    b, lp = pos_pad.shape
    t = ROW_TILE
    out = jax.ShapeDtypeStruct((b, lp, LANES), F32)
    return pl.pallas_call(
        _rope_table_kernel,
        grid=(b, lp // t),
        in_specs=[_row_spec(t, 1), _const_spec((1, LANES))],
        out_specs=[_row_spec(t, LANES)] * 2,
        out_shape=[out] * 2,
        name="rope_tables",
    )(pos_pad[:, :, None], freq_lane)


def _pre(h, tabs, lw):
    b, lp, _ = h.shape
    t = ROW_TILE
    nch = t // ATT_TILE
    k_shape = jax.ShapeDtypeStruct((b, N_HEADS, lp, HEAD_PAD), BF16)
    k_spec = pl.BlockSpec((1, N_HEADS, t, HEAD_PAD), lambda bb, i: (bb, 0, i, 0))
    qt_shape = jax.ShapeDtypeStruct((b, N_HEADS, HEAD_PAD, lp), BF16)
    qt_spec = pl.BlockSpec((1, N_HEADS, HEAD_PAD, t), lambda bb, i: (bb, 0, 0, i))
    vt_shape = jax.ShapeDtypeStruct((b, N_HEADS // 2, lp // ATT_TILE, 2 * V_HEAD_DIM, ATT_TILE), BF16)
    vt_spec = pl.BlockSpec((1, N_HEADS // 2, nch, 2 * V_HEAD_DIM, ATT_TILE), lambda bb, i: (bb, 0, i, 0, 0))
    weights = [lw["ng"], lw["win"], lw["wg"], lw["ps"], lw["wpu"], lw["gqa"], lw["gkva"], lw["wqb"], lw["wkvb"],
               lw["gqn"], lw["gkn"], lw["ones"]]
    return pl.pallas_call(
        _pre_kernel,
        grid=(b, lp // t),
        in_specs=[_row_spec(t, D_MODEL)] + [_row_spec(t, LANES)] * 2 + [_const_spec(w.shape) for w in weights],
        out_specs=[qt_spec, k_spec, vt_spec, _row_spec(t, MLA_WIDTH), _row_spec(t, D_MODEL),
                   _row_spec(t, D_MODEL)],
        out_shape=[qt_shape, k_shape, vt_shape,
                   jax.ShapeDtypeStruct((b, lp, MLA_WIDTH), BF16),
                   jax.ShapeDtypeStruct((b, lp, D_MODEL), BF16),
                   jax.ShapeDtypeStruct((b, lp, D_MODEL), BF16)],
        scratch_shapes=[pltpu.VMEM((HALO, POOL_WIDTH), F32)],
        compiler_params=pltpu.CompilerParams(dimension_semantics=("parallel", "arbitrary"),
                                             vmem_limit_bytes=VMEM_LIMIT),
        name="pre",
    )(h, *tabs, *weights)


def _scores_bounded(q_gain, k_gain):
    limit = (QK_HEAD_DIM * jnp.max(jnp.abs(q_gain)) * jnp.max(jnp.abs(k_gain))
             * (math.log2(math.e) / math.sqrt(QK_HEAD_DIM)) * 1.02)
    return (limit <= SCORE_BOUND).astype(jnp.int32).reshape(1)


def _attention(bounded, qt, k, vt):
    b, nh, lp, _ = k.shape
    t = ATT_TILE
    ah = ATT_HEADS
    resident = pl.Buffered(1)
    grid_spec = pltpu.PrefetchScalarGridSpec(
        num_scalar_prefetch=1,
        grid=(b, nh // ah, lp // t),
        in_specs=[pl.BlockSpec((1, ah, HEAD_PAD, t), lambda bb, p, i, f: (bb, p, 0, i)),
                  pl.BlockSpec((1, ah, lp, HEAD_PAD), lambda bb, p, i, f: (bb, p, 0, 0), pipeline_mode=resident),
                  pl.BlockSpec((1, ah // 2, lp // t, 2 * V_HEAD_DIM, t), lambda bb, p, i, f: (bb, p, 0, 0, 0),
                               pipeline_mode=resident)],
        out_specs=pl.BlockSpec((1, t, ah * V_HEAD_DIM), lambda bb, p, i, f: (bb, i, p)),
        scratch_shapes=[pltpu.VMEM((ah, 1, t), F32),
                        pltpu.VMEM((ah, 8, t), F32),
                        pltpu.VMEM((ah, V_HEAD_DIM, t), F32)],
    )
    return pl.pallas_call(
        _attn_kernel,
        grid_spec=grid_spec,
        out_shape=jax.ShapeDtypeStruct((b, lp, MLA_WIDTH), BF16),
        compiler_params=pltpu.CompilerParams(dimension_semantics=("parallel", "parallel", "arbitrary"),
                                             vmem_limit_bytes=VMEM_LIMIT),
        name="attn",
    )(bounded, qt, k, vt)


def _post(h, o, sz, sg, pc, lw):
    b, lp, _ = h.shape
    t = ROW_TILE
    return pl.pallas_call(
        _post_kernel,
        grid=(b, lp // t),
        in_specs=[_row_spec(t, D_MODEL), _row_spec(t, MLA_WIDTH), _row_spec(t, MLA_WIDTH), _row_spec(t, D_MODEL),
                  _row_spec(t, D_MODEL), _const_spec(lw["wmu"].shape), _const_spec(lw["wout"].shape)],
        out_specs=_row_spec(t, D_MODEL),
        out_shape=jax.ShapeDtypeStruct(h.shape, F32),
        input_output_aliases={0: 0},
        compiler_params=pltpu.CompilerParams(dimension_semantics=("parallel", "parallel"),
                                             vmem_limit_bytes=VMEM_LIMIT),
        name="post",
    )(h, o, sz, sg, pc, lw["wmu"], lw["wout"])


def _to_head_lanes(a):
    lanes = jnp.asarray(HEAD_LANES[:a.shape[-1]])
    return jnp.zeros(a.shape[:-1] + (HEAD_PAD,), a.dtype).at[..., lanes].set(a)


def _layer_weights(l, norm_gain, w_in, pool_w_group, pool_scale, pool_w_up, q_a_norm_gain, kv_a_norm_gain, w_q_b,
                   w_kv_b, q_norm_gain, k_norm_gain, mla_w_up, w_out):
    wi = w_in[l]
    u, z, cq, ckv, kr, zm, gp, gm = jnp.split(wi, (512, 1024, 1792, 2048, 2080, 2592, 3616), axis=1)
    kr_block = _to_head_lanes(jnp.concatenate([jnp.zeros((D_MODEL, QK_NOPE_DIM), kr.dtype), kr], axis=1))
    win = jnp.concatenate([u, z, cq, ckv, zm, gp, gm, kr_block], axis=1).astype(BF16)
    wqb = _to_head_lanes(w_q_b[l].reshape(Q_LORA_RANK, N_HEADS, QK_HEAD_DIM)).reshape(Q_LORA_RANK, -1)
    wkv = w_kv_b[l].reshape(KV_LORA_RANK, N_HEADS, QK_NOPE_DIM + V_HEAD_DIM)
    wkvb = jnp.concatenate([_to_head_lanes(wkv[:, :, :QK_NOPE_DIM]).reshape(KV_LORA_RANK, -1),
                            wkv[:, :, QK_NOPE_DIM:].reshape(KV_LORA_RANK, -1)], axis=1)
    q_scale = math.log2(math.e) / math.sqrt(QK_HEAD_DIM)
    block = jnp.arange(2 * HEAD_PAD) // HEAD_PAD
    return {
        "ones": (block[:, None] == block[None, :]).astype(BF16),
        "ng": norm_gain[l][None, :],
        "win": win,
        "wg": jax.scipy.linalg.block_diag(*pool_w_group[l]).astype(BF16),
        "ps": pool_scale[l][None, :],
        "wpu": pool_w_up[l].astype(BF16),
        "gqa": q_a_norm_gain[l][None, :],
        "gkva": kv_a_norm_gain[l][None, :],
        "wqb": wqb.astype(BF16),
        "wkvb": wkvb.astype(BF16),
        "gqn": jnp.tile(_to_head_lanes(q_norm_gain[l] * q_scale), 2)[None, :],
        "gkn": jnp.tile(_to_head_lanes(k_norm_gain[l]), 2)[None, :],
        "wmu": mla_w_up[l].astype(BF16),
        "wout": w_out[l].astype(BF16),
    }


def kernel(x, positions, meta_tokens, norm_gain, w_in, pool_w_group, pool_scale, pool_w_up, q_a_norm_gain,
           kv_a_norm_gain, w_q_b, w_kv_b, q_norm_gain, k_norm_gain, mla_w_up, w_out):
    b, seq, _ = x.shape
    lp = PAD_FRONT + N_META + seq + PAD_BACK
    assert lp % ROW_TILE == 0 and ROW_TILE % ATT_TILE == 0

    meta = jnp.broadcast_to(meta_tokens[None].astype(x.dtype), (b, N_META, D_MODEL))
    h = jnp.concatenate([jnp.zeros((b, PAD_FRONT, D_MODEL), x.dtype), meta, x,
                         jnp.zeros((b, PAD_BACK, D_MODEL), x.dtype)], axis=1)
    meta_pos = jnp.broadcast_to(jnp.arange(N_META, dtype=jnp.int32)[None], (b, N_META))
    pos = jnp.concatenate([jnp.zeros((b, PAD_FRONT), jnp.int32), meta_pos, positions + N_META,
                           jnp.zeros((b, PAD_BACK), jnp.int32)], axis=1)

    half = QK_ROPE_DIM // 2
    inv_freq = ROPE_THETA ** (-jnp.arange(half, dtype=F32) / half)
    freq_lane = _to_head_lanes(jnp.concatenate([jnp.zeros((QK_NOPE_DIM,), F32), inv_freq, inv_freq]))[None, :]
    tabs = _rope_tables(pos, freq_lane)

    for l in range(DEPTH):
        lw = _layer_weights(l, norm_gain, w_in, pool_w_group, pool_scale, pool_w_up, q_a_norm_gain, kv_a_norm_gain,
                            w_q_b, w_kv_b, q_norm_gain, k_norm_gain, mla_w_up, w_out)
        q, k, v, sz, sg, pc = _pre(h, tabs, lw)
        o = _attention(_scores_bounded(q_norm_gain[l], k_norm_gain[l]), q, k, v)
        h = _post(h, o, sz, sg, pc, lw)

    return h[:, PAD_FRONT + N_META:PAD_FRONT + N_META + seq]
```

```python
import functools
import math

import jax
import jax.numpy as jnp
from jax import lax
from jax.experimental import pallas as pl
from jax.experimental.pallas import tpu as pltpu

F32 = jnp.float32
BF16 = jnp.bfloat16

D_MODEL = 1024
DEPTH = 4
N_META = 16
POOL_WIDTH = 512
POOL_WINDOWS = (2, 4, 8, 16)
POOL_GROUP_DIM = 128
N_HEADS = 8
QK_NOPE_DIM = 64
QK_ROPE_DIM = 32
QK_HEAD_DIM = 96
V_HEAD_DIM = 64
MLA_WIDTH = 512
KV_LORA_RANK = 256
Q_LORA_RANK = 768
ROPE_THETA = 10000.0
NORM_EPS = 1e-6
MASK_VALUE = -1e30

LANES = 128
HEAD_PAD = LANES
PAD_FRONT = 0
PAD_BACK = 240
ROW_TILE = 768
ATT_TILE = 256
KV_GROUP = 4
SCORE_BOUND = 32.0
ATT_HEADS = 8
HALO = 16
VMEM_LIMIT = 56 * 1024 * 1024

C_U, C_Z, C_Q, C_KV, C_ZM, C_GP, C_GM, C_KR, C_END = 0, 512, 1024, 1792, 2048, 2560, 3584, 4608, 4736


def _rms(x, gain, n):
    inv = lax.rsqrt(jnp.sum(x * x, axis=-1, keepdims=True) * (1.0 / n) + NORM_EPS)
    return x * inv * gain


def _head_lane(d):
    half = QK_ROPE_DIM // 2
    if d >= QK_NOPE_DIM:
        r = d - QK_NOPE_DIM
        return r if r < half else LANES // 2 + (r - half)
    return half + d if d < LANES // 2 - half else LANES // 2 + half + (d - (LANES // 2 - half))


HEAD_LANES = [_head_lane(d) for d in range(QK_HEAD_DIM)]


def _head_norm_rope(x, ones_ref, gain, c, s):
    sq = x * x
    hi = sq.astype(BF16)
    lo = (sq - hi.astype(F32)).astype(BF16)
    ss = (jnp.dot(hi, ones_ref[...], preferred_element_type=F32)
          + jnp.dot(lo, ones_ref[...], preferred_element_type=F32))
    xn = x * lax.rsqrt(ss * (1.0 / QK_HEAD_DIM) + NORM_EPS) * gain
    out = []
    for j in range(x.shape[1] // LANES):
        blk = xn[:, j * LANES:(j + 1) * LANES]
        out.append(blk * c + pltpu.roll(blk, LANES // 2, axis=1) * s)
    return out


def _rope_table_kernel(pos_ref, freq_ref, c_ref, s_ref):
    ang = pos_ref[0].astype(F32) * freq_ref[...]
    lane = lax.broadcasted_iota(jnp.int32, ang.shape, 1)
    c_ref[0] = jnp.cos(ang)
    s_ref[0] = jnp.where(lane < LANES // 2, -jnp.sin(ang), jnp.sin(ang))


def _pre_kernel(h_ref, c_ref, s_ref, ng_ref, win_ref, wg_ref, ps_ref, wpu_ref, gqa_ref, gkva_ref,
                wqb_ref, wkvb_ref, gqn_ref, gkn_ref, ones_ref,
                q_ref, k_ref, v_ref, sz_ref, sg_ref, pc_ref, halo_ref):
    i = pl.program_id(1)
    t = h_ref.shape[1]
    x = h_ref[0]
    hn = _rms(x, ng_ref[...], D_MODEL).astype(BF16)

    def proj(lo, hi):
        return jnp.dot(hn, win_ref[:, lo:hi], preferred_element_type=F32)

    c, s = c_ref[0], s_ref[0]
    cqn = _rms(proj(C_Q, C_KV), gqa_ref[...], Q_LORA_RANK).astype(BF16)
    qp = jnp.dot(cqn, wqb_ref[...], preferred_element_type=F32)
    ckvn = _rms(proj(C_KV, C_ZM), gkva_ref[...], KV_LORA_RANK).astype(BF16)
    kv = jnp.dot(ckvn, wkvb_ref[...], preferred_element_type=F32)
    krp = proj(C_KR, C_END)
    krp2 = jnp.concatenate([krp, krp], axis=1)
    width = 2 * HEAD_PAD
    for pair in range(N_HEADS // 2):
        lo = pair * width
        qs = _head_norm_rope(qp[:, lo:lo + width], ones_ref, gqn_ref[...], c, s)
        ks = _head_norm_rope(kv[:, lo:lo + width] + krp2, ones_ref, gkn_ref[...], c, s)
        for odd in range(2):
            q_ref[0, 2 * pair + odd] = qs[odd].T.astype(BF16)
            k_ref[0, 2 * pair + odd] = ks[odd].astype(BF16)
        v0 = N_HEADS * HEAD_PAD + pair * LANES
        vbt = kv[:, v0:v0 + LANES].T.astype(BF16)
        for ch in range(t // ATT_TILE):
            v_ref[0, pair, ch] = vbt[:, ch * ATT_TILE:(ch + 1) * ATT_TILE]

    @pl.when(i == 0)
    def _():
        halo_ref[...] = jnp.zeros_like(halo_ref)

    u = proj(C_U, C_Z)
    uext = jnp.concatenate([halo_ref[...], u], axis=0)
    halo_ref[...] = u[t - HALO:, :]
    row = i * t + lax.broadcasted_iota(jnp.int32, (t, 1), 0)
    avail = jnp.maximum(row - (PAD_FRONT - 1), 1).astype(F32)
    mixed = []
    for g, w in enumerate(POOL_WINDOWS):
        a = uext[:, g * LANES:(g + 1) * LANES]
        ug = a[HALO:, :]
        step = 1
        while step < w:
            a = a + pltpu.roll(a, step, axis=0)
            step *= 2
        inv_cnt = 1.0 / jnp.minimum(avail, float(w))
        mixed.append((a[HALO:, :] * inv_cnt - ug).astype(BF16))
    ymix = jnp.dot(jnp.concatenate(mixed, axis=1), wg_ref[...], preferred_element_type=F32) * ps_ref[...]
    zp = proj(C_Z, C_Q)
    pooled = (ymix * (zp * jax.nn.sigmoid(zp))).astype(BF16)
    y_pool = jnp.dot(pooled, wpu_ref[...], preferred_element_type=F32)
    pc_ref[0] = (jax.nn.sigmoid(proj(C_GP, C_GM)) * y_pool).astype(pc_ref.dtype)

    zm = proj(C_ZM, C_GP)
    sz_ref[0] = (zm * jax.nn.sigmoid(zm)).astype(sz_ref.dtype)
    sg_ref[0] = jax.nn.sigmoid(proj(C_GM, C_KR)).astype(sg_ref.dtype)


def _attn_kernel(bounded_ref, qt_ref, k_ref, vt_ref, o_ref, m_ref, l_ref, acc_ref, *, layer):
    qi = pl.program_id(2)
    tq = qt_ref.shape[3]
    ck = vt_ref.shape[4]
    heads = range(qt_ref.shape[1])

    def span_scores(c0, n):
        start = pl.multiple_of(c0 * ck, ck)
        return [jnp.dot(k_ref[0, hh, pl.ds(start, n * ck), :], qt_ref[0, hh], preferred_element_type=F32)
                for hh in heads]

    def causal(c0, shape):
        kpos = c0 * ck + lax.broadcasted_iota(jnp.int32, shape, 0)
        qpos = qi * tq + lax.broadcasted_iota(jnp.int32, shape, 1)
        return kpos <= qpos

    def weighted_values(c0, n, hh, p):
        pb = p.astype(BF16)
        lo = (hh % 2) * V_HEAD_DIM
        out = None
        for c in range(n):
            vt = vt_ref[0, hh // 2, c0 + c][lo:lo + V_HEAD_DIM, :]
            d = jnp.dot(vt, pb[c * ck:(c + 1) * ck, :], preferred_element_type=F32)
            out = d if out is None else out + d
        return out

    def step_bounded(c0, n, masked):
        scores = span_scores(c0, n)
        for hh in heads:
            p = jnp.exp2(scores[hh])
            if masked:
                p = jnp.where(causal(c0, p.shape), p, 0.0)
            l_ref[hh] += jnp.sum(p.reshape(n * ck // 8, 8, tq), axis=0)
            acc_ref[hh] += weighted_values(c0, n, hh, p)

    def step_general(c0, n, masked):
        scores = span_scores(c0, n)
        for hh in heads:
            s = scores[hh]
            if masked:
                s = jnp.where(causal(c0, s.shape), s, MASK_VALUE)
            m_prev = m_ref[hh]
            m_new = jnp.maximum(m_prev, jnp.max(s, axis=0, keepdims=True))
            alpha = jnp.exp2(m_prev - m_new)
            p = jnp.exp2(s - m_new)
            l_ref[hh, 0:1, :] = alpha * l_ref[hh, 0:1, :] + jnp.sum(p, axis=0, keepdims=True)
            acc_ref[hh] = alpha * acc_ref[hh] + weighted_values(c0, n, hh, p)
            m_ref[hh] = m_new

    def run(step):
        m_ref[...] = jnp.full_like(m_ref, -jnp.inf)
        l_ref[...] = jnp.zeros_like(l_ref)
        acc_ref[...] = jnp.zeros_like(acc_ref)

        def body(g, carry):
            step(g * KV_GROUP, KV_GROUP, False)
            return carry

        lax.fori_loop(0, qi // KV_GROUP, body, 0)
        for r in range(KV_GROUP):
            @pl.when(qi % KV_GROUP == r)
            def _(r=r):
                step(qi - r, r + 1, True)
                out_t = jnp.concatenate([acc_ref[hh] / jnp.sum(l_ref[hh], axis=0, keepdims=True) for hh in heads],
                                        axis=0)
                o_ref[0] = out_t.T.astype(o_ref.dtype)

    bounded = bounded_ref[layer] != 0

    @pl.when(bounded)
    def _():
        run(step_bounded)

    @pl.when(jnp.logical_not(bounded))
    def _():
        run(step_general)


def _post_kernel(h_ref, o_ref, sz_ref, sg_ref, pc_ref, wmu_ref, wout_ref, out_ref):
    om = (o_ref[0].astype(F32) * sz_ref[0].astype(F32)).astype(BF16)
    y_mla = jnp.dot(om, wmu_ref[...], preferred_element_type=F32)
    merged = (pc_ref[0].astype(F32) + sg_ref[0].astype(F32) * y_mla).astype(BF16)
    out_ref[0] = h_ref[0] + jnp.dot(merged, wout_ref[...], preferred_element_type=F32)


def _const_spec(shape):
    return pl.BlockSpec(shape, lambda *_: (0,) * len(shape))


def _layer_spec(stacked, layer):
    rest = stacked.shape[1:]
    return pl.BlockSpec((None,) + rest, lambda *_: (layer,) + (0,) * len(rest))


def _row_spec(t, width):
    return pl.BlockSpec((1, t, width), lambda b, i: (b, i, 0))


def _rope_tables(pos_pad, freq_lane):
    b, lp = pos_pad.shape
    t = ROW_TILE
    out = jax.ShapeDtypeStruct((b, lp, LANES), F32)
    return pl.pallas_call(
        _rope_table_kernel,
        grid=(b, lp // t),
        in_specs=[_row_spec(t, 1), _const_spec((1, LANES))],
        out_specs=[_row_spec(t, LANES)] * 2,
        out_shape=[out] * 2,
        name="rope_tables",
    )(pos_pad[:, :, None], freq_lane)


def _pre(h, tabs, wts, layer):
    b, lp, _ = h.shape
    t = ROW_TILE
    nch = t // ATT_TILE
    k_shape = jax.ShapeDtypeStruct((b, N_HEADS, lp, HEAD_PAD), BF16)
    k_spec = pl.BlockSpec((1, N_HEADS, t, HEAD_PAD), lambda bb, i: (bb, 0, i, 0))
    qt_shape = jax.ShapeDtypeStruct((b, N_HEADS, HEAD_PAD, lp), BF16)
    qt_spec = pl.BlockSpec((1, N_HEADS, HEAD_PAD, t), lambda bb, i: (bb, 0, 0, i))
    vt_shape = jax.ShapeDtypeStruct((b, N_HEADS // 2, lp // ATT_TILE, 2 * V_HEAD_DIM, ATT_TILE), BF16)
    vt_spec = pl.BlockSpec((1, N_HEADS // 2, nch, 2 * V_HEAD_DIM, ATT_TILE), lambda bb, i: (bb, 0, i, 0, 0))
    weights = [wts[name] for name in ("ng", "win", "wg", "ps", "wpu", "gqa", "gkva", "wqb", "wkvb", "gqn", "gkn")]
    return pl.pallas_call(
        _pre_kernel,
        grid=(b, lp // t),
        in_specs=([_row_spec(t, D_MODEL)] + [_row_spec(t, LANES)] * 2 + [_layer_spec(w, layer) for w in weights]
                  + [_const_spec(wts["ones"].shape)]),
        out_specs=[qt_spec, k_spec, vt_spec, _row_spec(t, MLA_WIDTH), _row_spec(t, D_MODEL),
                   _row_spec(t, D_MODEL)],
        out_shape=[qt_shape, k_shape, vt_shape,
                   jax.ShapeDtypeStruct((b, lp, MLA_WIDTH), BF16),
                   jax.ShapeDtypeStruct((b, lp, D_MODEL), BF16),
                   jax.ShapeDtypeStruct((b, lp, D_MODEL), BF16)],
        scratch_shapes=[pltpu.VMEM((HALO, POOL_WIDTH), F32)],
        compiler_params=pltpu.CompilerParams(dimension_semantics=("parallel", "arbitrary"),
                                             vmem_limit_bytes=VMEM_LIMIT),
        name="pre",
    )(h, *tabs, *weights, wts["ones"])


def _scores_bounded(q_gain, k_gain):
    limit = (QK_HEAD_DIM * jnp.max(jnp.abs(q_gain), axis=-1) * jnp.max(jnp.abs(k_gain), axis=-1)
             * (math.log2(math.e) / math.sqrt(QK_HEAD_DIM)) * 1.02)
    return (limit <= SCORE_BOUND).astype(jnp.int32)


def _attention(bounded, layer, qt, k, vt):
    b, nh, lp, _ = k.shape
    t = ATT_TILE
    ah = ATT_HEADS
    resident = pl.Buffered(1)
    grid_spec = pltpu.PrefetchScalarGridSpec(
        num_scalar_prefetch=1,
        grid=(b, nh // ah, lp // t),
        in_specs=[pl.BlockSpec((1, ah, HEAD_PAD, t), lambda bb, p, i, f: (bb, p, 0, i)),
                  pl.BlockSpec((1, ah, lp, HEAD_PAD), lambda bb, p, i, f: (bb, p, 0, 0), pipeline_mode=resident),
                  pl.BlockSpec((1, ah // 2, lp // t, 2 * V_HEAD_DIM, t), lambda bb, p, i, f: (bb, p, 0, 0, 0),
                               pipeline_mode=resident)],
        out_specs=pl.BlockSpec((1, t, ah * V_HEAD_DIM), lambda bb, p, i, f: (bb, i, p)),
        scratch_shapes=[pltpu.VMEM((ah, 1, t), F32),
                        pltpu.VMEM((ah, 8, t), F32),
                        pltpu.VMEM((ah, V_HEAD_DIM, t), F32)],
    )
    return pl.pallas_call(
        functools.partial(_attn_kernel, layer=layer),
        grid_spec=grid_spec,
        out_shape=jax.ShapeDtypeStruct((b, lp, MLA_WIDTH), BF16),
        compiler_params=pltpu.CompilerParams(dimension_semantics=("parallel", "parallel", "arbitrary"),
                                             vmem_limit_bytes=VMEM_LIMIT),
        name="attn",
    )(bounded, qt, k, vt)


def _post(h, o, sz, sg, pc, wts, layer):
    b, lp, _ = h.shape
    t = ROW_TILE
    return pl.pallas_call(
        _post_kernel,
        grid=(b, lp // t),
        in_specs=[_row_spec(t, D_MODEL), _row_spec(t, MLA_WIDTH), _row_spec(t, MLA_WIDTH), _row_spec(t, D_MODEL),
                  _row_spec(t, D_MODEL), _layer_spec(wts["wmu"], layer), _layer_spec(wts["wout"], layer)],
        out_specs=_row_spec(t, D_MODEL),
        out_shape=jax.ShapeDtypeStruct(h.shape, F32),
        input_output_aliases={0: 0},
        compiler_params=pltpu.CompilerParams(dimension_semantics=("parallel", "parallel"),
                                             vmem_limit_bytes=VMEM_LIMIT),
        name="post",
    )(h, o, sz, sg, pc, wts["wmu"], wts["wout"])


def _to_head_lanes(a):
    n = a.shape[-1]
    half = QK_ROPE_DIM // 2
    split = LANES // 2 - half
    zeros = lambda w: jnp.zeros(a.shape[:-1] + (w,), a.dtype)
    rot1 = a[..., QK_NOPE_DIM:QK_NOPE_DIM + half] if n > QK_NOPE_DIM else zeros(half)
    rot2 = a[..., QK_NOPE_DIM + half:] if n > QK_NOPE_DIM else zeros(half)
    assert [HEAD_LANES[d] for d in (0, split, QK_NOPE_DIM, QK_NOPE_DIM + half)] == [half, LANES // 2 + half, 0,
                                                                                     LANES // 2]
    return jnp.concatenate([rot1, a[..., :split], rot2, a[..., split:QK_NOPE_DIM], zeros(HEAD_PAD - QK_HEAD_DIM)],
                           axis=-1)


def _prepare_weights(norm_gain, w_in, pool_w_group, pool_scale, pool_w_up, q_a_norm_gain, kv_a_norm_gain, w_q_b,
                     w_kv_b, q_norm_gain, k_norm_gain, mla_w_up, w_out):
    depth = w_in.shape[0]
    u, z, cq, ckv, kr, zm, gp, gm = jnp.split(w_in, (512, 1024, 1792, 2048, 2080, 2592, 3616), axis=2)
    kr_block = _to_head_lanes(jnp.concatenate([jnp.zeros(kr.shape[:-1] + (QK_NOPE_DIM,), kr.dtype), kr], axis=-1))
    win = jnp.concatenate([u, z, cq, ckv, zm, gp, gm, kr_block], axis=2).astype(BF16)
    wqb = _to_head_lanes(w_q_b.reshape(depth, Q_LORA_RANK, N_HEADS, QK_HEAD_DIM)).reshape(depth, Q_LORA_RANK, -1)
    wkv = w_kv_b.reshape(depth, KV_LORA_RANK, N_HEADS, QK_NOPE_DIM + V_HEAD_DIM)
    wkvb = jnp.concatenate([_to_head_lanes(wkv[..., :QK_NOPE_DIM]).reshape(depth, KV_LORA_RANK, -1),
                            wkv[..., QK_NOPE_DIM:].reshape(depth, KV_LORA_RANK, -1)], axis=2)
    eye = jnp.eye(pool_w_group.shape[1], dtype=pool_w_group.dtype)
    wg = (pool_w_group[:, :, :, None, :] * eye[None, :, None, :, None]).reshape(depth, POOL_WIDTH, POOL_WIDTH)
    q_scale = math.log2(math.e) / math.sqrt(QK_HEAD_DIM)
    block = jnp.arange(2 * HEAD_PAD) // HEAD_PAD
    row = lambda g: g[:, None, :]
    return {
        "ones": (block[:, None] == block[None, :]).astype(BF16),
        "ng": row(norm_gain),
        "win": win,
        "wg": wg.astype(BF16),
        "ps": row(pool_scale),
        "wpu": pool_w_up.astype(BF16),
        "gqa": row(q_a_norm_gain),
        "gkva": row(kv_a_norm_gain),
        "wqb": wqb.astype(BF16),
        "wkvb": wkvb.astype(BF16),
        "gqn": row(jnp.tile(_to_head_lanes(q_norm_gain * q_scale), (1, 2))),
        "gkn": row(jnp.tile(_to_head_lanes(k_norm_gain), (1, 2))),
        "wmu": mla_w_up.astype(BF16),
        "wout": w_out.astype(BF16),
    }


def kernel(x, positions, meta_tokens, norm_gain, w_in, pool_w_group, pool_scale, pool_w_up, q_a_norm_gain,
           kv_a_norm_gain, w_q_b, w_kv_b, q_norm_gain, k_norm_gain, mla_w_up, w_out):
    b, seq, _ = x.shape
    lp = PAD_FRONT + N_META + seq + PAD_BACK
    assert lp % ROW_TILE == 0 and ROW_TILE % ATT_TILE == 0

    meta = jnp.broadcast_to(meta_tokens[None].astype(x.dtype), (b, N_META, D_MODEL))
    h = jnp.concatenate([jnp.zeros((b, PAD_FRONT, D_MODEL), x.dtype), meta, x,
                         jnp.zeros((b, PAD_BACK, D_MODEL), x.dtype)], axis=1)
    meta_pos = jnp.broadcast_to(jnp.arange(N_META, dtype=jnp.int32)[None], (b, N_META))
    pos = jnp.concatenate([jnp.zeros((b, PAD_FRONT), jnp.int32), meta_pos, positions + N_META,
                           jnp.zeros((b, PAD_BACK), jnp.int32)], axis=1)

    half = QK_ROPE_DIM // 2
    inv_freq = ROPE_THETA ** (-jnp.arange(half, dtype=F32) / half)
    freq_lane = _to_head_lanes(jnp.concatenate([jnp.zeros((QK_NOPE_DIM,), F32), inv_freq, inv_freq]))[None, :]
    tabs = _rope_tables(pos, freq_lane)

    wts = _prepare_weights(norm_gain, w_in, pool_w_group, pool_scale, pool_w_up, q_a_norm_gain, kv_a_norm_gain,
                           w_q_b, w_kv_b, q_norm_gain, k_norm_gain, mla_w_up, w_out)
    bounded = _scores_bounded(q_norm_gain, k_norm_gain)
    for l in range(DEPTH):
        q, k, v, sz, sg, pc = _pre(h, tabs, wts, l)
        o = _attention(bounded, l, q, k, v)
        h = _post(h, o, sz, sg, pc, wts, l)

    return h[:, PAD_FRONT + N_META:PAD_FRONT + N_META + seq]
```

```python
import functools
import math

import jax
import jax.numpy as jnp
from jax import lax
from jax.experimental import pallas as pl
from jax.experimental.pallas import tpu as pltpu

F32 = jnp.float32
BF16 = jnp.bfloat16

D_MODEL = 1024
DEPTH = 4
N_META = 16
POOL_WIDTH = 512
POOL_WINDOWS = (2, 4, 8, 16)
POOL_GROUP_DIM = 128
N_HEADS = 8
QK_NOPE_DIM = 64
QK_ROPE_DIM = 32
QK_HEAD_DIM = 96
V_HEAD_DIM = 64
MLA_WIDTH = 512
KV_LORA_RANK = 256
Q_LORA_RANK = 768
ROPE_THETA = 10000.0
NORM_EPS = 1e-6
MASK_VALUE = -1e30

LANES = 128
HEAD_PAD = LANES
PAD_FRONT = 0
PAD_BACK = 240
ROW_TILE = 768
ATT_TILE = 256
KV_GROUPS = (8, 2)
SCORE_BOUND = 32.0
HALO = 16
VMEM_LIMIT = 56 * 1024 * 1024

C_U, C_Z, C_Q, C_KV, C_ZM, C_GP, C_GM, C_KR, C_END = 0, 512, 1024, 1792, 2048, 2560, 3584, 4608, 4736


def _rms(x, gain, n):
    inv = lax.rsqrt(jnp.sum(x * x, axis=-1, keepdims=True) * (1.0 / n) + NORM_EPS)
    return x * inv * gain


def _head_lane(d):
    half = QK_ROPE_DIM // 2
    if d >= QK_NOPE_DIM:
        r = d - QK_NOPE_DIM
        return r if r < half else LANES // 2 + (r - half)
    return half + d if d < LANES // 2 - half else LANES // 2 + half + (d - (LANES // 2 - half))


HEAD_LANES = [_head_lane(d) for d in range(QK_HEAD_DIM)]


def _head_norm_rope(x, ones_ref, gain, c, s):
    sq = x * x
    hi = sq.astype(BF16)
    lo = (sq - hi.astype(F32)).astype(BF16)
    ss = (jnp.dot(hi, ones_ref[...], preferred_element_type=F32)
          + jnp.dot(lo, ones_ref[...], preferred_element_type=F32))
    xn = x * lax.rsqrt(ss * (1.0 / QK_HEAD_DIM) + NORM_EPS) * gain
    out = []
    for j in range(x.shape[1] // LANES):
        blk = xn[:, j * LANES:(j + 1) * LANES]
        out.append(blk * c + pltpu.roll(blk, LANES // 2, axis=1) * s)
    return out


def _rope_table_kernel(pos_ref, freq_ref, c_ref, s_ref):
    ang = pos_ref[0].astype(F32) * freq_ref[...]
    lane = lax.broadcasted_iota(jnp.int32, ang.shape, 1)
    c_ref[0] = jnp.cos(ang)
    s_ref[0] = jnp.where(lane < LANES // 2, -jnp.sin(ang), jnp.sin(ang))


def _pre_kernel(h_ref, c_ref, s_ref, ng_ref, win_ref, wg_ref, ps_ref, wpu_ref, gqa_ref, gkva_ref,
                wqb_ref, wkvb_ref, gqn_ref, gkn_ref, ones_ref,
                q_ref, k_ref, v_ref, sz_ref, sg_ref, pc_ref, halo_ref):
    i = pl.program_id(1)
    t = h_ref.shape[1]
    x = h_ref[0]
    hn = _rms(x, ng_ref[...], D_MODEL).astype(BF16)

    def proj(lo, hi):
        return jnp.dot(hn, win_ref[:, lo:hi], preferred_element_type=F32)

    c, s = c_ref[0], s_ref[0]
    cqn = _rms(proj(C_Q, C_KV), gqa_ref[...], Q_LORA_RANK).astype(BF16)
    qp = jnp.dot(cqn, wqb_ref[...], preferred_element_type=F32)
    ckvn = _rms(proj(C_KV, C_ZM), gkva_ref[...], KV_LORA_RANK).astype(BF16)
    kv = jnp.dot(ckvn, wkvb_ref[...], preferred_element_type=F32)
    krp = proj(C_KR, C_END)
    krp2 = jnp.concatenate([krp, krp], axis=1)
    width = 2 * HEAD_PAD
    for pair in range(N_HEADS // 2):
        lo = pair * width
        qs = _head_norm_rope(qp[:, lo:lo + width], ones_ref, gqn_ref[...], c, s)
        ks = _head_norm_rope(kv[:, lo:lo + width] + krp2, ones_ref, gkn_ref[...], c, s)
        for odd in range(2):
            q_ref[0, 2 * pair + odd] = qs[odd].T.astype(BF16)
            k_ref[0, 2 * pair + odd] = ks[odd].astype(BF16)
        v0 = N_HEADS * HEAD_PAD + pair * LANES
        vbt = kv[:, v0:v0 + LANES].T.astype(BF16)
        for ch in range(t // ATT_TILE):
            v_ref[0, pair, ch] = vbt[:, ch * ATT_TILE:(ch + 1) * ATT_TILE]

    @pl.when(i == 0)
    def _():
        halo_ref[...] = jnp.zeros_like(halo_ref)

    u = proj(C_U, C_Z)
    uext = jnp.concatenate([halo_ref[...], u], axis=0)
    halo_ref[...] = u[t - HALO:, :]
    row = i * t + lax.broadcasted_iota(jnp.int32, (t, 1), 0)
    avail = jnp.maximum(row - (PAD_FRONT - 1), 1).astype(F32)
    mixed = []
    for g, w in enumerate(POOL_WINDOWS):
        a = uext[:, g * LANES:(g + 1) * LANES]
        ug = a[HALO:, :]
        step = 1
        while step < w:
            a = a + pltpu.roll(a, step, axis=0)
            step *= 2
        inv_cnt = 1.0 / jnp.minimum(avail, float(w))
        mixed.append((a[HALO:, :] * inv_cnt - ug).astype(BF16))
    ymix = jnp.dot(jnp.concatenate(mixed, axis=1), wg_ref[...], preferred_element_type=F32) * ps_ref[...]
    zp = proj(C_Z, C_Q)
    pooled = (ymix * (zp * jax.nn.sigmoid(zp))).astype(BF16)
    y_pool = jnp.dot(pooled, wpu_ref[...], preferred_element_type=F32)
    pc_ref[0] = (jax.nn.sigmoid(proj(C_GP, C_GM)) * y_pool).astype(pc_ref.dtype)

    zm = proj(C_ZM, C_GP)
    sz_ref[0] = (zm * jax.nn.sigmoid(zm)).astype(sz_ref.dtype)
    sg_ref[0] = jax.nn.sigmoid(proj(C_GM, C_KR)).astype(sg_ref.dtype)


def _attn_kernel(bounded_ref, qt_ref, k_ref, vt_ref, h_ref, sz_ref, sg_ref, pc_ref, wmu_ref, wout_ref, out_ref,
                 m_ref, l_ref, acc_ref, o_ref, *, layer):
    qi = pl.program_id(1)
    tq = qt_ref.shape[3]
    ck = vt_ref.shape[4]
    heads = range(qt_ref.shape[1])

    def span_scores(c0, n):
        start = pl.multiple_of(c0 * ck, ck)
        return [jnp.dot(k_ref[0, hh, pl.ds(start, n * ck), :], qt_ref[0, hh], preferred_element_type=F32)
                for hh in heads]

    def causal(c0, shape):
        kpos = c0 * ck + lax.broadcasted_iota(jnp.int32, shape, 0)
        qpos = qi * tq + lax.broadcasted_iota(jnp.int32, shape, 1)
        return kpos <= qpos

    def weighted_values(c0, n, hh, p):
        pb = p.astype(BF16)
        lo = (hh % 2) * V_HEAD_DIM
        out = None
        for c in range(n):
            vt = vt_ref[0, hh // 2, c0 + c][lo:lo + V_HEAD_DIM, :]
            d = jnp.dot(vt, pb[c * ck:(c + 1) * ck, :], preferred_element_type=F32)
            out = d if out is None else out + d
        return out

    def step_bounded(c0, n, masked):
        scores = span_scores(c0, n)
        for hh in heads:
            p = jnp.exp2(scores[hh])
            if masked:
                p = jnp.where(causal(c0, p.shape), p, 0.0)
            l_ref[hh] += jnp.sum(p.reshape(n * ck // 8, 8, tq), axis=0)
            acc_ref[hh] += weighted_values(c0, n, hh, p)

    def step_general(c0, n, masked):
        scores = span_scores(c0, n)
        for hh in heads:
            s = scores[hh]
            if masked:
                s = jnp.where(causal(c0, s.shape), s, MASK_VALUE)
            m_prev = m_ref[hh]
            m_new = jnp.maximum(m_prev, jnp.max(s, axis=0, keepdims=True))
            alpha = jnp.exp2(m_prev - m_new)
            p = jnp.exp2(s - m_new)
            l_ref[hh, 0:1, :] = alpha * l_ref[hh, 0:1, :] + jnp.sum(p, axis=0, keepdims=True)
            acc_ref[hh] = alpha * acc_ref[hh] + weighted_values(c0, n, hh, p)
            m_ref[hh] = m_new

    def run(step, groups):
        m_ref[...] = jnp.full_like(m_ref, -jnp.inf)
        l_ref[...] = jnp.zeros_like(l_ref)
        acc_ref[...] = jnp.zeros_like(acc_ref)

        first, left = 0, qi
        for group in groups:
            def body(g, carry, first=first, group=group):
                step(first + g * group, group, False)
                return carry

            trips = left // group
            lax.fori_loop(0, trips, body, 0)
            first, left = first + trips * group, left - trips * group
        for r in range(groups[-1]):
            @pl.when(left == r)
            def _(r=r):
                step(qi - r, r + 1, True)
                out_t = jnp.concatenate([acc_ref[hh] / jnp.sum(l_ref[hh], axis=0, keepdims=True) for hh in heads],
                                        axis=0)
                o_ref[...] = out_t.T

    bounded = bounded_ref[layer] != 0

    @pl.when(bounded)
    def _():
        run(step_bounded, KV_GROUPS)

    @pl.when(jnp.logical_not(bounded))
    def _():
        run(step_general, KV_GROUPS[-1:])

    om = (o_ref[...] * sz_ref[0].astype(F32)).astype(BF16)
    y_mla = jnp.dot(om, wmu_ref[...], preferred_element_type=F32)
    merged = (pc_ref[0].astype(F32) + sg_ref[0].astype(F32) * y_mla).astype(BF16)
    out_ref[0] = h_ref[0] + jnp.dot(merged, wout_ref[...], preferred_element_type=F32)


def _const_spec(shape):
    return pl.BlockSpec(shape, lambda *_: (0,) * len(shape))


def _layer_spec(stacked, layer):
    rest = stacked.shape[1:]
    return pl.BlockSpec((None,) + rest, lambda *_: (layer,) + (0,) * len(rest))


def _row_spec(t, width):
    return pl.BlockSpec((1, t, width), lambda b, i: (b, i, 0))


def _rope_tables(pos_pad, freq_lane):
    b, lp = pos_pad.shape
    t = ROW_TILE
    out = jax.ShapeDtypeStruct((b, lp, LANES), F32)
    return pl.pallas_call(
        _rope_table_kernel,
        grid=(b, lp // t),
        in_specs=[_row_spec(t, 1), _const_spec((1, LANES))],
        out_specs=[_row_spec(t, LANES)] * 2,
        out_shape=[out] * 2,
        name="rope_tables",
    )(pos_pad[:, :, None], freq_lane)


def _pre(h, tabs, wts, layer):
    b, lp, _ = h.shape
    t = ROW_TILE
    nch = t // ATT_TILE
    k_shape = jax.ShapeDtypeStruct((b, N_HEADS, lp, HEAD_PAD), BF16)
    k_spec = pl.BlockSpec((1, N_HEADS, t, HEAD_PAD), lambda bb, i: (bb, 0, i, 0))
    qt_shape = jax.ShapeDtypeStruct((b, N_HEADS, HEAD_PAD, lp), BF16)
    qt_spec = pl.BlockSpec((1, N_HEADS, HEAD_PAD, t), lambda bb, i: (bb, 0, 0, i))
    vt_shape = jax.ShapeDtypeStruct((b, N_HEADS // 2, lp // ATT_TILE, 2 * V_HEAD_DIM, ATT_TILE), BF16)
    vt_spec = pl.BlockSpec((1, N_HEADS // 2, nch, 2 * V_HEAD_DIM, ATT_TILE), lambda bb, i: (bb, 0, i, 0, 0))
    weights = [wts[name] for name in ("ng", "win", "wg", "ps", "wpu", "gqa", "gkva", "wqb", "wkvb", "gqn", "gkn")]
    return pl.pallas_call(
        _pre_kernel,
        grid=(b, lp // t),
        in_specs=([_row_spec(t, D_MODEL)] + [_row_spec(t, LANES)] * 2 + [_layer_spec(w, layer) for w in weights]
                  + [_const_spec(wts["ones"].shape)]),
        out_specs=[qt_spec, k_spec, vt_spec, _row_spec(t, MLA_WIDTH), _row_spec(t, D_MODEL),
                   _row_spec(t, D_MODEL)],
        out_shape=[qt_shape, k_shape, vt_shape,
                   jax.ShapeDtypeStruct((b, lp, MLA_WIDTH), BF16),
                   jax.ShapeDtypeStruct((b, lp, D_MODEL), BF16),
                   jax.ShapeDtypeStruct((b, lp, D_MODEL), BF16)],
        scratch_shapes=[pltpu.VMEM((HALO, POOL_WIDTH), F32)],
        compiler_params=pltpu.CompilerParams(dimension_semantics=("parallel", "arbitrary"),
                                             vmem_limit_bytes=VMEM_LIMIT),
        name="pre",
    )(h, *tabs, *weights, wts["ones"])


def _scores_bounded(q_gain, k_gain):
    limit = (QK_HEAD_DIM * jnp.max(jnp.abs(q_gain), axis=-1) * jnp.max(jnp.abs(k_gain), axis=-1)
             * (math.log2(math.e) / math.sqrt(QK_HEAD_DIM)) * 1.02)
    return (limit <= SCORE_BOUND).astype(jnp.int32)


def _attention(bounded, layer, qt, k, vt, h, sz, sg, pc, wts):
    b, nh, lp, _ = k.shape
    t = ATT_TILE
    single = pl.Buffered(1)
    rows = lambda width: pl.BlockSpec((1, t, width), lambda bb, i, f: (bb, i, 0))
    weight = lambda w: pl.BlockSpec((None,) + w.shape[1:], lambda bb, i, f: (layer, 0, 0), pipeline_mode=single)
    grid_spec = pltpu.PrefetchScalarGridSpec(
        num_scalar_prefetch=1,
        grid=(b, lp // t),
        in_specs=[pl.BlockSpec((1, nh, HEAD_PAD, t), lambda bb, i, f: (bb, 0, 0, i)),
                  pl.BlockSpec((1, nh, lp, HEAD_PAD), lambda bb, i, f: (bb, 0, 0, 0), pipeline_mode=single),
                  pl.BlockSpec((1, nh // 2, lp // t, 2 * V_HEAD_DIM, t), lambda bb, i, f: (bb, 0, 0, 0, 0),
                               pipeline_mode=single),
                  rows(D_MODEL), rows(MLA_WIDTH), rows(D_MODEL), rows(D_MODEL),
                  weight(wts["wmu"]), weight(wts["wout"])],
        out_specs=rows(D_MODEL),
        scratch_shapes=[pltpu.VMEM((nh, 1, t), F32),
                        pltpu.VMEM((nh, 8, t), F32),
                        pltpu.VMEM((nh, V_HEAD_DIM, t), F32),
                        pltpu.VMEM((t, nh * V_HEAD_DIM), F32)],
    )
    return pl.pallas_call(
        functools.partial(_attn_kernel, layer=layer),
        grid_spec=grid_spec,
        out_shape=jax.ShapeDtypeStruct(h.shape, F32),
        input_output_aliases={4: 0},
        compiler_params=pltpu.CompilerParams(dimension_semantics=("parallel", "arbitrary"),
                                             vmem_limit_bytes=VMEM_LIMIT),
        name="attn",
    )(bounded, qt, k, vt, h, sz, sg, pc, wts["wmu"], wts["wout"])


def _to_head_lanes(a):
    n = a.shape[-1]
    half = QK_ROPE_DIM // 2
    split = LANES // 2 - half
    zeros = lambda w: jnp.zeros(a.shape[:-1] + (w,), a.dtype)
    rot1 = a[..., QK_NOPE_DIM:QK_NOPE_DIM + half] if n > QK_NOPE_DIM else zeros(half)
    rot2 = a[..., QK_NOPE_DIM + half:] if n > QK_NOPE_DIM else zeros(half)
    assert [HEAD_LANES[d] for d in (0, split, QK_NOPE_DIM, QK_NOPE_DIM + half)] == [half, LANES // 2 + half, 0,
                                                                                     LANES // 2]
    return jnp.concatenate([rot1, a[..., :split], rot2, a[..., split:QK_NOPE_DIM], zeros(HEAD_PAD - QK_HEAD_DIM)],
                           axis=-1)


def _prepare_weights(norm_gain, w_in, pool_w_group, pool_scale, pool_w_up, q_a_norm_gain, kv_a_norm_gain, w_q_b,
                     w_kv_b, q_norm_gain, k_norm_gain, mla_w_up, w_out):
    depth = w_in.shape[0]
    u, z, cq, ckv, kr, zm, gp, gm = jnp.split(w_in.astype(BF16), (512, 1024, 1792, 2048, 2080, 2592, 3616), axis=2)
    kr_block = _to_head_lanes(jnp.concatenate([jnp.zeros(kr.shape[:-1] + (QK_NOPE_DIM,), kr.dtype), kr], axis=-1))
    win = jnp.concatenate([u, z, cq, ckv, zm, gp, gm, kr_block], axis=2)
    wqb = _to_head_lanes(w_q_b.reshape(depth, Q_LORA_RANK, N_HEADS, QK_HEAD_DIM)).reshape(depth, Q_LORA_RANK, -1)
    wkv = w_kv_b.reshape(depth, KV_LORA_RANK, N_HEADS, QK_NOPE_DIM + V_HEAD_DIM)
    wkvb = jnp.concatenate([_to_head_lanes(wkv[..., :QK_NOPE_DIM]).reshape(depth, KV_LORA_RANK, -1),
                            wkv[..., QK_NOPE_DIM:].reshape(depth, KV_LORA_RANK, -1)], axis=2)
    eye = jnp.eye(pool_w_group.shape[1], dtype=pool_w_group.dtype)
    wg = (pool_w_group[:, :, :, None, :] * eye[None, :, None, :, None]).reshape(depth, POOL_WIDTH, POOL_WIDTH)
    q_scale = math.log2(math.e) / math.sqrt(QK_HEAD_DIM)
    block = jnp.arange(2 * HEAD_PAD) // HEAD_PAD
    row = lambda g: g[:, None, :]
    return {
        "ones": (block[:, None] == block[None, :]).astype(BF16),
        "ng": row(norm_gain),
        "win": win,
        "wg": wg.astype(BF16),
        "ps": row(pool_scale),
        "wpu": pool_w_up.astype(BF16),
        "gqa": row(q_a_norm_gain),
        "gkva": row(kv_a_norm_gain),
        "wqb": wqb.astype(BF16),
        "wkvb": wkvb.astype(BF16),
        "gqn": row(jnp.tile(_to_head_lanes(q_norm_gain * q_scale), (1, 2))),
        "gkn": row(jnp.tile(_to_head_lanes(k_norm_gain), (1, 2))),
        "wmu": mla_w_up.astype(BF16),
        "wout": w_out.astype(BF16),
    }


def kernel(x, positions, meta_tokens, norm_gain, w_in, pool_w_group, pool_scale, pool_w_up, q_a_norm_gain,
           kv_a_norm_gain, w_q_b, w_kv_b, q_norm_gain, k_norm_gain, mla_w_up, w_out):
    b, seq, _ = x.shape
    lp = PAD_FRONT + N_META + seq + PAD_BACK
    assert lp % ROW_TILE == 0 and ROW_TILE % ATT_TILE == 0

    meta = jnp.broadcast_to(meta_tokens[None].astype(x.dtype), (b, N_META, D_MODEL))
    h = jnp.concatenate([jnp.zeros((b, PAD_FRONT, D_MODEL), x.dtype), meta, x,
                         jnp.zeros((b, PAD_BACK, D_MODEL), x.dtype)], axis=1)
    meta_pos = jnp.broadcast_to(jnp.arange(N_META, dtype=jnp.int32)[None], (b, N_META))
    pos = jnp.concatenate([jnp.zeros((b, PAD_FRONT), jnp.int32), meta_pos, positions + N_META,
                           jnp.zeros((b, PAD_BACK), jnp.int32)], axis=1)

    half = QK_ROPE_DIM // 2
    inv_freq = ROPE_THETA ** (-jnp.arange(half, dtype=F32) / half)
    freq_lane = _to_head_lanes(jnp.concatenate([jnp.zeros((QK_NOPE_DIM,), F32), inv_freq, inv_freq]))[None, :]
    tabs = _rope_tables(pos, freq_lane)

    wts = _prepare_weights(norm_gain, w_in, pool_w_group, pool_scale, pool_w_up, q_a_norm_gain, kv_a_norm_gain,
                           w_q_b, w_kv_b, q_norm_gain, k_norm_gain, mla_w_up, w_out)
    bounded = _scores_bounded(q_norm_gain, k_norm_gain)
    for l in range(DEPTH):
        q, k, v, sz, sg, pc = _pre(h, tabs, wts, l)
        h = _attention(bounded, l, q, k, v, h, sz, sg, pc, wts)

    return h[:, PAD_FRONT + N_META:PAD_FRONT + N_META + seq]
```

```python
import functools
import math

import jax
import jax.numpy as jnp
from jax import lax
from jax.experimental import pallas as pl
from jax.experimental.pallas import tpu as pltpu

F32 = jnp.float32
BF16 = jnp.bfloat16

D_MODEL = 1024
DEPTH = 4
N_META = 16
POOL_WIDTH = 512
POOL_WINDOWS = (2, 4, 8, 16)
POOL_GROUP_DIM = 128
N_HEADS = 8
QK_NOPE_DIM = 64
QK_ROPE_DIM = 32
QK_HEAD_DIM = 96
V_HEAD_DIM = 64
MLA_WIDTH = 512
KV_LORA_RANK = 256
Q_LORA_RANK = 768
ROPE_THETA = 10000.0
NORM_EPS = 1e-6
MASK_VALUE = -1e30
PAD_KEY_SCORE = MASK_VALUE

LANES = 128
HEAD_PAD = LANES
PAD_FRONT = 240
PAD_BACK = 0
ROW_TILE = 768
ATT_TILE = 256
KV_GROUPS = (8, 2)
SCORE_BOUND = 32.0
HALO = 16
VMEM_LIMIT = 56 * 1024 * 1024

C_U, C_Z, C_Q, C_KV, C_ZM, C_GP, C_GM, C_KR, C_END = 0, 512, 1024, 1792, 2048, 2560, 3584, 4608, 4736


def _rms(x, gain, n):
    inv = lax.rsqrt(jnp.sum(x * x, axis=-1, keepdims=True) * (1.0 / n) + NORM_EPS)
    return x * inv * gain


def _head_lane(d):
    half = QK_ROPE_DIM // 2
    if d >= QK_NOPE_DIM:
        r = d - QK_NOPE_DIM
        return r if r < half else LANES // 2 + (r - half)
    return half + d if d < LANES // 2 - half else LANES // 2 + half + (d - (LANES // 2 - half))


HEAD_LANES = [_head_lane(d) for d in range(QK_HEAD_DIM)]


def _head_norm_rope(x, ones_ref, gain, c, s):
    sq = x * x
    hi = sq.astype(BF16)
    lo = (sq - hi.astype(F32)).astype(BF16)
    ss = (jnp.dot(hi, ones_ref[...], preferred_element_type=F32)
          + jnp.dot(lo, ones_ref[...], preferred_element_type=F32))
    xn = x * lax.rsqrt(ss * (1.0 / QK_HEAD_DIM) + NORM_EPS) * gain
    out = []
    for j in range(x.shape[1] // LANES):
        blk = xn[:, j * LANES:(j + 1) * LANES]
        out.append(blk * c + pltpu.roll(blk, LANES // 2, axis=1) * s)
    return out


def _rope_table_kernel(pos_ref, freq_ref, c_ref, s_ref):
    ang = pos_ref[0].astype(F32) * freq_ref[...]
    lane = lax.broadcasted_iota(jnp.int32, ang.shape, 1)
    c_ref[0] = jnp.cos(ang)
    s_ref[0] = jnp.where(lane < LANES // 2, -jnp.sin(ang), jnp.sin(ang))


def _pre_kernel(h_ref, c_ref, s_ref, ng_ref, win_ref, wg_ref, ps_ref, wpu_ref, gqa_ref, gkva_ref,
                wqb_ref, wkvb_ref, gqn_ref, gkn_ref, ones_ref,
                q_ref, k_ref, v_ref, sz_ref, sg_ref, pc_ref, halo_ref):
    i = pl.program_id(1)
    t = h_ref.shape[1]
    x = h_ref[0]
    hn = _rms(x, ng_ref[...], D_MODEL).astype(BF16)

    def proj(lo, hi):
        return jnp.dot(hn, win_ref[:, lo:hi], preferred_element_type=F32)

    c, s = c_ref[0], s_ref[0]
    cqn = _rms(proj(C_Q, C_KV), gqa_ref[...], Q_LORA_RANK).astype(BF16)
    qp = jnp.dot(cqn, wqb_ref[...], preferred_element_type=F32)
    ckvn = _rms(proj(C_KV, C_ZM), gkva_ref[...], KV_LORA_RANK).astype(BF16)
    kv = jnp.dot(ckvn, wkvb_ref[...], preferred_element_type=F32)
    krp = proj(C_KR, C_END)
    krp2 = jnp.concatenate([krp, krp], axis=1)
    row = i * t + lax.broadcasted_iota(jnp.int32, (t, 1), 0)
    sentinel = lax.broadcasted_iota(jnp.int32, (1, HEAD_PAD), 1) == QK_HEAD_DIM
    q_mark = jnp.where(sentinel, 1.0, 0.0)
    k_mark = jnp.where(sentinel & (row < PAD_FRONT), PAD_KEY_SCORE, 0.0)
    width = 2 * HEAD_PAD
    for pair in range(N_HEADS // 2):
        lo = pair * width
        qs = _head_norm_rope(qp[:, lo:lo + width], ones_ref, gqn_ref[...], c, s)
        ks = _head_norm_rope(kv[:, lo:lo + width] + krp2, ones_ref, gkn_ref[...], c, s)
        for odd in range(2):
            q_ref[0, 2 * pair + odd] = (qs[odd] + q_mark).T.astype(BF16)
            k_ref[0, 2 * pair + odd] = (ks[odd] + k_mark).astype(BF16)
        v0 = N_HEADS * HEAD_PAD + pair * LANES
        vbt = kv[:, v0:v0 + LANES].T.astype(BF16)
        for ch in range(t // ATT_TILE):
            v_ref[0, pair, ch] = vbt[:, ch * ATT_TILE:(ch + 1) * ATT_TILE]

    @pl.when(i == 0)
    def _():
        halo_ref[...] = jnp.zeros_like(halo_ref)

    u = proj(C_U, C_Z)
    uext = jnp.concatenate([halo_ref[...], u], axis=0)
    halo_ref[...] = u[t - HALO:, :]
    avail = jnp.maximum(row - (PAD_FRONT - 1), 1).astype(F32)
    mixed = []
    for g, w in enumerate(POOL_WINDOWS):
        a = uext[:, g * LANES:(g + 1) * LANES]
        ug = a[HALO:, :]
        step = 1
        while step < w:
            a = a + pltpu.roll(a, step, axis=0)
            step *= 2
        inv_cnt = 1.0 / jnp.minimum(avail, float(w))
        mixed.append((a[HALO:, :] * inv_cnt - ug).astype(BF16))
    ymix = jnp.dot(jnp.concatenate(mixed, axis=1), wg_ref[...], preferred_element_type=F32) * ps_ref[...]
    zp = proj(C_Z, C_Q)
    pooled = (ymix * (zp * jax.nn.sigmoid(zp))).astype(BF16)
    y_pool = jnp.dot(pooled, wpu_ref[...], preferred_element_type=F32)
    pc_ref[0] = (jax.nn.sigmoid(proj(C_GP, C_GM)) * y_pool).astype(pc_ref.dtype)

    zm = proj(C_ZM, C_GP)
    sz_ref[0] = (zm * jax.nn.sigmoid(zm)).astype(sz_ref.dtype)
    sg_ref[0] = jax.nn.sigmoid(proj(C_GM, C_KR)).astype(sg_ref.dtype)


def _attn_kernel(bounded_ref, qt_ref, k_ref, vt_ref, h_ref, sz_ref, sg_ref, pc_ref, wmu_ref, wout_ref, out_ref,
                 m_ref, l_ref, acc_ref, o_ref, *, layer):
    qi = pl.program_id(1)
    tq = qt_ref.shape[3]
    ck = vt_ref.shape[4]
    heads = range(qt_ref.shape[1])

    def span_scores(c0, n):
        start = pl.multiple_of(c0 * ck, ck)
        return [jnp.dot(k_ref[0, hh, pl.ds(start, n * ck), :], qt_ref[0, hh], preferred_element_type=F32)
                for hh in heads]

    def visible(c0, shape):
        kpos = c0 * ck + lax.broadcasted_iota(jnp.int32, shape, 0)
        qpos = qi * tq + lax.broadcasted_iota(jnp.int32, shape, 1)
        return (kpos <= qpos) | (qpos < PAD_FRONT)

    def weighted_values(c0, n, hh, p):
        pb = p.astype(BF16)
        lo = (hh % 2) * V_HEAD_DIM
        out = None
        for c in range(n):
            vt = vt_ref[0, hh // 2, c0 + c][lo:lo + V_HEAD_DIM, :]
            d = jnp.dot(vt, pb[c * ck:(c + 1) * ck, :], preferred_element_type=F32)
            out = d if out is None else out + d
        return out

    def step_bounded(c0, n, masked):
        scores = span_scores(c0, n)
        for hh in heads:
            p = jnp.exp2(scores[hh])
            if masked:
                p = jnp.where(visible(c0, p.shape), p, 0.0)
            l_ref[hh] += jnp.sum(p.reshape(n * ck // 8, 8, tq), axis=0)
            acc_ref[hh] += weighted_values(c0, n, hh, p)

    def step_general(c0, n, masked):
        scores = span_scores(c0, n)
        for hh in heads:
            s = scores[hh]
            if masked:
                s = jnp.where(visible(c0, s.shape), s, MASK_VALUE)
            m_prev = m_ref[hh]
            m_new = jnp.maximum(m_prev, jnp.max(s, axis=0, keepdims=True))
            alpha = jnp.exp2(m_prev - m_new)
            p = jnp.exp2(s - m_new)
            l_ref[hh, 0:1, :] = alpha * l_ref[hh, 0:1, :] + jnp.sum(p, axis=0, keepdims=True)
            acc_ref[hh] = alpha * acc_ref[hh] + weighted_values(c0, n, hh, p)
            m_ref[hh] = m_new

    def run(step, groups):
        m_ref[...] = jnp.full_like(m_ref, -jnp.inf)
        l_ref[...] = jnp.zeros_like(l_ref)
        acc_ref[...] = jnp.zeros_like(acc_ref)

        first, left = 0, qi
        for group in groups:
            def body(g, carry, first=first, group=group):
                step(first + g * group, group, False)
                return carry

            trips = left // group
            lax.fori_loop(0, trips, body, 0)
            first, left = first + trips * group, left - trips * group
        for r in range(groups[-1]):
            @pl.when(left == r)
            def _(r=r):
                step(qi - r, r + 1, True)
                out_t = jnp.concatenate([acc_ref[hh] / jnp.sum(l_ref[hh], axis=0, keepdims=True) for hh in heads],
                                        axis=0)
                o_ref[...] = out_t.T

    bounded = bounded_ref[layer] != 0

    @pl.when(bounded)
    def _():
        run(step_bounded, KV_GROUPS)

    @pl.when(jnp.logical_not(bounded))
    def _():
        run(step_general, KV_GROUPS[-1:])

    om = (o_ref[...] * sz_ref[0].astype(F32)).astype(BF16)
    y_mla = jnp.dot(om, wmu_ref[...], preferred_element_type=F32)
    merged = (pc_ref[0].astype(F32) + sg_ref[0].astype(F32) * y_mla).astype(BF16)
    out_ref[0] = h_ref[0] + jnp.dot(merged, wout_ref[...], preferred_element_type=F32)


def _const_spec(shape):
    return pl.BlockSpec(shape, lambda *_: (0,) * len(shape))


def _layer_spec(stacked, layer):
    rest = stacked.shape[1:]
    return pl.BlockSpec((None,) + rest, lambda *_: (layer,) + (0,) * len(rest))


def _row_spec(t, width):
    return pl.BlockSpec((1, t, width), lambda b, i: (b, i, 0))


def _rope_tables(pos_pad, freq_lane):
    b, lp = pos_pad.shape
    t = ROW_TILE
    out = jax.ShapeDtypeStruct((b, lp, LANES), F32)
    return pl.pallas_call(
        _rope_table_kernel,
        grid=(b, lp // t),
        in_specs=[_row_spec(t, 1), _const_spec((1, LANES))],
        out_specs=[_row_spec(t, LANES)] * 2,
        out_shape=[out] * 2,
        name="rope_tables",
    )(pos_pad[:, :, None], freq_lane)


def _pre(h, tabs, wts, layer):
    b, lp, _ = h.shape
    t = ROW_TILE
    nch = t // ATT_TILE
    k_shape = jax.ShapeDtypeStruct((b, N_HEADS, lp, HEAD_PAD), BF16)
    k_spec = pl.BlockSpec((1, N_HEADS, t, HEAD_PAD), lambda bb, i: (bb, 0, i, 0))
    qt_shape = jax.ShapeDtypeStruct((b, N_HEADS, HEAD_PAD, lp), BF16)
    qt_spec = pl.BlockSpec((1, N_HEADS, HEAD_PAD, t), lambda bb, i: (bb, 0, 0, i))
    vt_shape = jax.ShapeDtypeStruct((b, N_HEADS // 2, lp // ATT_TILE, 2 * V_HEAD_DIM, ATT_TILE), BF16)
    vt_spec = pl.BlockSpec((1, N_HEADS // 2, nch, 2 * V_HEAD_DIM, ATT_TILE), lambda bb, i: (bb, 0, i, 0, 0))
    weights = [wts[name] for name in ("ng", "win", "wg", "ps", "wpu", "gqa", "gkva", "wqb", "wkvb", "gqn", "gkn")]
    return pl.pallas_call(
        _pre_kernel,
        grid=(b, lp // t),
        in_specs=([_row_spec(t, D_MODEL)] + [_row_spec(t, LANES)] * 2 + [_layer_spec(w, layer) for w in weights]
                  + [_const_spec(wts["ones"].shape)]),
        out_specs=[qt_spec, k_spec, vt_spec, _row_spec(t, MLA_WIDTH), _row_spec(t, D_MODEL),
                   _row_spec(t, D_MODEL)],
        out_shape=[qt_shape, k_shape, vt_shape,
                   jax.ShapeDtypeStruct((b, lp, MLA_WIDTH), BF16),
                   jax.ShapeDtypeStruct((b, lp, D_MODEL), BF16),
                   jax.ShapeDtypeStruct((b, lp, D_MODEL), BF16)],
        scratch_shapes=[pltpu.VMEM((HALO, POOL_WIDTH), F32)],
        compiler_params=pltpu.CompilerParams(dimension_semantics=("parallel", "arbitrary"),
                                             vmem_limit_bytes=VMEM_LIMIT),
        name="pre",
    )(h, *tabs, *weights, wts["ones"])


def _scores_bounded(q_gain, k_gain):
    limit = (QK_HEAD_DIM * jnp.max(jnp.abs(q_gain), axis=-1) * jnp.max(jnp.abs(k_gain), axis=-1)
             * (math.log2(math.e) / math.sqrt(QK_HEAD_DIM)) * 1.02)
    return (limit <= SCORE_BOUND).astype(jnp.int32)


def _attention(bounded, layer, qt, k, vt, h, sz, sg, pc, wts, final):
    b, nh, lp, _ = k.shape
    t = ATT_TILE
    assert PAD_FRONT + N_META == t
    if final:
        out_spec = pl.BlockSpec((1, t, D_MODEL), lambda bb, i, f: (bb, jnp.maximum(i - 1, 0), 0))
        out_shape = jax.ShapeDtypeStruct((b, lp - t - PAD_BACK, D_MODEL), F32)
        aliases = {}
    else:
        out_spec = pl.BlockSpec((1, t, D_MODEL), lambda bb, i, f: (bb, i, 0))
        out_shape = jax.ShapeDtypeStruct(h.shape, F32)
        aliases = {4: 0}
    single = pl.Buffered(1)
    rows = lambda width: pl.BlockSpec((1, t, width), lambda bb, i, f: (bb, i, 0))
    weight = lambda w: pl.BlockSpec((None,) + w.shape[1:], lambda bb, i, f: (layer, 0, 0), pipeline_mode=single)
    grid_spec = pltpu.PrefetchScalarGridSpec(
        num_scalar_prefetch=1,
        grid=(b, lp // t),
        in_specs=[pl.BlockSpec((1, nh, HEAD_PAD, t), lambda bb, i, f: (bb, 0, 0, i)),
                  pl.BlockSpec((1, nh, lp, HEAD_PAD), lambda bb, i, f: (bb, 0, 0, 0), pipeline_mode=single),
                  pl.BlockSpec((1, nh // 2, lp // t, 2 * V_HEAD_DIM, t), lambda bb, i, f: (bb, 0, 0, 0, 0),
                               pipeline_mode=single),
                  rows(D_MODEL), rows(MLA_WIDTH), rows(D_MODEL), rows(D_MODEL),
                  weight(wts["wmu"]), weight(wts["wout"])],
        out_specs=out_spec,
        scratch_shapes=[pltpu.VMEM((nh, 1, t), F32),
                        pltpu.VMEM((nh, 8, t), F32),
                        pltpu.VMEM((nh, V_HEAD_DIM, t), F32),
                        pltpu.VMEM((t, nh * V_HEAD_DIM), F32)],
    )
    return pl.pallas_call(
        functools.partial(_attn_kernel, layer=layer),
        grid_spec=grid_spec,
        out_shape=out_shape,
        input_output_aliases=aliases,
        compiler_params=pltpu.CompilerParams(dimension_semantics=("parallel", "arbitrary"),
                                             vmem_limit_bytes=VMEM_LIMIT),
        name="attn",
    )(bounded, qt, k, vt, h, sz, sg, pc, wts["wmu"], wts["wout"])


def _to_head_lanes(a):
    n = a.shape[-1]
    half = QK_ROPE_DIM // 2
    split = LANES // 2 - half
    zeros = lambda w: jnp.zeros(a.shape[:-1] + (w,), a.dtype)
    rot1 = a[..., QK_NOPE_DIM:QK_NOPE_DIM + half] if n > QK_NOPE_DIM else zeros(half)
    rot2 = a[..., QK_NOPE_DIM + half:] if n > QK_NOPE_DIM else zeros(half)
    assert [HEAD_LANES[d] for d in (0, split, QK_NOPE_DIM, QK_NOPE_DIM + half)] == [half, LANES // 2 + half, 0,
                                                                                     LANES // 2]
    return jnp.concatenate([rot1, a[..., :split], rot2, a[..., split:QK_NOPE_DIM], zeros(HEAD_PAD - QK_HEAD_DIM)],
                           axis=-1)


def _prepare_weights(norm_gain, w_in, pool_w_group, pool_scale, pool_w_up, q_a_norm_gain, kv_a_norm_gain, w_q_b,
                     w_kv_b, q_norm_gain, k_norm_gain, mla_w_up, w_out):
    depth = w_in.shape[0]
    u, z, cq, ckv, kr, zm, gp, gm = jnp.split(w_in.astype(BF16), (512, 1024, 1792, 2048, 2080, 2592, 3616), axis=2)
    kr_block = _to_head_lanes(jnp.concatenate([jnp.zeros(kr.shape[:-1] + (QK_NOPE_DIM,), kr.dtype), kr], axis=-1))
    win = jnp.concatenate([u, z, cq, ckv, zm, gp, gm, kr_block], axis=2)
    wqb = _to_head_lanes(w_q_b.reshape(depth, Q_LORA_RANK, N_HEADS, QK_HEAD_DIM)).reshape(depth, Q_LORA_RANK, -1)
    wkv = w_kv_b.reshape(depth, KV_LORA_RANK, N_HEADS, QK_NOPE_DIM + V_HEAD_DIM)
    wkvb = jnp.concatenate([_to_head_lanes(wkv[..., :QK_NOPE_DIM]).reshape(depth, KV_LORA_RANK, -1),
                            wkv[..., QK_NOPE_DIM:].reshape(depth, KV_LORA_RANK, -1)], axis=2)
    eye = jnp.eye(pool_w_group.shape[1], dtype=pool_w_group.dtype)
    wg = (pool_w_group[:, :, :, None, :] * eye[None, :, None, :, None]).reshape(depth, POOL_WIDTH, POOL_WIDTH)
    q_scale = math.log2(math.e) / math.sqrt(QK_HEAD_DIM)
    block = jnp.arange(2 * HEAD_PAD) // HEAD_PAD
    row = lambda g: g[:, None, :]
    return {
        "ones": (block[:, None] == block[None, :]).astype(BF16),
        "ng": row(norm_gain),
        "win": win,
        "wg": wg.astype(BF16),
        "ps": row(pool_scale),
        "wpu": pool_w_up.astype(BF16),
        "gqa": row(q_a_norm_gain),
        "gkva": row(kv_a_norm_gain),
        "wqb": wqb.astype(BF16),
        "wkvb": wkvb.astype(BF16),
        "gqn": row(jnp.tile(_to_head_lanes(q_norm_gain * q_scale), (1, 2))),
        "gkn": row(jnp.tile(_to_head_lanes(k_norm_gain), (1, 2))),
        "wmu": mla_w_up.astype(BF16),
        "wout": w_out.astype(BF16),
    }


def kernel(x, positions, meta_tokens, norm_gain, w_in, pool_w_group, pool_scale, pool_w_up, q_a_norm_gain,
           kv_a_norm_gain, w_q_b, w_kv_b, q_norm_gain, k_norm_gain, mla_w_up, w_out):
    b, seq, _ = x.shape
    lp = PAD_FRONT + N_META + seq + PAD_BACK
    assert lp % ROW_TILE == 0 and ROW_TILE % ATT_TILE == 0

    meta = jnp.broadcast_to(meta_tokens[None].astype(x.dtype), (b, N_META, D_MODEL))
    h = jnp.concatenate([jnp.zeros((b, PAD_FRONT, D_MODEL), x.dtype), meta, x,
                         jnp.zeros((b, PAD_BACK, D_MODEL), x.dtype)], axis=1)
    meta_pos = jnp.broadcast_to(jnp.arange(N_META, dtype=jnp.int32)[None], (b, N_META))
    pos = jnp.concatenate([jnp.zeros((b, PAD_FRONT), jnp.int32), meta_pos, positions + N_META,
                           jnp.zeros((b, PAD_BACK), jnp.int32)], axis=1)

    half = QK_ROPE_DIM // 2
    inv_freq = ROPE_THETA ** (-jnp.arange(half, dtype=F32) / half)
    freq_lane = _to_head_lanes(jnp.concatenate([jnp.zeros((QK_NOPE_DIM,), F32), inv_freq, inv_freq]))[None, :]
    tabs = _rope_tables(pos, freq_lane)

    wts = _prepare_weights(norm_gain, w_in, pool_w_group, pool_scale, pool_w_up, q_a_norm_gain, kv_a_norm_gain,
                           w_q_b, w_kv_b, q_norm_gain, k_norm_gain, mla_w_up, w_out)
    bounded = _scores_bounded(q_norm_gain, k_norm_gain)
    for l in range(DEPTH):
        q, k, v, sz, sg, pc = _pre(h, tabs, wts, l)
        h = _attention(bounded, l, q, k, v, h, sz, sg, pc, wts, final=(l == DEPTH - 1))

    return h
```

```python
import functools
import math

import jax
import jax.numpy as jnp
from jax import lax
from jax.experimental import pallas as pl
from jax.experimental.pallas import tpu as pltpu

F32 = jnp.float32
BF16 = jnp.bfloat16

D_MODEL = 1024
DEPTH = 4
N_META = 16
POOL_WIDTH = 512
POOL_WINDOWS = (2, 4, 8, 16)
POOL_GROUP_DIM = 128
N_HEADS = 8
QK_NOPE_DIM = 64
QK_ROPE_DIM = 32
QK_HEAD_DIM = 96
V_HEAD_DIM = 64
MLA_WIDTH = 512
KV_LORA_RANK = 256
Q_LORA_RANK = 768
ROPE_THETA = 10000.0
NORM_EPS = 1e-6
MASK_VALUE = -1e30
PAD_KEY_SCORE = MASK_VALUE

LANES = 128
HEAD_PAD = LANES
PAD_FRONT = 240
PAD_BACK = 0
ROW_TILE = 768
ATT_TILE = 256
KV_GROUPS = (8, 2)
SCORE_BOUND = 32.0
HALO = 16
VMEM_LIMIT = 56 * 1024 * 1024

C_U, C_Z, C_Q, C_KV, C_ZM, C_GP, C_GM, C_KR, C_END = 0, 512, 1024, 1792, 2048, 2560, 3584, 4608, 4736


def _rms(x, gain, n):
    inv = lax.rsqrt(jnp.sum(x * x, axis=-1, keepdims=True) * (1.0 / n) + NORM_EPS)
    return x * inv * gain


def _head_lane(d):
    half = QK_ROPE_DIM // 2
    if d >= QK_NOPE_DIM:
        r = d - QK_NOPE_DIM
        return r if r < half else LANES // 2 + (r - half)
    return half + d if d < LANES // 2 - half else LANES // 2 + half + (d - (LANES // 2 - half))


HEAD_LANES = [_head_lane(d) for d in range(QK_HEAD_DIM)]


def _head_norm_rope(x, ones_ref, gain, c, s):
    sq = x * x
    hi = sq.astype(BF16)
    lo = (sq - hi.astype(F32)).astype(BF16)
    ss = (jnp.dot(hi, ones_ref[...], preferred_element_type=F32)
          + jnp.dot(lo, ones_ref[...], preferred_element_type=F32))
    xn = x * lax.rsqrt(ss * (1.0 / QK_HEAD_DIM) + NORM_EPS) * gain
    out = []
    for j in range(x.shape[1] // LANES):
        blk = xn[:, j * LANES:(j + 1) * LANES]
        out.append(blk * c + pltpu.roll(blk, LANES // 2, axis=1) * s)
    return out


def _rope_table_kernel(pos_ref, freq_ref, c_ref, s_ref):
    ang = pos_ref[0].astype(F32) * freq_ref[...]
    lane = lax.broadcasted_iota(jnp.int32, ang.shape, 1)
    c_ref[0] = jnp.cos(ang)
    s_ref[0] = jnp.where(lane < LANES // 2, -jnp.sin(ang), jnp.sin(ang))


def _pre_kernel(*refs, first_layer):
    n_src = (1 + ROW_TILE // ATT_TILE) if first_layer else 1
    src, refs = refs[:n_src], refs[n_src:]
    (c_ref, s_ref, ng_ref, win_ref, wg_ref, ps_ref, wpu_ref, gqa_ref, gkva_ref, wqb_ref, wkvb_ref, gqn_ref, gkn_ref,
     ones_ref, q_ref, k_ref, v_ref, sz_ref, sg_ref, pc_ref, *rest) = refs
    halo_ref = rest[-1]
    i = pl.program_id(1)
    if first_layer:
        meta_ref, blocks = src[0], src[1:]
        lead = jnp.concatenate([jnp.zeros((PAD_FRONT, D_MODEL), F32), meta_ref[...]], axis=0)
        x = jnp.concatenate([jnp.where(i == 0, lead, blocks[0][0])] + [blk[0] for blk in blocks[1:]], axis=0)
        rest[0][0] = x
    else:
        x = src[0][0]
    t = x.shape[0]
    hn = _rms(x, ng_ref[...], D_MODEL).astype(BF16)

    def proj(lo, hi):
        return jnp.dot(hn, win_ref[:, lo:hi], preferred_element_type=F32)

    c, s = c_ref[0], s_ref[0]
    cqn = _rms(proj(C_Q, C_KV), gqa_ref[...], Q_LORA_RANK).astype(BF16)
    qp = jnp.dot(cqn, wqb_ref[...], preferred_element_type=F32)
    ckvn = _rms(proj(C_KV, C_ZM), gkva_ref[...], KV_LORA_RANK).astype(BF16)
    kv = jnp.dot(ckvn, wkvb_ref[...], preferred_element_type=F32)
    krp = proj(C_KR, C_END)
    krp2 = jnp.concatenate([krp, krp], axis=1)
    row = i * t + lax.broadcasted_iota(jnp.int32, (t, 1), 0)
    sentinel = lax.broadcasted_iota(jnp.int32, (1, HEAD_PAD), 1) == QK_HEAD_DIM
    q_mark = jnp.where(sentinel, 1.0, 0.0)
    k_mark = jnp.where(sentinel & (row < PAD_FRONT), PAD_KEY_SCORE, 0.0)
    width = 2 * HEAD_PAD
    for pair in range(N_HEADS // 2):
        lo = pair * width
        qs = _head_norm_rope(qp[:, lo:lo + width], ones_ref, gqn_ref[...], c, s)
        ks = _head_norm_rope(kv[:, lo:lo + width] + krp2, ones_ref, gkn_ref[...], c, s)
        for odd in range(2):
            q_ref[0, 2 * pair + odd] = (qs[odd] + q_mark).T.astype(BF16)
            k_ref[0, 2 * pair + odd] = (ks[odd] + k_mark).astype(BF16)
        v0 = N_HEADS * HEAD_PAD + pair * LANES
        vbt = kv[:, v0:v0 + LANES].T.astype(BF16)
        for ch in range(t // ATT_TILE):
            v_ref[0, pair, ch] = vbt[:, ch * ATT_TILE:(ch + 1) * ATT_TILE]

    @pl.when(i == 0)
    def _():
        halo_ref[...] = jnp.zeros_like(halo_ref)

    u = proj(C_U, C_Z)
    uext = jnp.concatenate([halo_ref[...], u], axis=0)
    halo_ref[...] = u[t - HALO:, :]
    avail = jnp.maximum(row - (PAD_FRONT - 1), 1).astype(F32)
    mixed = []
    for g, w in enumerate(POOL_WINDOWS):
        a = uext[:, g * LANES:(g + 1) * LANES]
        ug = a[HALO:, :]
        step = 1
        while step < w:
            a = a + pltpu.roll(a, step, axis=0)
            step *= 2
        inv_cnt = 1.0 / jnp.minimum(avail, float(w))
        mixed.append((a[HALO:, :] * inv_cnt - ug).astype(BF16))
    ymix = jnp.dot(jnp.concatenate(mixed, axis=1), wg_ref[...], preferred_element_type=F32) * ps_ref[...]
    zp = proj(C_Z, C_Q)
    pooled = (ymix * (zp * jax.nn.sigmoid(zp))).astype(BF16)
    y_pool = jnp.dot(pooled, wpu_ref[...], preferred_element_type=F32)
    pc_ref[0] = (jax.nn.sigmoid(proj(C_GP, C_GM)) * y_pool).astype(pc_ref.dtype)

    zm = proj(C_ZM, C_GP)
    sz_ref[0] = (zm * jax.nn.sigmoid(zm)).astype(sz_ref.dtype)
    sg_ref[0] = jax.nn.sigmoid(proj(C_GM, C_KR)).astype(sg_ref.dtype)


def _attn_kernel(bounded_ref, qt_ref, k_ref, vt_ref, h_ref, sz_ref, sg_ref, pc_ref, wmu_ref, wout_ref, out_ref,
                 m_ref, l_ref, acc_ref, o_ref, *, layer):
    qi = pl.program_id(1)
    tq = qt_ref.shape[3]
    ck = vt_ref.shape[4]
    heads = range(qt_ref.shape[1])

    def span_scores(c0, n):
        start = pl.multiple_of(c0 * ck, ck)
        return [jnp.dot(k_ref[0, hh, pl.ds(start, n * ck), :], qt_ref[0, hh], preferred_element_type=F32)
                for hh in heads]

    def visible(c0, shape):
        kpos = c0 * ck + lax.broadcasted_iota(jnp.int32, shape, 0)
        qpos = qi * tq + lax.broadcasted_iota(jnp.int32, shape, 1)
        return (kpos <= qpos) | (qpos < PAD_FRONT)

    def weighted_values(c0, n, hh, p):
        pb = p.astype(BF16)
        lo = (hh % 2) * V_HEAD_DIM
        out = None
        for c in range(n):
            vt = vt_ref[0, hh // 2, c0 + c][lo:lo + V_HEAD_DIM, :]
            d = jnp.dot(vt, pb[c * ck:(c + 1) * ck, :], preferred_element_type=F32)
            out = d if out is None else out + d
        return out

    def step_bounded(c0, n, masked):
        scores = span_scores(c0, n)
        for hh in heads:
            p = jnp.exp2(scores[hh])
            if masked:
                p = jnp.where(visible(c0, p.shape), p, 0.0)
            l_ref[hh] += jnp.sum(p.reshape(n * ck // 8, 8, tq), axis=0)
            acc_ref[hh] += weighted_values(c0, n, hh, p)

    def step_general(c0, n, masked):
        scores = span_scores(c0, n)
        for hh in heads:
            s = scores[hh]
            if masked:
                s = jnp.where(visible(c0, s.shape), s, MASK_VALUE)
            m_prev = m_ref[hh]
            m_new = jnp.maximum(m_prev, jnp.max(s, axis=0, keepdims=True))
            alpha = jnp.exp2(m_prev - m_new)
            p = jnp.exp2(s - m_new)
            l_ref[hh, 0:1, :] = alpha * l_ref[hh, 0:1, :] + jnp.sum(p, axis=0, keepdims=True)
            acc_ref[hh] = alpha * acc_ref[hh] + weighted_values(c0, n, hh, p)
            m_ref[hh] = m_new

    def run(step, groups):
        m_ref[...] = jnp.full_like(m_ref, -jnp.inf)
        l_ref[...] = jnp.zeros_like(l_ref)
        acc_ref[...] = jnp.zeros_like(acc_ref)

        first, left = 0, qi
        for group in groups:
            def body(g, carry, first=first, group=group):
                step(first + g * group, group, False)
                return carry

            trips = left // group
            lax.fori_loop(0, trips, body, 0)
            first, left = first + trips * group, left - trips * group
        for r in range(groups[-1]):
            @pl.when(left == r)
            def _(r=r):
                step(qi - r, r + 1, True)
                out_t = jnp.concatenate([acc_ref[hh] / jnp.sum(l_ref[hh], axis=0, keepdims=True) for hh in heads],
                                        axis=0)
                o_ref[...] = out_t.T

    bounded = bounded_ref[layer] != 0

    @pl.when(bounded)
    def _():
        run(step_bounded, KV_GROUPS)

    @pl.when(jnp.logical_not(bounded))
    def _():
        run(step_general, KV_GROUPS[-1:])

    om = (o_ref[...] * sz_ref[0].astype(F32)).astype(BF16)
    y_mla = jnp.dot(om, wmu_ref[...], preferred_element_type=F32)
    merged = (pc_ref[0].astype(F32) + sg_ref[0].astype(F32) * y_mla).astype(BF16)
    out_ref[0] = h_ref[0] + jnp.dot(merged, wout_ref[...], preferred_element_type=F32)


def _const_spec(shape):
    return pl.BlockSpec(shape, lambda *_: (0,) * len(shape))


def _layer_spec(stacked, layer):
    rest = stacked.shape[1:]
    return pl.BlockSpec((None,) + rest, lambda *_: (layer,) + (0,) * len(rest))


def _row_spec(t, width):
    return pl.BlockSpec((1, t, width), lambda b, i: (b, i, 0))


def _rope_tables(pos_pad, freq_lane):
    b, lp = pos_pad.shape
    t = ROW_TILE
    out = jax.ShapeDtypeStruct((b, lp, LANES), F32)
    return pl.pallas_call(
        _rope_table_kernel,
        grid=(b, lp // t),
        in_specs=[_row_spec(t, 1), _const_spec((1, LANES))],
        out_specs=[_row_spec(t, LANES)] * 2,
        out_shape=[out] * 2,
        name="rope_tables",
    )(pos_pad[:, :, None], freq_lane)


def _pre(h, tabs, wts, layer, meta=None):
    first_layer = meta is not None
    b = h.shape[0]
    lp = tabs[0].shape[1]
    t = ROW_TILE
    nch = t // ATT_TILE
    if first_layer:
        assert PAD_FRONT + N_META == ATT_TILE and PAD_BACK == 0
        token_block = lambda j: pl.BlockSpec((1, ATT_TILE, D_MODEL),
                                             lambda bb, i: (bb, jnp.maximum(nch * i - 1 + j, 0), 0))
        src = [meta] + [h] * nch
        src_specs = [_const_spec(meta.shape)] + [token_block(j) for j in range(nch)]
        extra_specs, extra_shapes = [_row_spec(t, D_MODEL)], [jax.ShapeDtypeStruct((b, lp, D_MODEL), F32)]
    else:
        src, src_specs, extra_specs, extra_shapes = [h], [_row_spec(t, D_MODEL)], [], []
    k_shape = jax.ShapeDtypeStruct((b, N_HEADS, lp, HEAD_PAD), BF16)
    k_spec = pl.BlockSpec((1, N_HEADS, t, HEAD_PAD), lambda bb, i: (bb, 0, i, 0))
    qt_shape = jax.ShapeDtypeStruct((b, N_HEADS, HEAD_PAD, lp), BF16)
    qt_spec = pl.BlockSpec((1, N_HEADS, HEAD_PAD, t), lambda bb, i: (bb, 0, 0, i))
    vt_shape = jax.ShapeDtypeStruct((b, N_HEADS // 2, lp // ATT_TILE, 2 * V_HEAD_DIM, ATT_TILE), BF16)
    vt_spec = pl.BlockSpec((1, N_HEADS // 2, nch, 2 * V_HEAD_DIM, ATT_TILE), lambda bb, i: (bb, 0, i, 0, 0))
    weights = [wts[name] for name in ("ng", "win", "wg", "ps", "wpu", "gqa", "gkva", "wqb", "wkvb", "gqn", "gkn")]
    return pl.pallas_call(
        functools.partial(_pre_kernel, first_layer=first_layer),
        grid=(b, lp // t),
        in_specs=(src_specs + [_row_spec(t, LANES)] * 2 + [_layer_spec(w, layer) for w in weights]
                  + [_const_spec(wts["ones"].shape)]),
        out_specs=[qt_spec, k_spec, vt_spec, _row_spec(t, MLA_WIDTH), _row_spec(t, D_MODEL),
                   _row_spec(t, D_MODEL)] + extra_specs,
        out_shape=[qt_shape, k_shape, vt_shape,
                   jax.ShapeDtypeStruct((b, lp, MLA_WIDTH), BF16),
                   jax.ShapeDtypeStruct((b, lp, D_MODEL), BF16),
                   jax.ShapeDtypeStruct((b, lp, D_MODEL), BF16)] + extra_shapes,
        scratch_shapes=[pltpu.VMEM((HALO, POOL_WIDTH), F32)],
        compiler_params=pltpu.CompilerParams(dimension_semantics=("parallel", "arbitrary"),
                                             vmem_limit_bytes=VMEM_LIMIT),
        name="pre",
    )(*src, *tabs, *weights, wts["ones"])


def _scores_bounded(q_gain, k_gain):
    limit = (QK_HEAD_DIM * jnp.max(jnp.abs(q_gain), axis=-1) * jnp.max(jnp.abs(k_gain), axis=-1)
             * (math.log2(math.e) / math.sqrt(QK_HEAD_DIM)) * 1.02)
    return (limit <= SCORE_BOUND).astype(jnp.int32)


def _attention(bounded, layer, qt, k, vt, h, sz, sg, pc, wts, final):
    b, nh, lp, _ = k.shape
    t = ATT_TILE
    assert PAD_FRONT + N_META == t
    if final:
        out_spec = pl.BlockSpec((1, t, D_MODEL), lambda bb, i, f: (bb, jnp.maximum(i - 1, 0), 0))
        out_shape = jax.ShapeDtypeStruct((b, lp - t - PAD_BACK, D_MODEL), F32)
        aliases = {}
    else:
        out_spec = pl.BlockSpec((1, t, D_MODEL), lambda bb, i, f: (bb, i, 0))
        out_shape = jax.ShapeDtypeStruct(h.shape, F32)
        aliases = {4: 0}
    single = pl.Buffered(1)
    rows = lambda width: pl.BlockSpec((1, t, width), lambda bb, i, f: (bb, i, 0))
    weight = lambda w: pl.BlockSpec((None,) + w.shape[1:], lambda bb, i, f: (layer, 0, 0), pipeline_mode=single)
    grid_spec = pltpu.PrefetchScalarGridSpec(
        num_scalar_prefetch=1,
        grid=(b, lp // t),
        in_specs=[pl.BlockSpec((1, nh, HEAD_PAD, t), lambda bb, i, f: (bb, 0, 0, i)),
                  pl.BlockSpec((1, nh, lp, HEAD_PAD), lambda bb, i, f: (bb, 0, 0, 0), pipeline_mode=single),
                  pl.BlockSpec((1, nh // 2, lp // t, 2 * V_HEAD_DIM, t), lambda bb, i, f: (bb, 0, 0, 0, 0),
                               pipeline_mode=single),
                  rows(D_MODEL), rows(MLA_WIDTH), rows(D_MODEL), rows(D_MODEL),
                  weight(wts["wmu"]), weight(wts["wout"])],
        out_specs=out_spec,
        scratch_shapes=[pltpu.VMEM((nh, 1, t), F32),
                        pltpu.VMEM((nh, 8, t), F32),
                        pltpu.VMEM((nh, V_HEAD_DIM, t), F32),
                        pltpu.VMEM((t, nh * V_HEAD_DIM), F32)],
    )
    return pl.pallas_call(
        functools.partial(_attn_kernel, layer=layer),
        grid_spec=grid_spec,
        out_shape=out_shape,
        input_output_aliases=aliases,
        compiler_params=pltpu.CompilerParams(dimension_semantics=("parallel", "arbitrary"),
                                             vmem_limit_bytes=VMEM_LIMIT),
        name="attn",
    )(bounded, qt, k, vt, h, sz, sg, pc, wts["wmu"], wts["wout"])


def _to_head_lanes(a):
    n = a.shape[-1]
    half = QK_ROPE_DIM // 2
    split = LANES // 2 - half
    zeros = lambda w: jnp.zeros(a.shape[:-1] + (w,), a.dtype)
    rot1 = a[..., QK_NOPE_DIM:QK_NOPE_DIM + half] if n > QK_NOPE_DIM else zeros(half)
    rot2 = a[..., QK_NOPE_DIM + half:] if n > QK_NOPE_DIM else zeros(half)
    assert [HEAD_LANES[d] for d in (0, split, QK_NOPE_DIM, QK_NOPE_DIM + half)] == [half, LANES // 2 + half, 0,
                                                                                     LANES // 2]
    return jnp.concatenate([rot1, a[..., :split], rot2, a[..., split:QK_NOPE_DIM], zeros(HEAD_PAD - QK_HEAD_DIM)],
                           axis=-1)


def _prepare_weights(norm_gain, w_in, pool_w_group, pool_scale, pool_w_up, q_a_norm_gain, kv_a_norm_gain, w_q_b,
                     w_kv_b, q_norm_gain, k_norm_gain, mla_w_up, w_out):
    depth = w_in.shape[0]
    u, z, cq, ckv, kr, zm, gp, gm = jnp.split(w_in.astype(BF16), (512, 1024, 1792, 2048, 2080, 2592, 3616), axis=2)
    kr_block = _to_head_lanes(jnp.concatenate([jnp.zeros(kr.shape[:-1] + (QK_NOPE_DIM,), kr.dtype), kr], axis=-1))
    win = jnp.concatenate([u, z, cq, ckv, zm, gp, gm, kr_block], axis=2)
    wqb = _to_head_lanes(w_q_b.reshape(depth, Q_LORA_RANK, N_HEADS, QK_HEAD_DIM)).reshape(depth, Q_LORA_RANK, -1)
    wkv = w_kv_b.reshape(depth, KV_LORA_RANK, N_HEADS, QK_NOPE_DIM + V_HEAD_DIM)
    wkvb = jnp.concatenate([_to_head_lanes(wkv[..., :QK_NOPE_DIM]).reshape(depth, KV_LORA_RANK, -1),
                            wkv[..., QK_NOPE_DIM:].reshape(depth, KV_LORA_RANK, -1)], axis=2)
    eye = jnp.eye(pool_w_group.shape[1], dtype=pool_w_group.dtype)
    wg = (pool_w_group[:, :, :, None, :] * eye[None, :, None, :, None]).reshape(depth, POOL_WIDTH, POOL_WIDTH)
    q_scale = math.log2(math.e) / math.sqrt(QK_HEAD_DIM)
    block = jnp.arange(2 * HEAD_PAD) // HEAD_PAD
    row = lambda g: g[:, None, :]
    return {
        "ones": (block[:, None] == block[None, :]).astype(BF16),
        "ng": row(norm_gain),
        "win": win,
        "wg": wg.astype(BF16),
        "ps": row(pool_scale),
        "wpu": pool_w_up.astype(BF16),
        "gqa": row(q_a_norm_gain),
        "gkva": row(kv_a_norm_gain),
        "wqb": wqb.astype(BF16),
        "wkvb": wkvb.astype(BF16),
        "gqn": row(jnp.tile(_to_head_lanes(q_norm_gain * q_scale), (1, 2))),
        "gkn": row(jnp.tile(_to_head_lanes(k_norm_gain), (1, 2))),
        "wmu": mla_w_up.astype(BF16),
        "wout": w_out.astype(BF16),
    }


def kernel(x, positions, meta_tokens, norm_gain, w_in, pool_w_group, pool_scale, pool_w_up, q_a_norm_gain,
           kv_a_norm_gain, w_q_b, w_kv_b, q_norm_gain, k_norm_gain, mla_w_up, w_out):
    b, seq, _ = x.shape
    lp = PAD_FRONT + N_META + seq + PAD_BACK
    assert lp % ROW_TILE == 0 and ROW_TILE % ATT_TILE == 0

    meta_pos = jnp.broadcast_to(jnp.arange(N_META, dtype=jnp.int32)[None], (b, N_META))
    pos = jnp.concatenate([jnp.zeros((b, PAD_FRONT), jnp.int32), meta_pos, positions + N_META,
                           jnp.zeros((b, PAD_BACK), jnp.int32)], axis=1)

    half = QK_ROPE_DIM // 2
    inv_freq = ROPE_THETA ** (-jnp.arange(half, dtype=F32) / half)
    freq_lane = _to_head_lanes(jnp.concatenate([jnp.zeros((QK_NOPE_DIM,), F32), inv_freq, inv_freq]))[None, :]
    tabs = _rope_tables(pos, freq_lane)

    wts = _prepare_weights(norm_gain, w_in, pool_w_group, pool_scale, pool_w_up, q_a_norm_gain, kv_a_norm_gain,
                           w_q_b, w_kv_b, q_norm_gain, k_norm_gain, mla_w_up, w_out)
    bounded = _scores_bounded(q_norm_gain, k_norm_gain)
    q, k, v, sz, sg, pc, h = _pre(x, tabs, wts, 0, meta=meta_tokens.astype(x.dtype))
    for l in range(DEPTH):
        if l > 0:
            q, k, v, sz, sg, pc = _pre(h, tabs, wts, l)
        h = _attention(bounded, l, q, k, v, h, sz, sg, pc, wts, final=(l == DEPTH - 1))

    return h
```

```python
import functools
import math

import jax
import jax.numpy as jnp
from jax import lax
from jax.experimental import pallas as pl
from jax.experimental.pallas import tpu as pltpu

F32 = jnp.float32
BF16 = jnp.bfloat16

D_MODEL = 1024
DEPTH = 4
N_META = 16
POOL_WIDTH = 512
POOL_WINDOWS = (2, 4, 8, 16)
POOL_GROUP_DIM = 128
N_HEADS = 8
QK_NOPE_DIM = 64
QK_ROPE_DIM = 32
QK_HEAD_DIM = 96
V_HEAD_DIM = 64
MLA_WIDTH = 512
KV_LORA_RANK = 256
Q_LORA_RANK = 768
ROPE_THETA = 10000.0
NORM_EPS = 1e-6
MASK_VALUE = -1e30
PAD_KEY_SCORE = MASK_VALUE

LANES = 128
HEAD_PAD = LANES
PAD_FRONT = 240
PAD_BACK = 0
ROW_TILE = 768
ATT_TILE = 256
KV_GROUPS = (8, 2)
SCORE_BOUND = 32.0
HALO = 16
VMEM_LIMIT = 56 * 1024 * 1024

C_U, C_Z, C_Q, C_KV, C_ZM, C_GP, C_GM, C_KR, C_END = 0, 512, 1024, 1792, 2048, 2560, 3584, 4608, 4736


def _rms(x, gain, n):
    inv = lax.rsqrt(jnp.sum(x * x, axis=-1, keepdims=True) * (1.0 / n) + NORM_EPS)
    return x * inv * gain


def _head_lane(d):
    half = QK_ROPE_DIM // 2
    if d >= QK_NOPE_DIM:
        r = d - QK_NOPE_DIM
        return r if r < half else LANES // 2 + (r - half)
    return half + d if d < LANES // 2 - half else LANES // 2 + half + (d - (LANES // 2 - half))


HEAD_LANES = [_head_lane(d) for d in range(QK_HEAD_DIM)]


def _head_norm_rope(x, ones_ref, gain, c, s):
    sq = x * x
    hi = sq.astype(BF16)
    lo = (sq - hi.astype(F32)).astype(BF16)
    ss = (jnp.dot(hi, ones_ref[...], preferred_element_type=F32)
          + jnp.dot(lo, ones_ref[...], preferred_element_type=F32))
    xn = x * lax.rsqrt(ss * (1.0 / QK_HEAD_DIM) + NORM_EPS) * gain
    out = []
    for j in range(x.shape[1] // LANES):
        blk = xn[:, j * LANES:(j + 1) * LANES]
        out.append(blk * c + pltpu.roll(blk, LANES // 2, axis=1) * s)
    return out


def _rope_table_kernel(pos_ref, freq_ref, c_ref, s_ref):
    ang = pos_ref[0].astype(F32) * freq_ref[...]
    lane = lax.broadcasted_iota(jnp.int32, ang.shape, 1)
    c_ref[0] = jnp.cos(ang)
    s_ref[0] = jnp.where(lane < LANES // 2, -jnp.sin(ang), jnp.sin(ang))


def _pre_kernel(*refs, first_layer):
    n_src = (1 + ROW_TILE // ATT_TILE) if first_layer else 1
    src, refs = refs[:n_src], refs[n_src:]
    (c_ref, s_ref, ng_ref, win_ref, wg_ref, ps_ref, wpu_ref, gqa_ref, gkva_ref, wqb_ref, wkvb_ref, gqn_ref, gkn_ref,
     ones_ref, q_ref, k_ref, v_ref, sz_ref, sg_ref, pc_ref, *rest) = refs
    halo_ref = rest[-1]
    i = pl.program_id(1)
    if first_layer:
        meta_ref, blocks = src[0], src[1:]
        lead = jnp.concatenate([jnp.zeros((PAD_FRONT, D_MODEL), F32), meta_ref[...]], axis=0)
        x = jnp.concatenate([jnp.where(i == 0, lead, blocks[0][0])] + [blk[0] for blk in blocks[1:]], axis=0)
        rest[0][0] = x
    else:
        x = src[0][0]
    t = x.shape[0]
    hn = _rms(x, ng_ref[...], D_MODEL).astype(BF16)

    def proj(lo, hi):
        return jnp.dot(hn, win_ref[:, lo:hi], preferred_element_type=F32)

    c, s = c_ref[0], s_ref[0]
    cqn = _rms(proj(C_Q, C_KV), gqa_ref[...], Q_LORA_RANK).astype(BF16)
    qp = jnp.dot(cqn, wqb_ref[...], preferred_element_type=F32)
    ckvn = _rms(proj(C_KV, C_ZM), gkva_ref[...], KV_LORA_RANK).astype(BF16)
    kv = jnp.dot(ckvn, wkvb_ref[...], preferred_element_type=F32)
    krp = proj(C_KR, C_END)
    krp2 = jnp.concatenate([krp, krp], axis=1)
    row = i * t + lax.broadcasted_iota(jnp.int32, (t, 1), 0)
    sentinel = lax.broadcasted_iota(jnp.int32, (1, HEAD_PAD), 1) == QK_HEAD_DIM
    q_mark = jnp.where(sentinel, 1.0, 0.0)
    k_mark = jnp.where(sentinel & (row < PAD_FRONT), PAD_KEY_SCORE, 0.0)
    width = 2 * HEAD_PAD
    for pair in range(N_HEADS // 2):
        lo = pair * width
        qs = _head_norm_rope(qp[:, lo:lo + width], ones_ref, gqn_ref[...], c, s)
        ks = _head_norm_rope(kv[:, lo:lo + width] + krp2, ones_ref, gkn_ref[...], c, s)
        for odd in range(2):
            q_ref[0, 2 * pair + odd] = (qs[odd] + q_mark).astype(BF16)
            k_ref[0, 2 * pair + odd] = (ks[odd] + k_mark).astype(BF16)
        v0 = N_HEADS * HEAD_PAD + pair * LANES
        vbt = kv[:, v0:v0 + LANES].T.astype(BF16)
        for ch in range(t // ATT_TILE):
            v_ref[0, pair, ch] = vbt[:, ch * ATT_TILE:(ch + 1) * ATT_TILE]

    @pl.when(i == 0)
    def _():
        halo_ref[...] = jnp.zeros_like(halo_ref)

    u = proj(C_U, C_Z)
    uext = jnp.concatenate([halo_ref[...], u], axis=0)
    halo_ref[...] = u[t - HALO:, :]
    avail = jnp.maximum(row - (PAD_FRONT - 1), 1).astype(F32)
    mixed = []
    for g, w in enumerate(POOL_WINDOWS):
        a = uext[:, g * LANES:(g + 1) * LANES]
        ug = a[HALO:, :]
        step = 1
        while step < w:
            a = a + pltpu.roll(a, step, axis=0)
            step *= 2
        inv_cnt = 1.0 / jnp.minimum(avail, float(w))
        mixed.append((a[HALO:, :] * inv_cnt - ug).astype(BF16))
    ymix = jnp.dot(jnp.concatenate(mixed, axis=1), wg_ref[...], preferred_element_type=F32) * ps_ref[...]
    zp = proj(C_Z, C_Q)
    pooled = (ymix * (zp * jax.nn.sigmoid(zp))).astype(BF16)
    y_pool = jnp.dot(pooled, wpu_ref[...], preferred_element_type=F32)
    pc_ref[0] = (jax.nn.sigmoid(proj(C_GP, C_GM)) * y_pool).astype(pc_ref.dtype)

    zm = proj(C_ZM, C_GP)
    sz_ref[0] = (zm * jax.nn.sigmoid(zm)).astype(sz_ref.dtype)
    sg_ref[0] = jax.nn.sigmoid(proj(C_GM, C_KR)).astype(sg_ref.dtype)


def _attn_kernel(bounded_ref, qt_ref, k_ref, vt_ref, h_ref, sz_ref, sg_ref, pc_ref, wmu_ref, wout_ref, out_ref,
                 m_ref, l_ref, acc_ref, o_ref, *, layer):
    qi = pl.program_id(1)
    tq = qt_ref.shape[2]
    ck = vt_ref.shape[4]
    heads = range(qt_ref.shape[1])

    def span_scores(c0, n):
        start = pl.multiple_of(c0 * ck, ck)
        return [lax.dot_general(k_ref[0, hh, pl.ds(start, n * ck), :], qt_ref[0, hh], (((1,), (1,)), ((), ())),
                                preferred_element_type=F32)
                for hh in heads]

    def visible(c0, shape):
        kpos = c0 * ck + lax.broadcasted_iota(jnp.int32, shape, 0)
        qpos = qi * tq + lax.broadcasted_iota(jnp.int32, shape, 1)
        return (kpos <= qpos) | (qpos < PAD_FRONT)

    def weighted_values(c0, n, hh, p):
        pb = p.astype(BF16)
        lo = (hh % 2) * V_HEAD_DIM
        out = None
        for c in range(n):
            vt = vt_ref[0, hh // 2, c0 + c][lo:lo + V_HEAD_DIM, :]
            d = jnp.dot(vt, pb[c * ck:(c + 1) * ck, :], preferred_element_type=F32)
            out = d if out is None else out + d
        return out

    def step_bounded(c0, n, masked):
        scores = span_scores(c0, n)
        for hh in heads:
            p = jnp.exp2(scores[hh])
            if masked:
                p = jnp.where(visible(c0, p.shape), p, 0.0)
            l_ref[hh] += jnp.sum(p.reshape(n * ck // 8, 8, tq), axis=0)
            acc_ref[hh] += weighted_values(c0, n, hh, p)

    def step_general(c0, n, masked):
        scores = span_scores(c0, n)
        for hh in heads:
            s = scores[hh]
            if masked:
                s = jnp.where(visible(c0, s.shape), s, MASK_VALUE)
            m_prev = m_ref[hh]
            m_new = jnp.maximum(m_prev, jnp.max(s, axis=0, keepdims=True))
            alpha = jnp.exp2(m_prev - m_new)
            p = jnp.exp2(s - m_new)
            l_ref[hh, 0:1, :] = alpha * l_ref[hh, 0:1, :] + jnp.sum(p, axis=0, keepdims=True)
            acc_ref[hh] = alpha * acc_ref[hh] + weighted_values(c0, n, hh, p)
            m_ref[hh] = m_new

    def run(step, groups):
        m_ref[...] = jnp.full_like(m_ref, -jnp.inf)
        l_ref[...] = jnp.zeros_like(l_ref)
        acc_ref[...] = jnp.zeros_like(acc_ref)

        first, left = 0, qi
        for group in groups:
            def body(g, carry, first=first, group=group):
                step(first + g * group, group, False)
                return carry

            trips = left // group
            lax.fori_loop(0, trips, body, 0)
            first, left = first + trips * group, left - trips * group
        for r in range(groups[-1]):
            @pl.when(left == r)
            def _(r=r):
                step(qi - r, r + 1, True)
                out_t = jnp.concatenate([acc_ref[hh] / jnp.sum(l_ref[hh], axis=0, keepdims=True) for hh in heads],
                                        axis=0)
                o_ref[...] = out_t.T

    bounded = bounded_ref[layer] != 0

    @pl.when(bounded)
    def _():
        run(step_bounded, KV_GROUPS)

    @pl.when(jnp.logical_not(bounded))
    def _():
        run(step_general, KV_GROUPS[-1:])

    om = (o_ref[...] * sz_ref[0].astype(F32)).astype(BF16)
    y_mla = jnp.dot(om, wmu_ref[...], preferred_element_type=F32)
    merged = (pc_ref[0].astype(F32) + sg_ref[0].astype(F32) * y_mla).astype(BF16)
    out_ref[0] = h_ref[0] + jnp.dot(merged, wout_ref[...], preferred_element_type=F32)


def _const_spec(shape):
    return pl.BlockSpec(shape, lambda *_: (0,) * len(shape))


def _layer_spec(stacked, layer):
    rest = stacked.shape[1:]
    return pl.BlockSpec((None,) + rest, lambda *_: (layer,) + (0,) * len(rest))


def _row_spec(t, width):
    return pl.BlockSpec((1, t, width), lambda b, i: (b, i, 0))


def _rope_tables(pos_pad, freq_lane):
    b, lp = pos_pad.shape
    t = ROW_TILE
    out = jax.ShapeDtypeStruct((b, lp, LANES), F32)
    return pl.pallas_call(
        _rope_table_kernel,
        grid=(b, lp // t),
        in_specs=[_row_spec(t, 1), _const_spec((1, LANES))],
        out_specs=[_row_spec(t, LANES)] * 2,
        out_shape=[out] * 2,
        name="rope_tables",
    )(pos_pad[:, :, None], freq_lane)


def _pre(h, tabs, wts, layer, meta=None):
    first_layer = meta is not None
    b = h.shape[0]
    lp = tabs[0].shape[1]
    t = ROW_TILE
    nch = t // ATT_TILE
    if first_layer:
        assert PAD_FRONT + N_META == ATT_TILE and PAD_BACK == 0
        token_block = lambda j: pl.BlockSpec((1, ATT_TILE, D_MODEL),
                                             lambda bb, i: (bb, jnp.maximum(nch * i - 1 + j, 0), 0))
        src = [meta] + [h] * nch
        src_specs = [_const_spec(meta.shape)] + [token_block(j) for j in range(nch)]
        extra_specs, extra_shapes = [_row_spec(t, D_MODEL)], [jax.ShapeDtypeStruct((b, lp, D_MODEL), F32)]
    else:
        src, src_specs, extra_specs, extra_shapes = [h], [_row_spec(t, D_MODEL)], [], []
    k_shape = jax.ShapeDtypeStruct((b, N_HEADS, lp, HEAD_PAD), BF16)
    k_spec = pl.BlockSpec((1, N_HEADS, t, HEAD_PAD), lambda bb, i: (bb, 0, i, 0))
    qt_shape, qt_spec = k_shape, k_spec
    vt_shape = jax.ShapeDtypeStruct((b, N_HEADS // 2, lp // ATT_TILE, 2 * V_HEAD_DIM, ATT_TILE), BF16)
    vt_spec = pl.BlockSpec((1, N_HEADS // 2, nch, 2 * V_HEAD_DIM, ATT_TILE), lambda bb, i: (bb, 0, i, 0, 0))
    weights = [wts[name] for name in ("ng", "win", "wg", "ps", "wpu", "gqa", "gkva", "wqb", "wkvb", "gqn", "gkn")]
    return pl.pallas_call(
        functools.partial(_pre_kernel, first_layer=first_layer),
        grid=(b, lp // t),
        in_specs=(src_specs + [_row_spec(t, LANES)] * 2 + [_layer_spec(w, layer) for w in weights]
                  + [_const_spec(wts["ones"].shape)]),
        out_specs=[qt_spec, k_spec, vt_spec, _row_spec(t, MLA_WIDTH), _row_spec(t, D_MODEL),
                   _row_spec(t, D_MODEL)] + extra_specs,
        out_shape=[qt_shape, k_shape, vt_shape,
                   jax.ShapeDtypeStruct((b, lp, MLA_WIDTH), BF16),
                   jax.ShapeDtypeStruct((b, lp, D_MODEL), BF16),
                   jax.ShapeDtypeStruct((b, lp, D_MODEL), BF16)] + extra_shapes,
        scratch_shapes=[pltpu.VMEM((HALO, POOL_WIDTH), F32)],
        compiler_params=pltpu.CompilerParams(dimension_semantics=("parallel", "arbitrary"),
                                             vmem_limit_bytes=VMEM_LIMIT),
        name="pre",
    )(*src, *tabs, *weights, wts["ones"])


def _scores_bounded(q_gain, k_gain):
    limit = (QK_HEAD_DIM * jnp.max(jnp.abs(q_gain), axis=-1) * jnp.max(jnp.abs(k_gain), axis=-1)
             * (math.log2(math.e) / math.sqrt(QK_HEAD_DIM)) * 1.02)
    return (limit <= SCORE_BOUND).astype(jnp.int32)


def _attention(bounded, layer, qt, k, vt, h, sz, sg, pc, wts, final):
    b, nh, lp, _ = k.shape
    t = ATT_TILE
    assert PAD_FRONT + N_META == t
    if final:
        out_spec = pl.BlockSpec((1, t, D_MODEL), lambda bb, i, f: (bb, jnp.maximum(i - 1, 0), 0))
        out_shape = jax.ShapeDtypeStruct((b, lp - t - PAD_BACK, D_MODEL), F32)
        aliases = {}
    else:
        out_spec = pl.BlockSpec((1, t, D_MODEL), lambda bb, i, f: (bb, i, 0))
        out_shape = jax.ShapeDtypeStruct(h.shape, F32)
        aliases = {4: 0}
    single = pl.Buffered(1)
    rows = lambda width: pl.BlockSpec((1, t, width), lambda bb, i, f: (bb, i, 0))
    weight = lambda w: pl.BlockSpec((None,) + w.shape[1:], lambda bb, i, f: (layer, 0, 0), pipeline_mode=single)
    grid_spec = pltpu.PrefetchScalarGridSpec(
        num_scalar_prefetch=1,
        grid=(b, lp // t),
        in_specs=[pl.BlockSpec((1, nh, t, HEAD_PAD), lambda bb, i, f: (bb, 0, i, 0)),
                  pl.BlockSpec((1, nh, lp, HEAD_PAD), lambda bb, i, f: (bb, 0, 0, 0), pipeline_mode=single),
                  pl.BlockSpec((1, nh // 2, lp // t, 2 * V_HEAD_DIM, t), lambda bb, i, f: (bb, 0, 0, 0, 0),
                               pipeline_mode=single),
                  rows(D_MODEL), rows(MLA_WIDTH), rows(D_MODEL), rows(D_MODEL),
                  weight(wts["wmu"]), weight(wts["wout"])],
        out_specs=out_spec,
        scratch_shapes=[pltpu.VMEM((nh, 1, t), F32),
                        pltpu.VMEM((nh, 8, t), F32),
                        pltpu.VMEM((nh, V_HEAD_DIM, t), F32),
                        pltpu.VMEM((t, nh * V_HEAD_DIM), F32)],
    )
    return pl.pallas_call(
        functools.partial(_attn_kernel, layer=layer),
        grid_spec=grid_spec,
        out_shape=out_shape,
        input_output_aliases=aliases,
        compiler_params=pltpu.CompilerParams(dimension_semantics=("parallel", "arbitrary"),
                                             vmem_limit_bytes=VMEM_LIMIT),
        name="attn",
    )(bounded, qt, k, vt, h, sz, sg, pc, wts["wmu"], wts["wout"])


def _to_head_lanes(a):
    n = a.shape[-1]
    half = QK_ROPE_DIM // 2
    split = LANES // 2 - half
    zeros = lambda w: jnp.zeros(a.shape[:-1] + (w,), a.dtype)
    rot1 = a[..., QK_NOPE_DIM:QK_NOPE_DIM + half] if n > QK_NOPE_DIM else zeros(half)
    rot2 = a[..., QK_NOPE_DIM + half:] if n > QK_NOPE_DIM else zeros(half)
    assert [HEAD_LANES[d] for d in (0, split, QK_NOPE_DIM, QK_NOPE_DIM + half)] == [half, LANES // 2 + half, 0,
                                                                                     LANES // 2]
    return jnp.concatenate([rot1, a[..., :split], rot2, a[..., split:QK_NOPE_DIM], zeros(HEAD_PAD - QK_HEAD_DIM)],
                           axis=-1)


def _prepare_weights(norm_gain, w_in, pool_w_group, pool_scale, pool_w_up, q_a_norm_gain, kv_a_norm_gain, w_q_b,
                     w_kv_b, q_norm_gain, k_norm_gain, mla_w_up, w_out):
    depth = w_in.shape[0]
    u, z, cq, ckv, kr, zm, gp, gm = jnp.split(w_in.astype(BF16), (512, 1024, 1792, 2048, 2080, 2592, 3616), axis=2)
    kr_block = _to_head_lanes(jnp.concatenate([jnp.zeros(kr.shape[:-1] + (QK_NOPE_DIM,), kr.dtype), kr], axis=-1))
    win = jnp.concatenate([u, z, cq, ckv, zm, gp, gm, kr_block], axis=2)
    wqb = _to_head_lanes(w_q_b.reshape(depth, Q_LORA_RANK, N_HEADS, QK_HEAD_DIM)).reshape(depth, Q_LORA_RANK, -1)
    wkv = w_kv_b.reshape(depth, KV_LORA_RANK, N_HEADS, QK_NOPE_DIM + V_HEAD_DIM)
    wkvb = jnp.concatenate([_to_head_lanes(wkv[..., :QK_NOPE_DIM]).reshape(depth, KV_LORA_RANK, -1),
                            wkv[..., QK_NOPE_DIM:].reshape(depth, KV_LORA_RANK, -1)], axis=2)
    eye = jnp.eye(pool_w_group.shape[1], dtype=pool_w_group.dtype)
    wg = (pool_w_group[:, :, :, None, :] * eye[None, :, None, :, None]).reshape(depth, POOL_WIDTH, POOL_WIDTH)
    q_scale = math.log2(math.e) / math.sqrt(QK_HEAD_DIM)
    block = jnp.arange(2 * HEAD_PAD) // HEAD_PAD
    row = lambda g: g[:, None, :]
    return {
        "ones": (block[:, None] == block[None, :]).astype(BF16),
        "ng": row(norm_gain),
        "win": win,
        "wg": wg.astype(BF16),
        "ps": row(pool_scale),
        "wpu": pool_w_up.astype(BF16),
        "gqa": row(q_a_norm_gain),
        "gkva": row(kv_a_norm_gain),
        "wqb": wqb.astype(BF16),
        "wkvb": wkvb.astype(BF16),
        "gqn": row(jnp.tile(_to_head_lanes(q_norm_gain * q_scale), (1, 2))),
        "gkn": row(jnp.tile(_to_head_lanes(k_norm_gain), (1, 2))),
        "wmu": mla_w_up.astype(BF16),
        "wout": w_out.astype(BF16),
    }


def kernel(x, positions, meta_tokens, norm_gain, w_in, pool_w_group, pool_scale, pool_w_up, q_a_norm_gain,
           kv_a_norm_gain, w_q_b, w_kv_b, q_norm_gain, k_norm_gain, mla_w_up, w_out):
    b, seq, _ = x.shape
    lp = PAD_FRONT + N_META + seq + PAD_BACK
    assert lp % ROW_TILE == 0 and ROW_TILE % ATT_TILE == 0

    meta_pos = jnp.broadcast_to(jnp.arange(N_META, dtype=jnp.int32)[None], (b, N_META))
    pos = jnp.concatenate([jnp.zeros((b, PAD_FRONT), jnp.int32), meta_pos, positions + N_META,
                           jnp.zeros((b, PAD_BACK), jnp.int32)], axis=1)

    half = QK_ROPE_DIM // 2
    inv_freq = ROPE_THETA ** (-jnp.arange(half, dtype=F32) / half)
    freq_lane = _to_head_lanes(jnp.concatenate([jnp.zeros((QK_NOPE_DIM,), F32), inv_freq, inv_freq]))[None, :]
    tabs = _rope_tables(pos, freq_lane)

    wts = _prepare_weights(norm_gain, w_in, pool_w_group, pool_scale, pool_w_up, q_a_norm_gain, kv_a_norm_gain,
                           w_q_b, w_kv_b, q_norm_gain, k_norm_gain, mla_w_up, w_out)
    bounded = _scores_bounded(q_norm_gain, k_norm_gain)
    q, k, v, sz, sg, pc, h = _pre(x, tabs, wts, 0, meta=meta_tokens.astype(x.dtype))
    for l in range(DEPTH):
        if l > 0:
            q, k, v, sz, sg, pc = _pre(h, tabs, wts, l)
        h = _attention(bounded, l, q, k, v, h, sz, sg, pc, wts, final=(l == DEPTH - 1))

    return h
```

```python
import functools
import math

import jax
import jax.numpy as jnp
from jax import lax
from jax.experimental import pallas as pl
from jax.experimental.pallas import tpu as pltpu

F32 = jnp.float32
BF16 = jnp.bfloat16

D_MODEL = 1024
DEPTH = 4
N_META = 16
POOL_WIDTH = 512
POOL_WINDOWS = (2, 4, 8, 16)
POOL_GROUP_DIM = 128
N_HEADS = 8
QK_NOPE_DIM = 64
QK_ROPE_DIM = 32
QK_HEAD_DIM = 96
V_HEAD_DIM = 64
MLA_WIDTH = 512
KV_LORA_RANK = 256
Q_LORA_RANK = 768
ROPE_THETA = 10000.0
NORM_EPS = 1e-6
MASK_VALUE = -1e30
PAD_KEY_SCORE = MASK_VALUE

LANES = 128
HEAD_PAD = LANES
PAD_FRONT = 240
PAD_BACK = 0
ROW_TILE = 768
ATT_TILE = 256
KV_GROUPS = (8, 2)
SCORE_BOUND = 32.0
HALO = 16
VMEM_LIMIT = 56 * 1024 * 1024

C_U, C_Z, C_Q, C_KV, C_ZM, C_GP, C_GM, C_KR, C_END = 0, 512, 1024, 1792, 2048, 2560, 3584, 4608, 4736


def _rms(x, gain, n):
    inv = lax.rsqrt(jnp.sum(x * x, axis=-1, keepdims=True) * (1.0 / n) + NORM_EPS)
    return x * inv * gain


def _head_lane(d):
    half = QK_ROPE_DIM // 2
    if d >= QK_NOPE_DIM:
        r = d - QK_NOPE_DIM
        return r if r < half else LANES // 2 + (r - half)
    return half + d if d < LANES // 2 - half else LANES // 2 + half + (d - (LANES // 2 - half))


HEAD_LANES = [_head_lane(d) for d in range(QK_HEAD_DIM)]


def _head_norm_rope(x, ones_ref, gain, c, s):
    sq = x * x
    hi = sq.astype(BF16)
    lo = (sq - hi.astype(F32)).astype(BF16)
    ss = (jnp.dot(hi, ones_ref[...], preferred_element_type=F32)
          + jnp.dot(lo, ones_ref[...], preferred_element_type=F32))
    xn = x * lax.rsqrt(ss * (1.0 / QK_HEAD_DIM) + NORM_EPS) * gain
    out = []
    for j in range(x.shape[1] // LANES):
        blk = xn[:, j * LANES:(j + 1) * LANES]
        out.append(blk * c + pltpu.roll(blk, LANES // 2, axis=1) * s)
    return out


def _rope_table_kernel(pos_ref, freq_ref, c_ref, s_ref):
    ang = pos_ref[0].astype(F32) * freq_ref[...]
    lane = lax.broadcasted_iota(jnp.int32, ang.shape, 1)
    c_ref[0] = jnp.cos(ang)
    s_ref[0] = jnp.where(lane < LANES // 2, -jnp.sin(ang), jnp.sin(ang))


def _pre_kernel(*refs, first_layer):
    n_src = (1 + ROW_TILE // ATT_TILE) if first_layer else 1
    src, refs = refs[:n_src], refs[n_src:]
    (c_ref, s_ref, ng_ref, win_ref, wg_ref, ps_ref, wpu_ref, gqa_ref, gkva_ref, wqb_ref, wkvb_ref, gqn_ref, gkn_ref,
     ones_ref, q_ref, k_ref, v_ref, sz_ref, sg_ref, pc_ref, *rest) = refs
    halo_ref = rest[-1]
    i = pl.program_id(1)
    if first_layer:
        meta_ref, blocks = src[0], src[1:]
        lead = jnp.concatenate([jnp.zeros((PAD_FRONT, D_MODEL), F32), meta_ref[...]], axis=0)
        x = jnp.concatenate([jnp.where(i == 0, lead, blocks[0][0])] + [blk[0] for blk in blocks[1:]], axis=0)
        rest[0][0] = x
    else:
        x = src[0][0]
    t = x.shape[0]
    hn = _rms(x, ng_ref[...], D_MODEL).astype(BF16)

    def proj(lo, hi):
        return jnp.dot(hn, win_ref[:, lo:hi], preferred_element_type=F32)

    c, s = c_ref[0], s_ref[0]
    cqn = _rms(proj(C_Q, C_KV), gqa_ref[...], Q_LORA_RANK).astype(BF16)
    qp = jnp.dot(cqn, wqb_ref[...], preferred_element_type=F32)
    ckvn = _rms(proj(C_KV, C_ZM), gkva_ref[...], KV_LORA_RANK).astype(BF16)
    kv = jnp.dot(ckvn, wkvb_ref[...], preferred_element_type=F32)
    krp = proj(C_KR, C_END)
    krp2 = jnp.concatenate([krp, krp], axis=1)
    row = i * t + lax.broadcasted_iota(jnp.int32, (t, 1), 0)
    sentinel = lax.broadcasted_iota(jnp.int32, (1, HEAD_PAD), 1) == QK_HEAD_DIM
    q_mark = jnp.where(sentinel, 1.0, 0.0)
    k_mark = jnp.where(sentinel & (row < PAD_FRONT), PAD_KEY_SCORE, 0.0)
    width = 2 * HEAD_PAD
    for pair in range(N_HEADS // 2):
        lo = pair * width
        qs = _head_norm_rope(qp[:, lo:lo + width], ones_ref, gqn_ref[...], c, s)
        ks = _head_norm_rope(kv[:, lo:lo + width] + krp2, ones_ref, gkn_ref[...], c, s)
        for odd in range(2):
            q_ref[0, 2 * pair + odd] = (qs[odd] + q_mark).astype(BF16)
            k_ref[0, 2 * pair + odd] = (ks[odd] + k_mark).astype(BF16)
        v0 = N_HEADS * HEAD_PAD + pair * LANES
        vbt = kv[:, v0:v0 + LANES].T.astype(BF16)
        for ch in range(t // ATT_TILE):
            v_ref[0, pair, ch] = vbt[:, ch * ATT_TILE:(ch + 1) * ATT_TILE]

    @pl.when(i == 0)
    def _():
        halo_ref[...] = jnp.zeros_like(halo_ref)

    u = proj(C_U, C_Z)
    uext = jnp.concatenate([halo_ref[...], u], axis=0)
    halo_ref[...] = u[t - HALO:, :]
    avail = jnp.maximum(row - (PAD_FRONT - 1), 1).astype(F32)
    mixed = []
    for g, w in enumerate(POOL_WINDOWS):
        a = uext[:, g * LANES:(g + 1) * LANES]
        ug = a[HALO:, :]
        step = 1
        while step < w:
            a = a + pltpu.roll(a, step, axis=0)
            step *= 2
        inv_cnt = 1.0 / jnp.minimum(avail, float(w))
        mixed.append((a[HALO:, :] * inv_cnt - ug).astype(BF16))
    ymix = jnp.dot(jnp.concatenate(mixed, axis=1), wg_ref[...], preferred_element_type=F32) * ps_ref[...]
    zp = proj(C_Z, C_Q)
    pooled = (ymix * (zp * jax.nn.sigmoid(zp))).astype(BF16)
    y_pool = jnp.dot(pooled, wpu_ref[...], preferred_element_type=F32)
    pc_ref[0] = (jax.nn.sigmoid(proj(C_GP, C_GM)) * y_pool).astype(pc_ref.dtype)

    zm = proj(C_ZM, C_GP)
    sz_ref[0] = (zm * jax.nn.sigmoid(zm)).astype(sz_ref.dtype)
    sg_ref[0] = jax.nn.sigmoid(proj(C_GM, C_KR)).astype(sg_ref.dtype)


def _attn_kernel(bounded_ref, qt_ref, knew_ref, vtnew_ref, h_ref, sz_ref, sg_ref, pc_ref, wmu_ref, wout_ref, out_ref,
                 k_ref, vt_ref, m_ref, l_ref, acc_ref, o_ref, *, layer):
    qi = pl.program_id(1)
    tq = qt_ref.shape[2]
    ck = vt_ref.shape[4]
    heads = range(qt_ref.shape[1])

    k_ref[0, :, pl.ds(pl.multiple_of(qi * ck, ck), ck), :] = knew_ref[0]
    vt_ref[0, :, qi] = vtnew_ref[0, :, 0]

    def span_scores(c0, n):
        start = pl.multiple_of(c0 * ck, ck)
        return [lax.dot_general(k_ref[0, hh, pl.ds(start, n * ck), :], qt_ref[0, hh], (((1,), (1,)), ((), ())),
                                preferred_element_type=F32)
                for hh in heads]

    def visible(c0, shape):
        kpos = c0 * ck + lax.broadcasted_iota(jnp.int32, shape, 0)
        qpos = qi * tq + lax.broadcasted_iota(jnp.int32, shape, 1)
        return (kpos <= qpos) | (qpos < PAD_FRONT)

    def weighted_values(c0, n, hh, p):
        pb = p.astype(BF16)
        lo = (hh % 2) * V_HEAD_DIM
        out = None
        for c in range(n):
            vt = vt_ref[0, hh // 2, c0 + c][lo:lo + V_HEAD_DIM, :]
            d = jnp.dot(vt, pb[c * ck:(c + 1) * ck, :], preferred_element_type=F32)
            out = d if out is None else out + d
        return out

    def step_bounded(c0, n, masked):
        scores = span_scores(c0, n)
        for hh in heads:
            p = jnp.exp2(scores[hh])
            if masked:
                p = jnp.where(visible(c0, p.shape), p, 0.0)
            l_ref[hh] += jnp.sum(p.reshape(n * ck // 8, 8, tq), axis=0)
            acc_ref[hh] += weighted_values(c0, n, hh, p)

    def step_general(c0, n, masked):
        scores = span_scores(c0, n)
        for hh in heads:
            s = scores[hh]
            if masked:
                s = jnp.where(visible(c0, s.shape), s, MASK_VALUE)
            m_prev = m_ref[hh]
            m_new = jnp.maximum(m_prev, jnp.max(s, axis=0, keepdims=True))
            alpha = jnp.exp2(m_prev - m_new)
            p = jnp.exp2(s - m_new)
            l_ref[hh, 0:1, :] = alpha * l_ref[hh, 0:1, :] + jnp.sum(p, axis=0, keepdims=True)
            acc_ref[hh] = alpha * acc_ref[hh] + weighted_values(c0, n, hh, p)
            m_ref[hh] = m_new

    def run(step, groups):
        m_ref[...] = jnp.full_like(m_ref, -jnp.inf)
        l_ref[...] = jnp.zeros_like(l_ref)
        acc_ref[...] = jnp.zeros_like(acc_ref)

        first, left = 0, qi
        for group in groups:
            def body(g, carry, first=first, group=group):
                step(first + g * group, group, False)
                return carry

            trips = left // group
            lax.fori_loop(0, trips, body, 0)
            first, left = first + trips * group, left - trips * group
        for r in range(groups[-1]):
            @pl.when(left == r)
            def _(r=r):
                step(qi - r, r + 1, True)
                out_t = jnp.concatenate([acc_ref[hh] / jnp.sum(l_ref[hh], axis=0, keepdims=True) for hh in heads],
                                        axis=0)
                o_ref[...] = out_t.T

    bounded = bounded_ref[layer] != 0

    @pl.when(bounded)
    def _():
        run(step_bounded, KV_GROUPS)

    @pl.when(jnp.logical_not(bounded))
    def _():
        run(step_general, KV_GROUPS[-1:])

    om = (o_ref[...] * sz_ref[0].astype(F32)).astype(BF16)
    y_mla = jnp.dot(om, wmu_ref[...], preferred_element_type=F32)
    merged = (pc_ref[0].astype(F32) + sg_ref[0].astype(F32) * y_mla).astype(BF16)
    out_ref[0] = h_ref[0] + jnp.dot(merged, wout_ref[...], preferred_element_type=F32)


def _const_spec(shape):
    return pl.BlockSpec(shape, lambda *_: (0,) * len(shape))


def _layer_spec(stacked, layer):
    rest = stacked.shape[1:]
    return pl.BlockSpec((None,) + rest, lambda *_: (layer,) + (0,) * len(rest))


def _row_spec(t, width):
    return pl.BlockSpec((1, t, width), lambda b, i: (b, i, 0))


def _rope_tables(pos_pad, freq_lane):
    b, lp = pos_pad.shape
    t = ROW_TILE
    out = jax.ShapeDtypeStruct((b, lp, LANES), F32)
    return pl.pallas_call(
        _rope_table_kernel,
        grid=(b, lp // t),
        in_specs=[_row_spec(t, 1), _const_spec((1, LANES))],
        out_specs=[_row_spec(t, LANES)] * 2,
        out_shape=[out] * 2,
        name="rope_tables",
    )(pos_pad[:, :, None], freq_lane)


def _pre(h, tabs, wts, layer, meta=None):
    first_layer = meta is not None
    b = h.shape[0]
    lp = tabs[0].shape[1]
    t = ROW_TILE
    nch = t // ATT_TILE
    if first_layer:
        assert PAD_FRONT + N_META == ATT_TILE and PAD_BACK == 0
        token_block = lambda j: pl.BlockSpec((1, ATT_TILE, D_MODEL),
                                             lambda bb, i: (bb, jnp.maximum(nch * i - 1 + j, 0), 0))
        src = [meta] + [h] * nch
        src_specs = [_const_spec(meta.shape)] + [token_block(j) for j in range(nch)]
        extra_specs, extra_shapes = [_row_spec(t, D_MODEL)], [jax.ShapeDtypeStruct((b, lp, D_MODEL), F32)]
    else:
        src, src_specs, extra_specs, extra_shapes = [h], [_row_spec(t, D_MODEL)], [], []
    k_shape = jax.ShapeDtypeStruct((b, N_HEADS, lp, HEAD_PAD), BF16)
    k_spec = pl.BlockSpec((1, N_HEADS, t, HEAD_PAD), lambda bb, i: (bb, 0, i, 0))
    qt_shape, qt_spec = k_shape, k_spec
    vt_shape = jax.ShapeDtypeStruct((b, N_HEADS // 2, lp // ATT_TILE, 2 * V_HEAD_DIM, ATT_TILE), BF16)
    vt_spec = pl.BlockSpec((1, N_HEADS // 2, nch, 2 * V_HEAD_DIM, ATT_TILE), lambda bb, i: (bb, 0, i, 0, 0))
    weights = [wts[name] for name in ("ng", "win", "wg", "ps", "wpu", "gqa", "gkva", "wqb", "wkvb", "gqn", "gkn")]
    return pl.pallas_call(
        functools.partial(_pre_kernel, first_layer=first_layer),
        grid=(b, lp // t),
        in_specs=(src_specs + [_row_spec(t, LANES)] * 2 + [_layer_spec(w, layer) for w in weights]
                  + [_const_spec(wts["ones"].shape)]),
        out_specs=[qt_spec, k_spec, vt_spec, _row_spec(t, MLA_WIDTH), _row_spec(t, D_MODEL),
                   _row_spec(t, D_MODEL)] + extra_specs,
        out_shape=[qt_shape, k_shape, vt_shape,
                   jax.ShapeDtypeStruct((b, lp, MLA_WIDTH), BF16),
                   jax.ShapeDtypeStruct((b, lp, D_MODEL), BF16),
                   jax.ShapeDtypeStruct((b, lp, D_MODEL), BF16)] + extra_shapes,
        scratch_shapes=[pltpu.VMEM((HALO, POOL_WIDTH), F32)],
        compiler_params=pltpu.CompilerParams(dimension_semantics=("parallel", "arbitrary"),
                                             vmem_limit_bytes=VMEM_LIMIT),
        name="pre",
    )(*src, *tabs, *weights, wts["ones"])


def _scores_bounded(q_gain, k_gain):
    limit = (QK_HEAD_DIM * jnp.max(jnp.abs(q_gain), axis=-1) * jnp.max(jnp.abs(k_gain), axis=-1)
             * (math.log2(math.e) / math.sqrt(QK_HEAD_DIM)) * 1.02)
    return (limit <= SCORE_BOUND).astype(jnp.int32)


def _attention(bounded, layer, qt, k, vt, h, sz, sg, pc, wts, final):
    b, nh, lp, _ = k.shape
    t = ATT_TILE
    assert PAD_FRONT + N_META == t
    if final:
        out_spec = pl.BlockSpec((1, t, D_MODEL), lambda bb, i, f: (bb, jnp.maximum(i - 1, 0), 0))
        out_shape = jax.ShapeDtypeStruct((b, lp - t - PAD_BACK, D_MODEL), F32)
        aliases = {}
    else:
        out_spec = pl.BlockSpec((1, t, D_MODEL), lambda bb, i, f: (bb, i, 0))
        out_shape = jax.ShapeDtypeStruct(h.shape, F32)
        aliases = {4: 0}
    single = pl.Buffered(1)
    rows = lambda width: pl.BlockSpec((1, t, width), lambda bb, i, f: (bb, i, 0))
    weight = lambda w: pl.BlockSpec((None,) + w.shape[1:], lambda bb, i, f: (layer, 0, 0), pipeline_mode=single)
    grid_spec = pltpu.PrefetchScalarGridSpec(
        num_scalar_prefetch=1,
        grid=(b, lp // t),
        in_specs=[pl.BlockSpec((1, nh, t, HEAD_PAD), lambda bb, i, f: (bb, 0, i, 0)),
                  pl.BlockSpec((1, nh, t, HEAD_PAD), lambda bb, i, f: (bb, 0, i, 0)),
                  pl.BlockSpec((1, nh // 2, 1, 2 * V_HEAD_DIM, t), lambda bb, i, f: (bb, 0, i, 0, 0)),
                  rows(D_MODEL), rows(MLA_WIDTH), rows(D_MODEL), rows(D_MODEL),
                  weight(wts["wmu"]), weight(wts["wout"])],
        out_specs=out_spec,
        scratch_shapes=[pltpu.VMEM((1, nh, lp, HEAD_PAD), BF16),
                        pltpu.VMEM((1, nh // 2, lp // t, 2 * V_HEAD_DIM, t), BF16),
                        pltpu.VMEM((nh, 1, t), F32),
                        pltpu.VMEM((nh, 8, t), F32),
                        pltpu.VMEM((nh, V_HEAD_DIM, t), F32),
                        pltpu.VMEM((t, nh * V_HEAD_DIM), F32)],
    )
    return pl.pallas_call(
        functools.partial(_attn_kernel, layer=layer),
        grid_spec=grid_spec,
        out_shape=out_shape,
        input_output_aliases=aliases,
        compiler_params=pltpu.CompilerParams(dimension_semantics=("parallel", "arbitrary"),
                                             vmem_limit_bytes=VMEM_LIMIT),
        name="attn",
    )(bounded, qt, k, vt, h, sz, sg, pc, wts["wmu"], wts["wout"])


def _to_head_lanes(a):
    n = a.shape[-1]
    half = QK_ROPE_DIM // 2
    split = LANES // 2 - half
    zeros = lambda w: jnp.zeros(a.shape[:-1] + (w,), a.dtype)
    rot1 = a[..., QK_NOPE_DIM:QK_NOPE_DIM + half] if n > QK_NOPE_DIM else zeros(half)
    rot2 = a[..., QK_NOPE_DIM + half:] if n > QK_NOPE_DIM else zeros(half)
    assert [HEAD_LANES[d] for d in (0, split, QK_NOPE_DIM, QK_NOPE_DIM + half)] == [half, LANES // 2 + half, 0,
                                                                                     LANES // 2]
    return jnp.concatenate([rot1, a[..., :split], rot2, a[..., split:QK_NOPE_DIM], zeros(HEAD_PAD - QK_HEAD_DIM)],
                           axis=-1)


def _prepare_weights(norm_gain, w_in, pool_w_group, pool_scale, pool_w_up, q_a_norm_gain, kv_a_norm_gain, w_q_b,
                     w_kv_b, q_norm_gain, k_norm_gain, mla_w_up, w_out):
    depth = w_in.shape[0]
    u, z, cq, ckv, kr, zm, gp, gm = jnp.split(w_in.astype(BF16), (512, 1024, 1792, 2048, 2080, 2592, 3616), axis=2)
    kr_block = _to_head_lanes(jnp.concatenate([jnp.zeros(kr.shape[:-1] + (QK_NOPE_DIM,), kr.dtype), kr], axis=-1))
    win = jnp.concatenate([u, z, cq, ckv, zm, gp, gm, kr_block], axis=2)
    wqb = _to_head_lanes(w_q_b.reshape(depth, Q_LORA_RANK, N_HEADS, QK_HEAD_DIM)).reshape(depth, Q_LORA_RANK, -1)
    wkv = w_kv_b.reshape(depth, KV_LORA_RANK, N_HEADS, QK_NOPE_DIM + V_HEAD_DIM)
    wkvb = jnp.concatenate([_to_head_lanes(wkv[..., :QK_NOPE_DIM]).reshape(depth, KV_LORA_RANK, -1),
                            wkv[..., QK_NOPE_DIM:].reshape(depth, KV_LORA_RANK, -1)], axis=2)
    eye = jnp.eye(pool_w_group.shape[1], dtype=pool_w_group.dtype)
    wg = (pool_w_group[:, :, :, None, :] * eye[None, :, None, :, None]).reshape(depth, POOL_WIDTH, POOL_WIDTH)
    q_scale = math.log2(math.e) / math.sqrt(QK_HEAD_DIM)
    block = jnp.arange(2 * HEAD_PAD) // HEAD_PAD
    row = lambda g: g[:, None, :]
    return {
        "ones": (block[:, None] == block[None, :]).astype(BF16),
        "ng": row(norm_gain),
        "win": win,
        "wg": wg.astype(BF16),
        "ps": row(pool_scale),
        "wpu": pool_w_up.astype(BF16),
        "gqa": row(q_a_norm_gain),
        "gkva": row(kv_a_norm_gain),
        "wqb": wqb.astype(BF16),
        "wkvb": wkvb.astype(BF16),
        "gqn": row(jnp.tile(_to_head_lanes(q_norm_gain * q_scale), (1, 2))),
        "gkn": row(jnp.tile(_to_head_lanes(k_norm_gain), (1, 2))),
        "wmu": mla_w_up.astype(BF16),
        "wout": w_out.astype(BF16),
    }


def kernel(x, positions, meta_tokens, norm_gain, w_in, pool_w_group, pool_scale, pool_w_up, q_a_norm_gain,
           kv_a_norm_gain, w_q_b, w_kv_b, q_norm_gain, k_norm_gain, mla_w_up, w_out):
    b, seq, _ = x.shape
    lp = PAD_FRONT + N_META + seq + PAD_BACK
    assert lp % ROW_TILE == 0 and ROW_TILE % ATT_TILE == 0

    meta_pos = jnp.broadcast_to(jnp.arange(N_META, dtype=jnp.int32)[None], (b, N_META))
    pos = jnp.concatenate([jnp.zeros((b, PAD_FRONT), jnp.int32), meta_pos, positions + N_META,
                           jnp.zeros((b, PAD_BACK), jnp.int32)], axis=1)

    half = QK_ROPE_DIM // 2
    inv_freq = ROPE_THETA ** (-jnp.arange(half, dtype=F32) / half)
    freq_lane = _to_head_lanes(jnp.concatenate([jnp.zeros((QK_NOPE_DIM,), F32), inv_freq, inv_freq]))[None, :]
    tabs = _rope_tables(pos, freq_lane)

    wts = _prepare_weights(norm_gain, w_in, pool_w_group, pool_scale, pool_w_up, q_a_norm_gain, kv_a_norm_gain,
                           w_q_b, w_kv_b, q_norm_gain, k_norm_gain, mla_w_up, w_out)
    bounded = _scores_bounded(q_norm_gain, k_norm_gain)
    q, k, v, sz, sg, pc, h = _pre(x, tabs, wts, 0, meta=meta_tokens.astype(x.dtype))
    for l in range(DEPTH):
        if l > 0:
            q, k, v, sz, sg, pc = _pre(h, tabs, wts, l)
        h = _attention(bounded, l, q, k, v, h, sz, sg, pc, wts, final=(l == DEPTH - 1))

    return h
```

```python
import functools
import math

import jax
import jax.numpy as jnp
from jax import lax
from jax.experimental import pallas as pl
from jax.experimental.pallas import tpu as pltpu

F32 = jnp.float32
BF16 = jnp.bfloat16

D_MODEL = 1024
DEPTH = 4
N_META = 16
POOL_WIDTH = 512
POOL_WINDOWS = (2, 4, 8, 16)
N_HEADS = 8
QK_NOPE_DIM = 64
QK_ROPE_DIM = 32
QK_HEAD_DIM = 96
V_HEAD_DIM = 64
MLA_WIDTH = 512
KV_LORA_RANK = 256
Q_LORA_RANK = 768
ROPE_THETA = 10000.0
NORM_EPS = 1e-6
MASK_VALUE = -1e30
PAD_KEY_SCORE = MASK_VALUE

LANES = 128
HEAD_PAD = LANES
PAD_FRONT = 240
PAD_BACK = 0
ROW_TILE = 768
ATT_TILE = 256
KV_GROUPS = (8, 2)
SCORE_BOUND = 32.0
HALO = 16
VMEM_LIMIT = 56 * 1024 * 1024

C_U, C_Z, C_Q, C_KV, C_ZM, C_GP, C_GM, C_KR, C_END = 0, 512, 1024, 1792, 2048, 2560, 3584, 4608, 4736


def _rms(x, gain, n):
    inv = lax.rsqrt(jnp.sum(x * x, axis=-1, keepdims=True) * (1.0 / n) + NORM_EPS)
    return x * inv * gain


def _head_lane(d):
    half = QK_ROPE_DIM // 2
    if d >= QK_NOPE_DIM:
        r = d - QK_NOPE_DIM
        return r if r < half else LANES // 2 + (r - half)
    return half + d if d < LANES // 2 - half else LANES // 2 + half + (d - (LANES // 2 - half))


HEAD_LANES = [_head_lane(d) for d in range(QK_HEAD_DIM)]


def _head_norm_rope(x, ones_ref, gain, c, s):
    sq = x * x
    hi = sq.astype(BF16)
    lo = (sq - hi.astype(F32)).astype(BF16)
    ss = (jnp.dot(hi, ones_ref[...], preferred_element_type=F32)
          + jnp.dot(lo, ones_ref[...], preferred_element_type=F32))
    xn = x * lax.rsqrt(ss * (1.0 / QK_HEAD_DIM) + NORM_EPS) * gain
    out = []
    for j in range(x.shape[1] // LANES):
        blk = xn[:, j * LANES:(j + 1) * LANES]
        out.append(blk * c + pltpu.roll(blk, LANES // 2, axis=1) * s)
    return out


def _rope_table_kernel(pos_ref, freq_ref, phase_ref, c_ref, s_ref):
    half = QK_ROPE_DIM // 2
    t = jnp.cos(pos_ref[0].astype(F32) * freq_ref[...] + phase_ref[...])
    lane = lax.broadcasted_iota(jnp.int32, t.shape, 1)
    rotary = lane % (LANES // 2) < half
    sin = pltpu.roll(t, LANES - half, axis=1)
    c_ref[0] = jnp.where(rotary, t, 1.0)
    s_ref[0] = jnp.where(rotary, jnp.where(lane < LANES // 2, -sin, sin), 0.0)


def _pre_kernel(*refs, first_layer):
    n_src = (1 + ROW_TILE // ATT_TILE) if first_layer else 1
    src, refs = refs[:n_src], refs[n_src:]
    (c_ref, s_ref, ng_ref, win_ref, wg_ref, ps_ref, wpu_ref, gqa_ref, gkva_ref, wqb_ref, wkvb_ref, gqn_ref, gkn_ref,
     ones_ref, q_ref, k_ref, v_ref, sz_ref, sg_ref, pc_ref, *rest) = refs
    halo_ref = rest[-1]
    i = pl.program_id(1)
    if first_layer:
        meta_ref, blocks = src[0], src[1:]
        lead = jnp.concatenate([jnp.zeros((PAD_FRONT, D_MODEL), F32), meta_ref[...]], axis=0)
        x = jnp.concatenate([jnp.where(i == 0, lead, blocks[0][0])] + [blk[0] for blk in blocks[1:]], axis=0)
        rest[0][0] = x
    else:
        x = src[0][0]
    t = x.shape[0]
    hn = _rms(x, ng_ref[...], D_MODEL).astype(BF16)

    def proj(lo, hi):
        return jnp.dot(hn, win_ref[:, lo:hi], preferred_element_type=F32)

    c, s = c_ref[0], s_ref[0]
    cqn = _rms(proj(C_Q, C_KV), gqa_ref[...], Q_LORA_RANK).astype(BF16)
    qp = jnp.dot(cqn, wqb_ref[...], preferred_element_type=F32)
    ckvn = _rms(proj(C_KV, C_ZM), gkva_ref[...], KV_LORA_RANK).astype(BF16)
    kv = jnp.dot(ckvn, wkvb_ref[...], preferred_element_type=F32)
    krp = proj(C_KR, C_END)
    krp2 = jnp.concatenate([krp, krp], axis=1)
    row = i * t + lax.broadcasted_iota(jnp.int32, (t, 1), 0)
    sentinel = lax.broadcasted_iota(jnp.int32, (1, HEAD_PAD), 1) == QK_HEAD_DIM
    q_mark = jnp.where(sentinel, 1.0, 0.0)
    k_mark = jnp.where(sentinel & (row < PAD_FRONT), PAD_KEY_SCORE, 0.0)
    width = 2 * HEAD_PAD
    for pair in range(N_HEADS // 2):
        lo = pair * width
        qs = _head_norm_rope(qp[:, lo:lo + width], ones_ref, gqn_ref[...], c, s)
        ks = _head_norm_rope(kv[:, lo:lo + width] + krp2, ones_ref, gkn_ref[...], c, s)
        for odd in range(2):
            q_ref[0, 2 * pair + odd] = (qs[odd] + q_mark).astype(BF16)
            k_ref[0, 2 * pair + odd] = (ks[odd] + k_mark).astype(BF16)
        v0 = N_HEADS * HEAD_PAD + pair * LANES
        vbt = kv[:, v0:v0 + LANES].T.astype(BF16)
        for ch in range(t // ATT_TILE):
            v_ref[0, pair, ch] = vbt[:, ch * ATT_TILE:(ch + 1) * ATT_TILE]

    @pl.when(i == 0)
    def _():
        halo_ref[...] = jnp.zeros_like(halo_ref)

    u = proj(C_U, C_Z)
    uext = jnp.concatenate([halo_ref[...], u], axis=0)
    halo_ref[...] = u[t - HALO:, :]
    avail = jnp.maximum(row - (PAD_FRONT - 1), 1).astype(F32)
    mixed = []
    for g, w in enumerate(POOL_WINDOWS):
        a = uext[:, g * LANES:(g + 1) * LANES]
        ug = a[HALO:, :]
        step = 1
        while step < w:
            a = a + pltpu.roll(a, step, axis=0)
            step *= 2
        inv_cnt = 1.0 / jnp.minimum(avail, float(w))
        mixed.append((a[HALO:, :] * inv_cnt - ug).astype(BF16))
    ymix = jnp.dot(jnp.concatenate(mixed, axis=1), wg_ref[...], preferred_element_type=F32) * ps_ref[...]
    zp = proj(C_Z, C_Q)
    pooled = (ymix * (zp * jax.nn.sigmoid(zp))).astype(BF16)
    y_pool = jnp.dot(pooled, wpu_ref[...], preferred_element_type=F32)
    pc_ref[0] = (jax.nn.sigmoid(proj(C_GP, C_GM)) * y_pool).astype(pc_ref.dtype)

    zm = proj(C_ZM, C_GP)
    sz_ref[0] = (zm * jax.nn.sigmoid(zm)).astype(sz_ref.dtype)
    sg_ref[0] = jax.nn.sigmoid(proj(C_GM, C_KR)).astype(sg_ref.dtype)


def _attn_kernel(bounded_ref, q_ref, knew_ref, vtnew_ref, h_ref, sz_ref, sg_ref, pc_ref, wmu_ref, wout_ref, out_ref,
                 k_ref, vt_ref, m_ref, l_ref, acc_ref, o_ref, *, layer):
    qi = pl.program_id(1)
    tq = q_ref.shape[2]
    ck = vt_ref.shape[4]
    heads = range(q_ref.shape[1])

    k_ref[0, :, pl.ds(pl.multiple_of(qi * ck, ck), ck), :] = knew_ref[0]
    vt_ref[0, :, qi] = vtnew_ref[0, :, 0]

    def span_scores(c0, n):
        start = pl.multiple_of(c0 * ck, ck)
        return [lax.dot_general(k_ref[0, hh, pl.ds(start, n * ck), :], q_ref[0, hh], (((1,), (1,)), ((), ())),
                                preferred_element_type=F32)
                for hh in heads]

    def visible(c0, shape):
        kpos = c0 * ck + lax.broadcasted_iota(jnp.int32, shape, 0)
        qpos = qi * tq + lax.broadcasted_iota(jnp.int32, shape, 1)
        return (kpos <= qpos) | (qpos < PAD_FRONT)

    def weighted_values(c0, n, hh, p):
        pb = p.astype(BF16)
        lo = (hh % 2) * V_HEAD_DIM
        out = None
        for c in range(n):
            vt = vt_ref[0, hh // 2, c0 + c][lo:lo + V_HEAD_DIM, :]
            d = jnp.dot(vt, pb[c * ck:(c + 1) * ck, :], preferred_element_type=F32)
            out = d if out is None else out + d
        return out

    def step_bounded(c0, n, masked):
        scores = span_scores(c0, n)
        for hh in heads:
            p = jnp.exp2(scores[hh])
            if masked:
                p = jnp.where(visible(c0, p.shape), p, 0.0)
            l_ref[hh] += jnp.sum(p.reshape(n * ck // 8, 8, tq), axis=0)
            acc_ref[hh] += weighted_values(c0, n, hh, p)

    def step_general(c0, n, masked):
        scores = span_scores(c0, n)
        for hh in heads:
            s = scores[hh]
            if masked:
                s = jnp.where(visible(c0, s.shape), s, MASK_VALUE)
            m_prev = m_ref[hh]
            m_new = jnp.maximum(m_prev, jnp.max(s, axis=0, keepdims=True))
            alpha = jnp.exp2(m_prev - m_new)
            p = jnp.exp2(s - m_new)
            l_ref[hh, 0:1, :] = alpha * l_ref[hh, 0:1, :] + jnp.sum(p, axis=0, keepdims=True)
            acc_ref[hh] = alpha * acc_ref[hh] + weighted_values(c0, n, hh, p)
            m_ref[hh] = m_new

    def run(step, groups):
        m_ref[...] = jnp.full_like(m_ref, -jnp.inf)
        l_ref[...] = jnp.zeros_like(l_ref)
        acc_ref[...] = jnp.zeros_like(acc_ref)

        first, left = 0, qi
        for group in groups:
            def body(g, carry, first=first, group=group):
                step(first + g * group, group, False)
                return carry

            trips = left // group
            lax.fori_loop(0, trips, body, 0)
            first, left = first + trips * group, left - trips * group
        for r in range(groups[-1]):
            @pl.when(left == r)
            def _(r=r):
                step(qi - r, r + 1, True)
                out_t = jnp.concatenate([acc_ref[hh] / jnp.sum(l_ref[hh], axis=0, keepdims=True) for hh in heads],
                                        axis=0)
                o_ref[...] = out_t.T

    bounded = bounded_ref[layer] != 0

    @pl.when(bounded)
    def _():
        run(step_bounded, KV_GROUPS)

    @pl.when(jnp.logical_not(bounded))
    def _():
        run(step_general, KV_GROUPS[-1:])

    om = (o_ref[...] * sz_ref[0].astype(F32)).astype(BF16)
    y_mla = jnp.dot(om, wmu_ref[...], preferred_element_type=F32)
    merged = (pc_ref[0].astype(F32) + sg_ref[0].astype(F32) * y_mla).astype(BF16)
    out_ref[0] = h_ref[0] + jnp.dot(merged, wout_ref[...], preferred_element_type=F32)


def _const_spec(shape):
    return pl.BlockSpec(shape, lambda *_: (0,) * len(shape))


def _layer_spec(stacked, layer):
    rest = stacked.shape[1:]
    return pl.BlockSpec((None,) + rest, lambda *_: (layer,) + (0,) * len(rest))


def _row_spec(t, width):
    return pl.BlockSpec((1, t, width), lambda b, i: (b, i, 0))


def _rope_tables(pos_pad):
    b, lp = pos_pad.shape
    t = ROW_TILE
    half = QK_ROPE_DIM // 2
    inv_freq = ROPE_THETA ** (-jnp.arange(half, dtype=F32) / half)
    rest = jnp.zeros((LANES // 2 - 2 * half,), F32)
    freq = jnp.tile(jnp.concatenate([inv_freq, inv_freq, rest]), 2)[None, :]
    phase = jnp.tile(jnp.concatenate([jnp.zeros((half,), F32), jnp.full((half,), -math.pi / 2, F32), rest]),
                     2)[None, :]
    out = jax.ShapeDtypeStruct((b, lp, LANES), F32)
    return pl.pallas_call(
        _rope_table_kernel,
        grid=(b, lp // t),
        in_specs=[_row_spec(t, 1), _const_spec((1, LANES)), _const_spec((1, LANES))],
        out_specs=[_row_spec(t, LANES)] * 2,
        out_shape=[out] * 2,
        name="rope_tables",
    )(pos_pad[:, :, None], freq, phase)


def _pre(h, tabs, wts, layer, meta=None):
    first_layer = meta is not None
    b = h.shape[0]
    lp = tabs[0].shape[1]
    t = ROW_TILE
    nch = t // ATT_TILE
    if first_layer:
        assert PAD_FRONT + N_META == ATT_TILE and PAD_BACK == 0
        token_block = lambda j: pl.BlockSpec((1, ATT_TILE, D_MODEL),
                                             lambda bb, i: (bb, jnp.maximum(nch * i - 1 + j, 0), 0))
        src = [meta] + [h] * nch
        src_specs = [_const_spec(meta.shape)] + [token_block(j) for j in range(nch)]
        extra_specs, extra_shapes = [_row_spec(t, D_MODEL)], [jax.ShapeDtypeStruct((b, lp, D_MODEL), F32)]
    else:
        src, src_specs, extra_specs, extra_shapes = [h], [_row_spec(t, D_MODEL)], [], []
    k_shape = jax.ShapeDtypeStruct((b, N_HEADS, lp, HEAD_PAD), BF16)
    k_spec = pl.BlockSpec((1, N_HEADS, t, HEAD_PAD), lambda bb, i: (bb, 0, i, 0))
    vt_shape = jax.ShapeDtypeStruct((b, N_HEADS // 2, lp // ATT_TILE, 2 * V_HEAD_DIM, ATT_TILE), BF16)
    vt_spec = pl.BlockSpec((1, N_HEADS // 2, nch, 2 * V_HEAD_DIM, ATT_TILE), lambda bb, i: (bb, 0, i, 0, 0))
    weights = [wts[name] for name in ("ng", "win", "wg", "ps", "wpu", "gqa", "gkva", "wqb", "wkvb", "gqn", "gkn")]
    return pl.pallas_call(
        functools.partial(_pre_kernel, first_layer=first_layer),
        grid=(b, lp // t),
        in_specs=(src_specs + [_row_spec(t, LANES)] * 2 + [_layer_spec(w, layer) for w in weights]
                  + [_const_spec(wts["ones"].shape)]),
        out_specs=[k_spec, k_spec, vt_spec, _row_spec(t, MLA_WIDTH), _row_spec(t, D_MODEL),
                   _row_spec(t, D_MODEL)] + extra_specs,
        out_shape=[k_shape, k_shape, vt_shape,
                   jax.ShapeDtypeStruct((b, lp, MLA_WIDTH), BF16),
                   jax.ShapeDtypeStruct((b, lp, D_MODEL), BF16),
                   jax.ShapeDtypeStruct((b, lp, D_MODEL), BF16)] + extra_shapes,
        scratch_shapes=[pltpu.VMEM((HALO, POOL_WIDTH), F32)],
        compiler_params=pltpu.CompilerParams(dimension_semantics=("parallel", "arbitrary"),
                                             vmem_limit_bytes=VMEM_LIMIT),
        name="pre",
    )(*src, *tabs, *weights, wts["ones"])


def _scores_bounded(q_gain, k_gain):
    limit = (QK_HEAD_DIM * jnp.max(jnp.abs(q_gain), axis=-1) * jnp.max(jnp.abs(k_gain), axis=-1)
             * (math.log2(math.e) / math.sqrt(QK_HEAD_DIM)) * 1.02)
    return (limit <= SCORE_BOUND).astype(jnp.int32)


def _attention(bounded, layer, qt, k, vt, h, sz, sg, pc, wts, final):
    b, nh, lp, _ = k.shape
    t = ATT_TILE
    assert PAD_FRONT + N_META == t
    if final:
        out_spec = pl.BlockSpec((1, t, D_MODEL), lambda bb, i, f: (bb, jnp.maximum(i - 1, 0), 0))
        out_shape = jax.ShapeDtypeStruct((b, lp - t - PAD_BACK, D_MODEL), F32)
        aliases = {}
    else:
        out_spec = pl.BlockSpec((1, t, D_MODEL), lambda bb, i, f: (bb, i, 0))
        out_shape = jax.ShapeDtypeStruct(h.shape, F32)
        aliases = {4: 0}
    single = pl.Buffered(1)
    rows = lambda width: pl.BlockSpec((1, t, width), lambda bb, i, f: (bb, i, 0))
    weight = lambda w: pl.BlockSpec((None,) + w.shape[1:], lambda bb, i, f: (layer, 0, 0), pipeline_mode=single)
    grid_spec = pltpu.PrefetchScalarGridSpec(
        num_scalar_prefetch=1,
        grid=(b, lp // t),
        in_specs=[pl.BlockSpec((1, nh, t, HEAD_PAD), lambda bb, i, f: (bb, 0, i, 0)),
                  pl.BlockSpec((1, nh, t, HEAD_PAD), lambda bb, i, f: (bb, 0, i, 0)),
                  pl.BlockSpec((1, nh // 2, 1, 2 * V_HEAD_DIM, t), lambda bb, i, f: (bb, 0, i, 0, 0)),
                  rows(D_MODEL), rows(MLA_WIDTH), rows(D_MODEL), rows(D_MODEL),
                  weight(wts["wmu"]), weight(wts["wout"])],
        out_specs=out_spec,
        scratch_shapes=[pltpu.VMEM((1, nh, lp, HEAD_PAD), BF16),
                        pltpu.VMEM((1, nh // 2, lp // t, 2 * V_HEAD_DIM, t), BF16),
                        pltpu.VMEM((nh, 1, t), F32),
                        pltpu.VMEM((nh, 8, t), F32),
                        pltpu.VMEM((nh, V_HEAD_DIM, t), F32),
                        pltpu.VMEM((t, nh * V_HEAD_DIM), F32)],
    )
    return pl.pallas_call(
        functools.partial(_attn_kernel, layer=layer),
        grid_spec=grid_spec,
        out_shape=out_shape,
        input_output_aliases=aliases,
        compiler_params=pltpu.CompilerParams(dimension_semantics=("parallel", "arbitrary"),
                                             vmem_limit_bytes=VMEM_LIMIT),
        name="attn",
    )(bounded, qt, k, vt, h, sz, sg, pc, wts["wmu"], wts["wout"])


def _to_head_lanes(a):
    n = a.shape[-1]
    half = QK_ROPE_DIM // 2
    split = LANES // 2 - half
    zeros = lambda w: jnp.zeros(a.shape[:-1] + (w,), a.dtype)
    rot1 = a[..., QK_NOPE_DIM:QK_NOPE_DIM + half] if n > QK_NOPE_DIM else zeros(half)
    rot2 = a[..., QK_NOPE_DIM + half:] if n > QK_NOPE_DIM else zeros(half)
    assert [HEAD_LANES[d] for d in (0, split, QK_NOPE_DIM, QK_NOPE_DIM + half)] == [half, LANES // 2 + half, 0,
                                                                                     LANES // 2]
    return jnp.concatenate([rot1, a[..., :split], rot2, a[..., split:QK_NOPE_DIM], zeros(HEAD_PAD - QK_HEAD_DIM)],
                           axis=-1)


def _prepare_weights(norm_gain, w_in, pool_w_group, pool_scale, pool_w_up, q_a_norm_gain, kv_a_norm_gain, w_q_b,
                     w_kv_b, q_norm_gain, k_norm_gain, mla_w_up, w_out):
    depth = w_in.shape[0]
    u, z, cq, ckv, kr, zm, gp, gm = jnp.split(w_in.astype(BF16), (512, 1024, 1792, 2048, 2080, 2592, 3616), axis=2)
    kr_block = _to_head_lanes(jnp.concatenate([jnp.zeros(kr.shape[:-1] + (QK_NOPE_DIM,), kr.dtype), kr], axis=-1))
    win = jnp.concatenate([u, z, cq, ckv, zm, gp, gm, kr_block], axis=2)
    wqb = _to_head_lanes(w_q_b.reshape(depth, Q_LORA_RANK, N_HEADS, QK_HEAD_DIM)).reshape(depth, Q_LORA_RANK, -1)
    wkv = w_kv_b.reshape(depth, KV_LORA_RANK, N_HEADS, QK_NOPE_DIM + V_HEAD_DIM)
    wkvb = jnp.concatenate([_to_head_lanes(wkv[..., :QK_NOPE_DIM]).reshape(depth, KV_LORA_RANK, -1),
                            wkv[..., QK_NOPE_DIM:].reshape(depth, KV_LORA_RANK, -1)], axis=2)
    eye = jnp.eye(pool_w_group.shape[1], dtype=pool_w_group.dtype)
    wg = (pool_w_group[:, :, :, None, :] * eye[None, :, None, :, None]).reshape(depth, POOL_WIDTH, POOL_WIDTH)
    q_scale = math.log2(math.e) / math.sqrt(QK_HEAD_DIM)
    block = jnp.arange(2 * HEAD_PAD) // HEAD_PAD
    row = lambda g: g[:, None, :]
    return {
        "ones": (block[:, None] == block[None, :]).astype(BF16),
        "ng": row(norm_gain),
        "win": win,
        "wg": wg.astype(BF16),
        "ps": row(pool_scale),
        "wpu": pool_w_up.astype(BF16),
        "gqa": row(q_a_norm_gain),
        "gkva": row(kv_a_norm_gain),
        "wqb": wqb.astype(BF16),
        "wkvb": wkvb.astype(BF16),
        "gqn": row(jnp.tile(_to_head_lanes(q_norm_gain * q_scale), (1, 2))),
        "gkn": row(jnp.tile(_to_head_lanes(k_norm_gain), (1, 2))),
        "wmu": mla_w_up.astype(BF16),
        "wout": w_out.astype(BF16),
    }


def kernel(x, positions, meta_tokens, norm_gain, w_in, pool_w_group, pool_scale, pool_w_up, q_a_norm_gain,
           kv_a_norm_gain, w_q_b, w_kv_b, q_norm_gain, k_norm_gain, mla_w_up, w_out):
    b, seq, _ = x.shape
    lp = PAD_FRONT + N_META + seq + PAD_BACK
    assert lp % ROW_TILE == 0 and ROW_TILE % ATT_TILE == 0

    meta_pos = jnp.broadcast_to(jnp.arange(N_META, dtype=jnp.int32)[None], (b, N_META))
    pos = jnp.concatenate([jnp.zeros((b, PAD_FRONT), jnp.int32), meta_pos, positions + N_META,
                           jnp.zeros((b, PAD_BACK), jnp.int32)], axis=1)

    tabs = _rope_tables(pos)

    wts = _prepare_weights(norm_gain, w_in, pool_w_group, pool_scale, pool_w_up, q_a_norm_gain, kv_a_norm_gain,
                           w_q_b, w_kv_b, q_norm_gain, k_norm_gain, mla_w_up, w_out)
    bounded = _scores_bounded(q_norm_gain, k_norm_gain)
    q, k, v, sz, sg, pc, h = _pre(x, tabs, wts, 0, meta=meta_tokens.astype(x.dtype))
    for l in range(DEPTH):
        if l > 0:
            q, k, v, sz, sg, pc = _pre(h, tabs, wts, l)
        h = _attention(bounded, l, q, k, v, h, sz, sg, pc, wts, final=(l == DEPTH - 1))

    return h
```

```python
import functools
import math

import jax
import jax.numpy as jnp
from jax import lax
from jax.experimental import pallas as pl
from jax.experimental.pallas import tpu as pltpu

F32 = jnp.float32
BF16 = jnp.bfloat16

D_MODEL = 1024
DEPTH = 4
N_META = 16
POOL_WIDTH = 512
POOL_WINDOWS = (2, 4, 8, 16)
N_HEADS = 8
QK_NOPE_DIM = 64
QK_ROPE_DIM = 32
QK_HEAD_DIM = 96
V_HEAD_DIM = 64
MLA_WIDTH = 512
KV_LORA_RANK = 256
Q_LORA_RANK = 768
ROPE_THETA = 10000.0
NORM_EPS = 1e-6
MASK_VALUE = -1e30
PAD_KEY_SCORE = MASK_VALUE

LANES = 128
HEAD_PAD = LANES
PAD_FRONT = 240
PAD_BACK = 0
ROW_TILE = 768
ATT_TILE = 256
KV_GROUPS = (8, 4, 2)
SCORE_BOUND = 32.0
HALO = 16
VMEM_LIMIT = 56 * 1024 * 1024

C_U, C_Z, C_Q, C_KV, C_ZM, C_GP, C_GM, C_KR, C_END = 0, 512, 1024, 1792, 2048, 2560, 3584, 4608, 4736


def _rms(x, gain, n):
    inv = lax.rsqrt(jnp.sum(x * x, axis=-1, keepdims=True) * (1.0 / n) + NORM_EPS)
    return x * inv * gain


def _head_lane(d):
    half = QK_ROPE_DIM // 2
    if d >= QK_NOPE_DIM:
        r = d - QK_NOPE_DIM
        return r if r < half else LANES // 2 + (r - half)
    return half + d if d < LANES // 2 - half else LANES // 2 + half + (d - (LANES // 2 - half))


HEAD_LANES = [_head_lane(d) for d in range(QK_HEAD_DIM)]


def _head_norm_rope(x, ones_ref, gain, c, s):
    sq = x * x
    hi = sq.astype(BF16)
    lo = (sq - hi.astype(F32)).astype(BF16)
    ss = (jnp.dot(hi, ones_ref[...], preferred_element_type=F32)
          + jnp.dot(lo, ones_ref[...], preferred_element_type=F32))
    xn = x * lax.rsqrt(ss * (1.0 / QK_HEAD_DIM) + NORM_EPS) * gain
    out = []
    for j in range(x.shape[1] // LANES):
        blk = xn[:, j * LANES:(j + 1) * LANES]
        out.append(blk * c + pltpu.roll(blk, LANES // 2, axis=1) * s)
    return out


def _rope_table_kernel(pos_ref, freq_ref, c_ref, s_ref):
    ang = pos_ref[0].astype(F32) * freq_ref[...]
    lane = lax.broadcasted_iota(jnp.int32, ang.shape, 1)
    c_ref[0] = jnp.cos(ang)
    s_ref[0] = jnp.where(lane < LANES // 2, -jnp.sin(ang), jnp.sin(ang))


def _pre_kernel(*refs, first_layer):
    n_src = (1 + ROW_TILE // ATT_TILE) if first_layer else 1
    src, refs = refs[:n_src], refs[n_src:]
    (c_ref, s_ref, ng_ref, win_ref, wg_ref, ps_ref, wpu_ref, gqa_ref, gkva_ref, wqb_ref, wkvb_ref, gqn_ref, gkn_ref,
     ones_ref, q_ref, k_ref, v_ref, sz_ref, sg_ref, pc_ref, *rest) = refs
    halo_ref = rest[-1]
    i = pl.program_id(1)
    if first_layer:
        meta_ref, blocks = src[0], src[1:]
        lead = jnp.concatenate([jnp.zeros((PAD_FRONT, D_MODEL), F32), meta_ref[...]], axis=0)
        x = jnp.concatenate([jnp.where(i == 0, lead, blocks[0][0])] + [blk[0] for blk in blocks[1:]], axis=0)
        rest[0][0] = x
    else:
        x = src[0][0]
    t = x.shape[0]
    hn = _rms(x, ng_ref[...], D_MODEL).astype(BF16)

    def proj(lo, hi):
        return jnp.dot(hn, win_ref[:, lo:hi], preferred_element_type=F32)

    c, s = c_ref[0], s_ref[0]
    cqn = _rms(proj(C_Q, C_KV), gqa_ref[...], Q_LORA_RANK).astype(BF16)
    qp = jnp.dot(cqn, wqb_ref[...], preferred_element_type=F32)
    ckvn = _rms(proj(C_KV, C_ZM), gkva_ref[...], KV_LORA_RANK).astype(BF16)
    kv = jnp.dot(ckvn, wkvb_ref[...], preferred_element_type=F32)
    krp = proj(C_KR, C_END)
    krp2 = jnp.concatenate([krp, krp], axis=1)
    row = i * t + lax.broadcasted_iota(jnp.int32, (t, 1), 0)
    sentinel = lax.broadcasted_iota(jnp.int32, (1, HEAD_PAD), 1) == QK_HEAD_DIM
    q_mark = jnp.where(sentinel, 1.0, 0.0)
    k_mark = jnp.where(sentinel & (row < PAD_FRONT), PAD_KEY_SCORE, 0.0)
    width = 2 * HEAD_PAD
    for pair in range(N_HEADS // 2):
        lo = pair * width
        qs = _head_norm_rope(qp[:, lo:lo + width], ones_ref, gqn_ref[...], c, s)
        ks = _head_norm_rope(kv[:, lo:lo + width] + krp2, ones_ref, gkn_ref[...], c, s)
        for odd in range(2):
            q_ref[0, 2 * pair + odd] = (qs[odd] + q_mark).astype(BF16)
            k_ref[0, 2 * pair + odd] = (ks[odd] + k_mark).astype(BF16)
        v0 = N_HEADS * HEAD_PAD + pair * LANES
        vbt = kv[:, v0:v0 + LANES].T.astype(BF16)
        for ch in range(t // ATT_TILE):
            v_ref[0, pair, ch] = vbt[:, ch * ATT_TILE:(ch + 1) * ATT_TILE]

    @pl.when(i == 0)
    def _():
        halo_ref[...] = jnp.zeros_like(halo_ref)

    u = proj(C_U, C_Z)
    uext = jnp.concatenate([halo_ref[...], u], axis=0)
    halo_ref[...] = u[t - HALO:, :]
    avail = jnp.maximum(row - (PAD_FRONT - 1), 1).astype(F32)
    mixed = []
    for g, w in enumerate(POOL_WINDOWS):
        a = uext[:, g * LANES:(g + 1) * LANES]
        ug = a[HALO:, :]
        step = 1
        while step < w:
            a = a + pltpu.roll(a, step, axis=0)
            step *= 2
        inv_cnt = 1.0 / jnp.minimum(avail, float(w))
        mixed.append((a[HALO:, :] * inv_cnt - ug).astype(BF16))
    ymix = jnp.dot(jnp.concatenate(mixed, axis=1), wg_ref[...], preferred_element_type=F32) * ps_ref[...]
    zp = proj(C_Z, C_Q)
    pooled = (ymix * (zp * jax.nn.sigmoid(zp))).astype(BF16)
    y_pool = jnp.dot(pooled, wpu_ref[...], preferred_element_type=F32)
    pc_ref[0] = (jax.nn.sigmoid(proj(C_GP, C_GM)) * y_pool).astype(pc_ref.dtype)

    zm = proj(C_ZM, C_GP)
    sz_ref[0] = (zm * jax.nn.sigmoid(zm)).astype(sz_ref.dtype)
    sg_ref[0] = jax.nn.sigmoid(proj(C_GM, C_KR)).astype(sg_ref.dtype)


def _attn_kernel(bounded_ref, q_ref, knew_ref, vtnew_ref, h_ref, sz_ref, sg_ref, pc_ref, wmu_ref, wout_ref, out_ref,
                 k_ref, vt_ref, m_ref, l_ref, acc_ref, o_ref, *, layer):
    qi = pl.program_id(1)
    tq = q_ref.shape[2]
    ck = vt_ref.shape[4]
    heads = range(q_ref.shape[1])

    k_ref[0, :, pl.ds(pl.multiple_of(qi * ck, ck), ck), :] = knew_ref[0]
    vt_ref[0, :, qi] = vtnew_ref[0, :, 0]

    def span_scores(c0, n):
        start = pl.multiple_of(c0 * ck, ck)
        return [lax.dot_general(k_ref[0, hh, pl.ds(start, n * ck), :], q_ref[0, hh], (((1,), (1,)), ((), ())),
                                preferred_element_type=F32)
                for hh in heads]

    def visible(c0, shape):
        kpos = c0 * ck + lax.broadcasted_iota(jnp.int32, shape, 0)
        qpos = qi * tq + lax.broadcasted_iota(jnp.int32, shape, 1)
        return (kpos <= qpos) | (qpos < PAD_FRONT)

    def weighted_values(c0, n, hh, p):
        pb = p.astype(BF16)
        lo = (hh % 2) * V_HEAD_DIM
        out = None
        for c in range(n):
            vt = vt_ref[0, hh // 2, c0 + c][lo:lo + V_HEAD_DIM, :]
            d = jnp.dot(vt, pb[c * ck:(c + 1) * ck, :], preferred_element_type=F32)
            out = d if out is None else out + d
        return out

    def step_bounded(c0, n, masked):
        scores = span_scores(c0, n)
        for hh in heads:
            p = jnp.exp2(scores[hh])
            if masked:
                p = jnp.where(visible(c0, p.shape), p, 0.0)
            l_ref[hh] += jnp.sum(p.reshape(n * ck // 8, 8, tq), axis=0)
            acc_ref[hh] += weighted_values(c0, n, hh, p)

    def step_general(c0, n, masked):
        scores = span_scores(c0, n)
        for hh in heads:
            s = scores[hh]
            if masked:
                s = jnp.where(visible(c0, s.shape), s, MASK_VALUE)
            m_prev = m_ref[hh]
            m_new = jnp.maximum(m_prev, jnp.max(s, axis=0, keepdims=True))
            alpha = jnp.exp2(m_prev - m_new)
            p = jnp.exp2(s - m_new)
            l_ref[hh, 0:1, :] = alpha * l_ref[hh, 0:1, :] + jnp.sum(p, axis=0, keepdims=True)
            acc_ref[hh] = alpha * acc_ref[hh] + weighted_values(c0, n, hh, p)
            m_ref[hh] = m_new

    def run(step, groups):
        m_ref[...] = jnp.full_like(m_ref, -jnp.inf)
        l_ref[...] = jnp.zeros_like(l_ref)
        acc_ref[...] = jnp.zeros_like(acc_ref)

        first, left = 0, qi
        for group in groups:
            def body(g, carry, first=first, group=group):
                step(first + g * group, group, False)
                return carry

            trips = left // group
            lax.fori_loop(0, trips, body, 0)
            first, left = first + trips * group, left - trips * group
        for r in range(groups[-1]):
            @pl.when(left == r)
            def _(r=r):
                step(qi - r, r + 1, True)
                out_t = jnp.concatenate([acc_ref[hh] / jnp.sum(l_ref[hh], axis=0, keepdims=True) for hh in heads],
                                        axis=0)
                o_ref[...] = out_t.T

    bounded = bounded_ref[layer] != 0

    @pl.when(bounded)
    def _():
        run(step_bounded, KV_GROUPS)

    @pl.when(jnp.logical_not(bounded))
    def _():
        run(step_general, KV_GROUPS[-1:])

    om = (o_ref[...] * sz_ref[0].astype(F32)).astype(BF16)
    y_mla = jnp.dot(om, wmu_ref[...], preferred_element_type=F32)
    merged = (pc_ref[0].astype(F32) + sg_ref[0].astype(F32) * y_mla).astype(BF16)
    out_ref[0] = h_ref[0] + jnp.dot(merged, wout_ref[...], preferred_element_type=F32)


def _const_spec(shape):
    return pl.BlockSpec(shape, lambda *_: (0,) * len(shape))


def _layer_spec(stacked, layer):
    rest = stacked.shape[1:]
    return pl.BlockSpec((None,) + rest, lambda *_: (layer,) + (0,) * len(rest))


def _row_spec(t, width):
    return pl.BlockSpec((1, t, width), lambda b, i: (b, i, 0))


def _rope_tables(pos_pad):
    b, lp = pos_pad.shape
    t = ROW_TILE
    half = QK_ROPE_DIM // 2
    inv_freq = ROPE_THETA ** (-jnp.arange(half, dtype=F32) / half)
    freq = _to_head_lanes(jnp.concatenate([jnp.zeros((QK_NOPE_DIM,), F32), inv_freq, inv_freq]))[None, :]
    out = jax.ShapeDtypeStruct((b, lp, LANES), F32)
    return pl.pallas_call(
        _rope_table_kernel,
        grid=(b, lp // t),
        in_specs=[_row_spec(t, 1), _const_spec((1, LANES))],
        out_specs=[_row_spec(t, LANES)] * 2,
        out_shape=[out] * 2,
        name="rope_tables",
    )(pos_pad[:, :, None], freq)


def _pre(h, tabs, wts, layer, meta=None):
    first_layer = meta is not None
    b = h.shape[0]
    lp = tabs[0].shape[1]
    t = ROW_TILE
    nch = t // ATT_TILE
    if first_layer:
        assert PAD_FRONT + N_META == ATT_TILE and PAD_BACK == 0
        token_block = lambda j: pl.BlockSpec((1, ATT_TILE, D_MODEL),
                                             lambda bb, i: (bb, jnp.maximum(nch * i - 1 + j, 0), 0))
        src = [meta] + [h] * nch
        src_specs = [_const_spec(meta.shape)] + [token_block(j) for j in range(nch)]
        extra_specs, extra_shapes = [_row_spec(t, D_MODEL)], [jax.ShapeDtypeStruct((b, lp, D_MODEL), F32)]
    else:
        src, src_specs, extra_specs, extra_shapes = [h], [_row_spec(t, D_MODEL)], [], []
    k_shape = jax.ShapeDtypeStruct((b, N_HEADS, lp, HEAD_PAD), BF16)
    k_spec = pl.BlockSpec((1, N_HEADS, t, HEAD_PAD), lambda bb, i: (bb, 0, i, 0))
    vt_shape = jax.ShapeDtypeStruct((b, N_HEADS // 2, lp // ATT_TILE, 2 * V_HEAD_DIM, ATT_TILE), BF16)
    vt_spec = pl.BlockSpec((1, N_HEADS // 2, nch, 2 * V_HEAD_DIM, ATT_TILE), lambda bb, i: (bb, 0, i, 0, 0))
    weights = [wts[name] for name in ("ng", "win", "wg", "ps", "wpu", "gqa", "gkva", "wqb", "wkvb", "gqn", "gkn")]
    return pl.pallas_call(
        functools.partial(_pre_kernel, first_layer=first_layer),
        grid=(b, lp // t),
        in_specs=(src_specs + [_row_spec(t, LANES)] * 2 + [_layer_spec(w, layer) for w in weights]
                  + [_const_spec(wts["ones"].shape)]),
        out_specs=[k_spec, k_spec, vt_spec, _row_spec(t, MLA_WIDTH), _row_spec(t, D_MODEL),
                   _row_spec(t, D_MODEL)] + extra_specs,
        out_shape=[k_shape, k_shape, vt_shape,
                   jax.ShapeDtypeStruct((b, lp, MLA_WIDTH), BF16),
                   jax.ShapeDtypeStruct((b, lp, D_MODEL), BF16),
                   jax.ShapeDtypeStruct((b, lp, D_MODEL), BF16)] + extra_shapes,
        scratch_shapes=[pltpu.VMEM((HALO, POOL_WIDTH), F32)],
        compiler_params=pltpu.CompilerParams(dimension_semantics=("parallel", "arbitrary"),
                                             vmem_limit_bytes=VMEM_LIMIT),
        name="pre",
    )(*src, *tabs, *weights, wts["ones"])


def _scores_bounded(q_gain, k_gain):
    limit = (QK_HEAD_DIM * jnp.max(jnp.abs(q_gain), axis=-1) * jnp.max(jnp.abs(k_gain), axis=-1)
             * (math.log2(math.e) / math.sqrt(QK_HEAD_DIM)) * 1.02)
    return (limit <= SCORE_BOUND).astype(jnp.int32)


def _attention(bounded, layer, qt, k, vt, h, sz, sg, pc, wts, final):
    b, nh, lp, _ = k.shape
    t = ATT_TILE
    assert PAD_FRONT + N_META == t
    if final:
        out_spec = pl.BlockSpec((1, t, D_MODEL), lambda bb, i, f: (bb, jnp.maximum(i - 1, 0), 0))
        out_shape = jax.ShapeDtypeStruct((b, lp - t - PAD_BACK, D_MODEL), F32)
        aliases = {}
    else:
        out_spec = pl.BlockSpec((1, t, D_MODEL), lambda bb, i, f: (bb, i, 0))
        out_shape = jax.ShapeDtypeStruct(h.shape, F32)
        aliases = {4: 0}
    single = pl.Buffered(1)
    rows = lambda width: pl.BlockSpec((1, t, width), lambda bb, i, f: (bb, i, 0))
    weight = lambda w: pl.BlockSpec((None,) + w.shape[1:], lambda bb, i, f: (layer, 0, 0), pipeline_mode=single)
    grid_spec = pltpu.PrefetchScalarGridSpec(
        num_scalar_prefetch=1,
        grid=(b, lp // t),
        in_specs=[pl.BlockSpec((1, nh, t, HEAD_PAD), lambda bb, i, f: (bb, 0, i, 0)),
                  pl.BlockSpec((1, nh, t, HEAD_PAD), lambda bb, i, f: (bb, 0, i, 0)),
                  pl.BlockSpec((1, nh // 2, 1, 2 * V_HEAD_DIM, t), lambda bb, i, f: (bb, 0, i, 0, 0)),
                  rows(D_MODEL), rows(MLA_WIDTH), rows(D_MODEL), rows(D_MODEL),
                  weight(wts["wmu"]), weight(wts["wout"])],
        out_specs=out_spec,
        scratch_shapes=[pltpu.VMEM((1, nh, lp, HEAD_PAD), BF16),
                        pltpu.VMEM((1, nh // 2, lp // t, 2 * V_HEAD_DIM, t), BF16),
                        pltpu.VMEM((nh, 1, t), F32),
                        pltpu.VMEM((nh, 8, t), F32),
                        pltpu.VMEM((nh, V_HEAD_DIM, t), F32),
                        pltpu.VMEM((t, nh * V_HEAD_DIM), F32)],
    )
    return pl.pallas_call(
        functools.partial(_attn_kernel, layer=layer),
        grid_spec=grid_spec,
        out_shape=out_shape,
        input_output_aliases=aliases,
        compiler_params=pltpu.CompilerParams(dimension_semantics=("parallel", "arbitrary"),
                                             vmem_limit_bytes=VMEM_LIMIT),
        name="attn",
    )(bounded, qt, k, vt, h, sz, sg, pc, wts["wmu"], wts["wout"])


def _to_head_lanes(a):
    n = a.shape[-1]
    half = QK_ROPE_DIM // 2
    split = LANES // 2 - half
    zeros = lambda w: jnp.zeros(a.shape[:-1] + (w,), a.dtype)
    rot1 = a[..., QK_NOPE_DIM:QK_NOPE_DIM + half] if n > QK_NOPE_DIM else zeros(half)
    rot2 = a[..., QK_NOPE_DIM + half:] if n > QK_NOPE_DIM else zeros(half)
    assert [HEAD_LANES[d] for d in (0, split, QK_NOPE_DIM, QK_NOPE_DIM + half)] == [half, LANES // 2 + half, 0,
                                                                                     LANES // 2]
    return jnp.concatenate([rot1, a[..., :split], rot2, a[..., split:QK_NOPE_DIM], zeros(HEAD_PAD - QK_HEAD_DIM)],
                           axis=-1)


def _prepare_weights(norm_gain, w_in, pool_w_group, pool_scale, pool_w_up, q_a_norm_gain, kv_a_norm_gain, w_q_b,
                     w_kv_b, q_norm_gain, k_norm_gain, mla_w_up, w_out):
    depth = w_in.shape[0]
    u, z, cq, ckv, kr, zm, gp, gm = jnp.split(w_in.astype(BF16), (512, 1024, 1792, 2048, 2080, 2592, 3616), axis=2)
    kr_block = _to_head_lanes(jnp.concatenate([jnp.zeros(kr.shape[:-1] + (QK_NOPE_DIM,), kr.dtype), kr], axis=-1))
    win = jnp.concatenate([u, z, cq, ckv, zm, gp, gm, kr_block], axis=2)
    wqb = _to_head_lanes(w_q_b.reshape(depth, Q_LORA_RANK, N_HEADS, QK_HEAD_DIM)).reshape(depth, Q_LORA_RANK, -1)
    wkv = w_kv_b.reshape(depth, KV_LORA_RANK, N_HEADS, QK_NOPE_DIM + V_HEAD_DIM)
    wkvb = jnp.concatenate([_to_head_lanes(wkv[..., :QK_NOPE_DIM]).reshape(depth, KV_LORA_RANK, -1),
                            wkv[..., QK_NOPE_DIM:].reshape(depth, KV_LORA_RANK, -1)], axis=2)
    eye = jnp.eye(pool_w_group.shape[1], dtype=pool_w_group.dtype)
    wg = (pool_w_group[:, :, :, None, :] * eye[None, :, None, :, None]).reshape(depth, POOL_WIDTH, POOL_WIDTH)
    q_scale = math.log2(math.e) / math.sqrt(QK_HEAD_DIM)
    block = jnp.arange(2 * HEAD_PAD) // HEAD_PAD
    row = lambda g: g[:, None, :]
    return {
        "ones": (block[:, None] == block[None, :]).astype(BF16),
        "ng": row(norm_gain),
        "win": win,
        "wg": wg.astype(BF16),
        "ps": row(pool_scale),
        "wpu": pool_w_up.astype(BF16),
        "gqa": row(q_a_norm_gain),
        "gkva": row(kv_a_norm_gain),
        "wqb": wqb.astype(BF16),
        "wkvb": wkvb.astype(BF16),
        "gqn": row(jnp.tile(_to_head_lanes(q_norm_gain * q_scale), (1, 2))),
        "gkn": row(jnp.tile(_to_head_lanes(k_norm_gain), (1, 2))),
        "wmu": mla_w_up.astype(BF16),
        "wout": w_out.astype(BF16),
    }


def kernel(x, positions, meta_tokens, norm_gain, w_in, pool_w_group, pool_scale, pool_w_up, q_a_norm_gain,
           kv_a_norm_gain, w_q_b, w_kv_b, q_norm_gain, k_norm_gain, mla_w_up, w_out):
    b, seq, _ = x.shape
    lp = PAD_FRONT + N_META + seq + PAD_BACK
    assert lp % ROW_TILE == 0 and ROW_TILE % ATT_TILE == 0

    meta_pos = jnp.broadcast_to(jnp.arange(N_META, dtype=jnp.int32)[None], (b, N_META))
    pos = jnp.concatenate([jnp.zeros((b, PAD_FRONT), jnp.int32), meta_pos, positions + N_META,
                           jnp.zeros((b, PAD_BACK), jnp.int32)], axis=1)

    tabs = _rope_tables(pos)

    wts = _prepare_weights(norm_gain, w_in, pool_w_group, pool_scale, pool_w_up, q_a_norm_gain, kv_a_norm_gain,
                           w_q_b, w_kv_b, q_norm_gain, k_norm_gain, mla_w_up, w_out)
    bounded = _scores_bounded(q_norm_gain, k_norm_gain)
    q, k, v, sz, sg, pc, h = _pre(x, tabs, wts, 0, meta=meta_tokens.astype(x.dtype))
    for l in range(DEPTH):
        if l > 0:
            q, k, v, sz, sg, pc = _pre(h, tabs, wts, l)
        h = _attention(bounded, l, q, k, v, h, sz, sg, pc, wts, final=(l == DEPTH - 1))

    return h
```

```python
import functools
import math

import jax
import jax.numpy as jnp
from jax import lax
from jax.experimental import pallas as pl
from jax.experimental.pallas import tpu as pltpu

F32 = jnp.float32
BF16 = jnp.bfloat16

D_MODEL = 1024
DEPTH = 4
N_META = 16
POOL_WIDTH = 512
POOL_WINDOWS = (2, 4, 8, 16)
N_HEADS = 8
QK_NOPE_DIM = 64
QK_ROPE_DIM = 32
QK_HEAD_DIM = 96
V_HEAD_DIM = 64
MLA_WIDTH = 512
KV_LORA_RANK = 256
Q_LORA_RANK = 768
ROPE_THETA = 10000.0
NORM_EPS = 1e-6
MASK_VALUE = -1e30
PAD_KEY_SCORE = MASK_VALUE

LANES = 128
HEAD_PAD = LANES
PAD_FRONT = 240
PAD_BACK = 0
ROW_TILE = 768
ATT_TILE = 256
KV_GROUPS = (8, 4, 2)
SCORE_BOUND = 32.0
HALO = 16
VMEM_LIMIT = 56 * 1024 * 1024

C_U, C_Z, C_Q, C_KV, C_ZM, C_GP, C_GM, C_KR, C_END = 0, 512, 1024, 1792, 2048, 2560, 3584, 4608, 4736


def _rms(x, gain, n):
    inv = lax.rsqrt(jnp.sum(x * x, axis=-1, keepdims=True) * (1.0 / n) + NORM_EPS)
    return x * inv * gain


def _head_lane(d):
    half = QK_ROPE_DIM // 2
    if d >= QK_NOPE_DIM:
        r = d - QK_NOPE_DIM
        return r if r < half else LANES // 2 + (r - half)
    return half + d if d < LANES // 2 - half else LANES // 2 + half + (d - (LANES // 2 - half))


HEAD_LANES = [_head_lane(d) for d in range(QK_HEAD_DIM)]


def _head_norm_rope(x, ones_ref, gain, c, s):
    sq = x * x
    hi = sq.astype(BF16)
    lo = (sq - hi.astype(F32)).astype(BF16)
    ss = (jnp.dot(hi, ones_ref[...], preferred_element_type=F32)
          + jnp.dot(lo, ones_ref[...], preferred_element_type=F32))
    xn = x * lax.rsqrt(ss * (1.0 / QK_HEAD_DIM) + NORM_EPS) * gain
    out = []
    for j in range(x.shape[1] // LANES):
        blk = xn[:, j * LANES:(j + 1) * LANES]
        out.append(blk * c + pltpu.roll(blk, LANES // 2, axis=1) * s)
    return out


def _rope_table_kernel(pos_ref, freq_ref, c_ref, s_ref):
    ang = pos_ref[0].astype(F32) * freq_ref[...]
    lane = lax.broadcasted_iota(jnp.int32, ang.shape, 1)
    c_ref[0] = jnp.cos(ang)
    s_ref[0] = jnp.where(lane < LANES // 2, -jnp.sin(ang), jnp.sin(ang))


def _pre_kernel(*refs, first_layer):
    n_src = (1 + ROW_TILE // ATT_TILE) if first_layer else 1
    src, refs = refs[:n_src], refs[n_src:]
    (c_ref, s_ref, ng_ref, win_ref, wg_ref, ps_ref, wpu_ref, gqa_ref, gkva_ref, wqb_ref, wkvb_ref, gqn_ref, gkn_ref,
     ones_ref, q_ref, k_ref, v_ref, sz_ref, sg_ref, pc_ref, *rest) = refs
    halo_ref = rest[-1]
    i = pl.program_id(1)
    if first_layer:
        meta_ref, blocks = src[0], src[1:]
        lead = jnp.concatenate([jnp.zeros((PAD_FRONT, D_MODEL), F32), meta_ref[...]], axis=0)
        x = jnp.concatenate([jnp.where(i == 0, lead, blocks[0][0])] + [blk[0] for blk in blocks[1:]], axis=0)
        rest[0][0] = x
    else:
        x = src[0][0]
    t = x.shape[0]
    hn = _rms(x, ng_ref[...], D_MODEL).astype(BF16)

    def proj(lo, hi):
        return jnp.dot(hn, win_ref[:, lo:hi], preferred_element_type=F32)

    c, s = c_ref[0], s_ref[0]
    cqn = _rms(proj(C_Q, C_KV), gqa_ref[...], Q_LORA_RANK).astype(BF16)
    qp = jnp.dot(cqn, wqb_ref[...], preferred_element_type=F32)
    ckvn = _rms(proj(C_KV, C_ZM), gkva_ref[...], KV_LORA_RANK).astype(BF16)
    kv = jnp.dot(ckvn, wkvb_ref[...], preferred_element_type=F32)
    krp = proj(C_KR, C_END)
    krp2 = jnp.concatenate([krp, krp], axis=1)
    row = i * t + lax.broadcasted_iota(jnp.int32, (t, 1), 0)
    sentinel = lax.broadcasted_iota(jnp.int32, (1, HEAD_PAD), 1) == QK_HEAD_DIM
    q_mark = jnp.where(sentinel, 1.0, 0.0)
    k_mark = jnp.where(sentinel & (row < PAD_FRONT), PAD_KEY_SCORE, 0.0)
    width = 2 * HEAD_PAD
    for pair in range(N_HEADS // 2):
        lo = pair * width
        qs = _head_norm_rope(qp[:, lo:lo + width], ones_ref, gqn_ref[...], c, s)
        ks = _head_norm_rope(kv[:, lo:lo + width] + krp2, ones_ref, gkn_ref[...], c, s)
        for odd in range(2):
            q_ref[0, 2 * pair + odd] = (qs[odd] + q_mark).astype(BF16)
            k_ref[0, 2 * pair + odd] = (ks[odd] + k_mark).astype(BF16)
        v0 = N_HEADS * HEAD_PAD + pair * LANES
        vbt = kv[:, v0:v0 + LANES].T.astype(BF16)
        for ch in range(t // ATT_TILE):
            v_ref[0, pair, ch] = vbt[:, ch * ATT_TILE:(ch + 1) * ATT_TILE]

    @pl.when(i == 0)
    def _():
        halo_ref[...] = jnp.zeros_like(halo_ref)

    u = proj(C_U, C_Z)
    uext = jnp.concatenate([halo_ref[...], u], axis=0)
    halo_ref[...] = u[t - HALO:, :]
    avail = jnp.maximum(row - (PAD_FRONT - 1), 1).astype(F32)
    mixed = []
    for g, w in enumerate(POOL_WINDOWS):
        a = uext[:, g * LANES:(g + 1) * LANES]
        ug = a[HALO:, :]
        step = 1
        while step < w:
            a = a + pltpu.roll(a, step, axis=0)
            step *= 2
        inv_cnt = 1.0 / jnp.minimum(avail, float(w))
        mixed.append((a[HALO:, :] * inv_cnt - ug).astype(BF16))
    ymix = jnp.dot(jnp.concatenate(mixed, axis=1), wg_ref[...], preferred_element_type=F32) * ps_ref[...]
    zp = proj(C_Z, C_Q)
    pooled = (ymix * (zp * jax.nn.sigmoid(zp))).astype(BF16)
    y_pool = jnp.dot(pooled, wpu_ref[...], preferred_element_type=F32)
    pc_ref[0] = (jax.nn.sigmoid(proj(C_GP, C_GM)) * y_pool).astype(pc_ref.dtype)

    zm = proj(C_ZM, C_GP)
    sz_ref[0] = (zm * jax.nn.sigmoid(zm)).astype(sz_ref.dtype)
    sg_ref[0] = jax.nn.sigmoid(proj(C_GM, C_KR)).astype(sg_ref.dtype)


def _attn_kernel(bounded_ref, q_ref, knew_ref, vtnew_ref, h_ref, sz_ref, sg_ref, pc_ref, wmu_ref, wout_ref, out_ref,
                 k_ref, vt_ref, m_ref, l_ref, acc_ref, o_ref, *, layer):
    qi = pl.program_id(1)
    n_tiles = pl.num_programs(1) - 1
    tq = q_ref.shape[2]
    ck = vt_ref.shape[4]
    heads = range(q_ref.shape[1])

    def output_stage():
        om = (o_ref[...] * sz_ref[0].astype(F32)).astype(BF16)
        y_mla = jnp.dot(om, wmu_ref[...], preferred_element_type=F32)
        merged = (pc_ref[0].astype(F32) + sg_ref[0].astype(F32) * y_mla).astype(BF16)
        out_ref[0] = h_ref[0] + jnp.dot(merged, wout_ref[...], preferred_element_type=F32)

    def span_scores(c0, n):
        start = pl.multiple_of(c0 * ck, ck)
        return [lax.dot_general(k_ref[0, hh, pl.ds(start, n * ck), :], q_ref[0, hh], (((1,), (1,)), ((), ())),
                                preferred_element_type=F32)
                for hh in heads]

    def visible(c0, shape):
        kpos = c0 * ck + lax.broadcasted_iota(jnp.int32, shape, 0)
        qpos = qi * tq + lax.broadcasted_iota(jnp.int32, shape, 1)
        return (kpos <= qpos) | (qpos < PAD_FRONT)

    def weighted_values(c0, n, hh, p):
        pb = p.astype(BF16)
        lo = (hh % 2) * V_HEAD_DIM
        out = None
        for c in range(n):
            vt = vt_ref[0, hh // 2, c0 + c][lo:lo + V_HEAD_DIM, :]
            d = jnp.dot(vt, pb[c * ck:(c + 1) * ck, :], preferred_element_type=F32)
            out = d if out is None else out + d
        return out

    def step_bounded(c0, n, masked):
        scores = span_scores(c0, n)
        for hh in heads:
            p = jnp.exp2(scores[hh])
            if masked:
                p = jnp.where(visible(c0, p.shape), p, 0.0)
            l_ref[hh] += jnp.sum(p.reshape(n * ck // 8, 8, tq), axis=0)
            acc_ref[hh] += weighted_values(c0, n, hh, p)

    def step_general(c0, n, masked):
        scores = span_scores(c0, n)
        for hh in heads:
            s = scores[hh]
            if masked:
                s = jnp.where(visible(c0, s.shape), s, MASK_VALUE)
            m_prev = m_ref[hh]
            m_new = jnp.maximum(m_prev, jnp.max(s, axis=0, keepdims=True))
            alpha = jnp.exp2(m_prev - m_new)
            p = jnp.exp2(s - m_new)
            l_ref[hh, 0:1, :] = alpha * l_ref[hh, 0:1, :] + jnp.sum(p, axis=0, keepdims=True)
            acc_ref[hh] = alpha * acc_ref[hh] + weighted_values(c0, n, hh, p)
            m_ref[hh] = m_new

    def run(step, groups):
        m_ref[...] = jnp.full_like(m_ref, -jnp.inf)
        l_ref[...] = jnp.zeros_like(l_ref)
        acc_ref[...] = jnp.zeros_like(acc_ref)

        first, left = 0, qi
        for group in groups:
            def body(g, carry, first=first, group=group):
                step(first + g * group, group, False)
                return carry

            trips = left // group
            lax.fori_loop(0, trips, body, 0)
            first, left = first + trips * group, left - trips * group
        for r in range(groups[-1]):
            @pl.when(left == r)
            def _(r=r):
                output_stage()
                step(qi - r, r + 1, True)
                out_t = jnp.concatenate([acc_ref[hh] / jnp.sum(l_ref[hh], axis=0, keepdims=True) for hh in heads],
                                        axis=0)
                o_ref[...] = out_t.T

    @pl.when(qi == 0)
    def _():
        o_ref[...] = jnp.zeros_like(o_ref)

    @pl.when(qi == n_tiles)
    def _():
        output_stage()

    @pl.when(qi < n_tiles)
    def _():
        k_ref[0, :, pl.ds(pl.multiple_of(qi * ck, ck), ck), :] = knew_ref[0]
        vt_ref[0, :, qi] = vtnew_ref[0, :, 0]
        bounded = bounded_ref[layer] != 0

        @pl.when(bounded)
        def _():
            run(step_bounded, KV_GROUPS)

        @pl.when(jnp.logical_not(bounded))
        def _():
            run(step_general, KV_GROUPS[-1:])


def _const_spec(shape):
    return pl.BlockSpec(shape, lambda *_: (0,) * len(shape))


def _layer_spec(stacked, layer):
    rest = stacked.shape[1:]
    return pl.BlockSpec((None,) + rest, lambda *_: (layer,) + (0,) * len(rest))


def _row_spec(t, width):
    return pl.BlockSpec((1, t, width), lambda b, i: (b, i, 0))


def _rope_tables(pos_pad):
    b, lp = pos_pad.shape
    t = ROW_TILE
    half = QK_ROPE_DIM // 2
    inv_freq = ROPE_THETA ** (-jnp.arange(half, dtype=F32) / half)
    freq = _to_head_lanes(jnp.concatenate([jnp.zeros((QK_NOPE_DIM,), F32), inv_freq, inv_freq]))[None, :]
    out = jax.ShapeDtypeStruct((b, lp, LANES), F32)
    return pl.pallas_call(
        _rope_table_kernel,
        grid=(b, lp // t),
        in_specs=[_row_spec(t, 1), _const_spec((1, LANES))],
        out_specs=[_row_spec(t, LANES)] * 2,
        out_shape=[out] * 2,
        name="rope_tables",
    )(pos_pad[:, :, None], freq)


def _pre(h, tabs, wts, layer, meta=None):
    first_layer = meta is not None
    b = h.shape[0]
    lp = tabs[0].shape[1]
    t = ROW_TILE
    nch = t // ATT_TILE
    if first_layer:
        assert PAD_FRONT + N_META == ATT_TILE and PAD_BACK == 0
        token_block = lambda j: pl.BlockSpec((1, ATT_TILE, D_MODEL),
                                             lambda bb, i: (bb, jnp.maximum(nch * i - 1 + j, 0), 0))
        src = [meta] + [h] * nch
        src_specs = [_const_spec(meta.shape)] + [token_block(j) for j in range(nch)]
        extra_specs, extra_shapes = [_row_spec(t, D_MODEL)], [jax.ShapeDtypeStruct((b, lp, D_MODEL), F32)]
    else:
        src, src_specs, extra_specs, extra_shapes = [h], [_row_spec(t, D_MODEL)], [], []
    k_shape = jax.ShapeDtypeStruct((b, N_HEADS, lp, HEAD_PAD), BF16)
    k_spec = pl.BlockSpec((1, N_HEADS, t, HEAD_PAD), lambda bb, i: (bb, 0, i, 0))
    vt_shape = jax.ShapeDtypeStruct((b, N_HEADS // 2, lp // ATT_TILE, 2 * V_HEAD_DIM, ATT_TILE), BF16)
    vt_spec = pl.BlockSpec((1, N_HEADS // 2, nch, 2 * V_HEAD_DIM, ATT_TILE), lambda bb, i: (bb, 0, i, 0, 0))
    weights = [wts[name] for name in ("ng", "win", "wg", "ps", "wpu", "gqa", "gkva", "wqb", "wkvb", "gqn", "gkn")]
    return pl.pallas_call(
        functools.partial(_pre_kernel, first_layer=first_layer),
        grid=(b, lp // t),
        in_specs=(src_specs + [_row_spec(t, LANES)] * 2 + [_layer_spec(w, layer) for w in weights]
                  + [_const_spec(wts["ones"].shape)]),
        out_specs=[k_spec, k_spec, vt_spec, _row_spec(t, MLA_WIDTH), _row_spec(t, D_MODEL),
                   _row_spec(t, D_MODEL)] + extra_specs,
        out_shape=[k_shape, k_shape, vt_shape,
                   jax.ShapeDtypeStruct((b, lp, MLA_WIDTH), BF16),
                   jax.ShapeDtypeStruct((b, lp, D_MODEL), BF16),
                   jax.ShapeDtypeStruct((b, lp, D_MODEL), BF16)] + extra_shapes,
        scratch_shapes=[pltpu.VMEM((HALO, POOL_WIDTH), F32)],
        compiler_params=pltpu.CompilerParams(dimension_semantics=("parallel", "arbitrary"),
                                             vmem_limit_bytes=VMEM_LIMIT),
        name="pre",
    )(*src, *tabs, *weights, wts["ones"])


def _scores_bounded(q_gain, k_gain):
    limit = (QK_HEAD_DIM * jnp.max(jnp.abs(q_gain), axis=-1) * jnp.max(jnp.abs(k_gain), axis=-1)
             * (math.log2(math.e) / math.sqrt(QK_HEAD_DIM)) * 1.02)
    return (limit <= SCORE_BOUND).astype(jnp.int32)


def _attention(bounded, layer, qt, k, vt, h, sz, sg, pc, wts, final):
    b, nh, lp, _ = k.shape
    t = ATT_TILE
    n_tiles = lp // t
    assert PAD_FRONT + N_META == t
    tile = lambda i: jnp.minimum(i, n_tiles - 1)
    prev = lambda i: jnp.maximum(i - 1, 0)
    if final:
        out_spec = pl.BlockSpec((1, t, D_MODEL), lambda bb, i, f: (bb, jnp.maximum(i - 2, 0), 0))
        out_shape = jax.ShapeDtypeStruct((b, lp - t - PAD_BACK, D_MODEL), F32)
        aliases = {}
    else:
        out_spec = pl.BlockSpec((1, t, D_MODEL), lambda bb, i, f: (bb, prev(i), 0))
        out_shape = jax.ShapeDtypeStruct(h.shape, F32)
        aliases = {4: 0}
    single = pl.Buffered(1)
    rows = lambda width: pl.BlockSpec((1, t, width), lambda bb, i, f: (bb, prev(i), 0))
    weight = lambda w: pl.BlockSpec((None,) + w.shape[1:], lambda bb, i, f: (layer, 0, 0), pipeline_mode=single)
    grid_spec = pltpu.PrefetchScalarGridSpec(
        num_scalar_prefetch=1,
        grid=(b, n_tiles + 1),
        in_specs=[pl.BlockSpec((1, nh, t, HEAD_PAD), lambda bb, i, f: (bb, 0, tile(i), 0)),
                  pl.BlockSpec((1, nh, t, HEAD_PAD), lambda bb, i, f: (bb, 0, tile(i), 0)),
                  pl.BlockSpec((1, nh // 2, 1, 2 * V_HEAD_DIM, t), lambda bb, i, f: (bb, 0, tile(i), 0, 0)),
                  rows(D_MODEL), rows(MLA_WIDTH), rows(D_MODEL), rows(D_MODEL),
                  weight(wts["wmu"]), weight(wts["wout"])],
        out_specs=out_spec,
        scratch_shapes=[pltpu.VMEM((1, nh, lp, HEAD_PAD), BF16),
                        pltpu.VMEM((1, nh // 2, lp // t, 2 * V_HEAD_DIM, t), BF16),
                        pltpu.VMEM((nh, 1, t), F32),
                        pltpu.VMEM((nh, 8, t), F32),
                        pltpu.VMEM((nh, V_HEAD_DIM, t), F32),
                        pltpu.VMEM((t, nh * V_HEAD_DIM), F32)],
    )
    return pl.pallas_call(
        functools.partial(_attn_kernel, layer=layer),
        grid_spec=grid_spec,
        out_shape=out_shape,
        input_output_aliases=aliases,
        compiler_params=pltpu.CompilerParams(dimension_semantics=("parallel", "arbitrary"),
                                             vmem_limit_bytes=VMEM_LIMIT),
        name="attn",
    )(bounded, qt, k, vt, h, sz, sg, pc, wts["wmu"], wts["wout"])


def _to_head_lanes(a):
    n = a.shape[-1]
    half = QK_ROPE_DIM // 2
    split = LANES // 2 - half
    zeros = lambda w: jnp.zeros(a.shape[:-1] + (w,), a.dtype)
    rot1 = a[..., QK_NOPE_DIM:QK_NOPE_DIM + half] if n > QK_NOPE_DIM else zeros(half)
    rot2 = a[..., QK_NOPE_DIM + half:] if n > QK_NOPE_DIM else zeros(half)
    assert [HEAD_LANES[d] for d in (0, split, QK_NOPE_DIM, QK_NOPE_DIM + half)] == [half, LANES // 2 + half, 0,
                                                                                     LANES // 2]
    return jnp.concatenate([rot1, a[..., :split], rot2, a[..., split:QK_NOPE_DIM], zeros(HEAD_PAD - QK_HEAD_DIM)],
                           axis=-1)


def _prepare_weights(norm_gain, w_in, pool_w_group, pool_scale, pool_w_up, q_a_norm_gain, kv_a_norm_gain, w_q_b,
                     w_kv_b, q_norm_gain, k_norm_gain, mla_w_up, w_out):
    depth = w_in.shape[0]
    u, z, cq, ckv, kr, zm, gp, gm = jnp.split(w_in.astype(BF16), (512, 1024, 1792, 2048, 2080, 2592, 3616), axis=2)
    kr_block = _to_head_lanes(jnp.concatenate([jnp.zeros(kr.shape[:-1] + (QK_NOPE_DIM,), kr.dtype), kr], axis=-1))
    win = jnp.concatenate([u, z, cq, ckv, zm, gp, gm, kr_block], axis=2)
    wqb = _to_head_lanes(w_q_b.reshape(depth, Q_LORA_RANK, N_HEADS, QK_HEAD_DIM)).reshape(depth, Q_LORA_RANK, -1)
    wkv = w_kv_b.reshape(depth, KV_LORA_RANK, N_HEADS, QK_NOPE_DIM + V_HEAD_DIM)
    wkvb = jnp.concatenate([_to_head_lanes(wkv[..., :QK_NOPE_DIM]).reshape(depth, KV_LORA_RANK, -1),
                            wkv[..., QK_NOPE_DIM:].reshape(depth, KV_LORA_RANK, -1)], axis=2)
    eye = jnp.eye(pool_w_group.shape[1], dtype=pool_w_group.dtype)
    wg = (pool_w_group[:, :, :, None, :] * eye[None, :, None, :, None]).reshape(depth, POOL_WIDTH, POOL_WIDTH)
    q_scale = math.log2(math.e) / math.sqrt(QK_HEAD_DIM)
    block = jnp.arange(2 * HEAD_PAD) // HEAD_PAD
    row = lambda g: g[:, None, :]
    return {
        "ones": (block[:, None] == block[None, :]).astype(BF16),
        "ng": row(norm_gain),
        "win": win,
        "wg": wg.astype(BF16),
        "ps": row(pool_scale),
        "wpu": pool_w_up.astype(BF16),
        "gqa": row(q_a_norm_gain),
        "gkva": row(kv_a_norm_gain),
        "wqb": wqb.astype(BF16),
        "wkvb": wkvb.astype(BF16),
        "gqn": row(jnp.tile(_to_head_lanes(q_norm_gain * q_scale), (1, 2))),
        "gkn": row(jnp.tile(_to_head_lanes(k_norm_gain), (1, 2))),
        "wmu": mla_w_up.astype(BF16),
        "wout": w_out.astype(BF16),
    }


def kernel(x, positions, meta_tokens, norm_gain, w_in, pool_w_group, pool_scale, pool_w_up, q_a_norm_gain,
           kv_a_norm_gain, w_q_b, w_kv_b, q_norm_gain, k_norm_gain, mla_w_up, w_out):
    b, seq, _ = x.shape
    lp = PAD_FRONT + N_META + seq + PAD_BACK
    assert lp % ROW_TILE == 0 and ROW_TILE % ATT_TILE == 0

    meta_pos = jnp.broadcast_to(jnp.arange(N_META, dtype=jnp.int32)[None], (b, N_META))
    pos = jnp.concatenate([jnp.zeros((b, PAD_FRONT), jnp.int32), meta_pos, positions + N_META,
                           jnp.zeros((b, PAD_BACK), jnp.int32)], axis=1)

    tabs = _rope_tables(pos)

    wts = _prepare_weights(norm_gain, w_in, pool_w_group, pool_scale, pool_w_up, q_a_norm_gain, kv_a_norm_gain,
                           w_q_b, w_kv_b, q_norm_gain, k_norm_gain, mla_w_up, w_out)
    bounded = _scores_bounded(q_norm_gain, k_norm_gain)
    q, k, v, sz, sg, pc, h = _pre(x, tabs, wts, 0, meta=meta_tokens.astype(x.dtype))
    for l in range(DEPTH):
        if l > 0:
            q, k, v, sz, sg, pc = _pre(h, tabs, wts, l)
        h = _attention(bounded, l, q, k, v, h, sz, sg, pc, wts, final=(l == DEPTH - 1))

    return h
```

```python
import functools
import math

import jax
import jax.numpy as jnp
from jax import lax
from jax.experimental import pallas as pl
from jax.experimental.pallas import tpu as pltpu

F32 = jnp.float32
BF16 = jnp.bfloat16

D_MODEL = 1024
DEPTH = 4
N_META = 16
POOL_WIDTH = 512
POOL_WINDOWS = (2, 4, 8, 16)
N_HEADS = 8
QK_NOPE_DIM = 64
QK_ROPE_DIM = 32
QK_HEAD_DIM = 96
V_HEAD_DIM = 64
MLA_WIDTH = 512
KV_LORA_RANK = 256
Q_LORA_RANK = 768
ROPE_THETA = 10000.0
NORM_EPS = 1e-6
MASK_VALUE = -1e30
PAD_KEY_SCORE = MASK_VALUE

LANES = 128
HEAD_PAD = LANES
PAD_FRONT = 240
PAD_BACK = 0
ROW_TILE = 768
ATT_TILE = 256
KV_GROUPS = (8, 4, 2)
SCORE_BOUND = 32.0
HALO = 16
VMEM_LIMIT = 56 * 1024 * 1024

C_U, C_Z, C_Q, C_KV, C_ZM, C_GP, C_GM, C_KR, C_END = 0, 512, 1024, 1792, 2048, 2560, 3584, 4608, 4736


def _rms(x, gain, n):
    inv = lax.rsqrt(jnp.sum(x * x, axis=-1, keepdims=True) * (1.0 / n) + NORM_EPS)
    return x * inv * gain


def _head_lane(d):
    half = QK_ROPE_DIM // 2
    if d >= QK_NOPE_DIM:
        r = d - QK_NOPE_DIM
        return r if r < half else LANES // 2 + (r - half)
    return half + d if d < LANES // 2 - half else LANES // 2 + half + (d - (LANES // 2 - half))


HEAD_LANES = [_head_lane(d) for d in range(QK_HEAD_DIM)]


def _head_norm_rope(x, ones_ref, gain, c, s):
    sq = x * x
    hi = sq.astype(BF16)
    lo = (sq - hi.astype(F32)).astype(BF16)
    ss = (jnp.dot(hi, ones_ref[...], preferred_element_type=F32)
          + jnp.dot(lo, ones_ref[...], preferred_element_type=F32))
    xn = x * lax.rsqrt(ss * (1.0 / QK_HEAD_DIM) + NORM_EPS) * gain
    out = []
    for j in range(x.shape[1] // LANES):
        blk = xn[:, j * LANES:(j + 1) * LANES]
        out.append(blk * c + pltpu.roll(blk, LANES // 2, axis=1) * s)
    return out


def _rope_table_kernel(pos_ref, freq_ref, c_ref, s_ref):
    ang = pos_ref[0].astype(F32) * freq_ref[...]
    lane = lax.broadcasted_iota(jnp.int32, ang.shape, 1)
    c_ref[0] = jnp.cos(ang)
    s_ref[0] = jnp.where(lane < LANES // 2, -jnp.sin(ang), jnp.sin(ang))


def _pre_kernel(*refs, first_layer):
    n_src = (1 + ROW_TILE // ATT_TILE) if first_layer else 1
    src, refs = refs[:n_src], refs[n_src:]
    (c_ref, s_ref, ng_ref, win_ref, wg_ref, ps_ref, wpu_ref, gqa_ref, gkva_ref, wqb_ref, wkvb_ref, gqn_ref, gkn_ref,
     ones_ref, q_ref, k_ref, v_ref, sz_ref, sg_ref, pc_ref, *rest) = refs
    halo_ref = rest[-1]
    i = pl.program_id(1)
    if first_layer:
        meta_ref, blocks = src[0], src[1:]
        lead = jnp.concatenate([jnp.zeros((PAD_FRONT, D_MODEL), F32), meta_ref[...]], axis=0)
        x = jnp.concatenate([jnp.where(i == 0, lead, blocks[0][0])] + [blk[0] for blk in blocks[1:]], axis=0)
        rest[0][0] = x
    else:
        x = src[0][0]
    t = x.shape[0]
    hn = _rms(x, ng_ref[...], D_MODEL).astype(BF16)

    def proj(lo, hi):
        return jnp.dot(hn, win_ref[:, lo:hi], preferred_element_type=F32)

    c, s = c_ref[0], s_ref[0]
    cqn = _rms(proj(C_Q, C_KV), gqa_ref[...], Q_LORA_RANK).astype(BF16)
    qp = jnp.dot(cqn, wqb_ref[...], preferred_element_type=F32)
    ckvn = _rms(proj(C_KV, C_ZM), gkva_ref[...], KV_LORA_RANK).astype(BF16)
    kv = jnp.dot(ckvn, wkvb_ref[...], preferred_element_type=F32)
    krp = proj(C_KR, C_END)
    krp2 = jnp.concatenate([krp, krp], axis=1)
    row = i * t + lax.broadcasted_iota(jnp.int32, (t, 1), 0)
    sentinel = lax.broadcasted_iota(jnp.int32, (1, HEAD_PAD), 1) == QK_HEAD_DIM
    q_mark = jnp.where(sentinel, 1.0, 0.0)
    k_mark = jnp.where(sentinel & (row < PAD_FRONT), PAD_KEY_SCORE, 0.0)
    width = 2 * HEAD_PAD
    for pair in range(N_HEADS // 2):
        lo = pair * width
        qs = _head_norm_rope(qp[:, lo:lo + width], ones_ref, gqn_ref[...], c, s)
        ks = _head_norm_rope(kv[:, lo:lo + width] + krp2, ones_ref, gkn_ref[...], c, s)
        for odd in range(2):
            q_ref[0, 2 * pair + odd] = (qs[odd] + q_mark).astype(BF16)
            k_ref[0, 2 * pair + odd] = (ks[odd] + k_mark).astype(BF16)
        v0 = N_HEADS * HEAD_PAD + pair * LANES
        vbt = kv[:, v0:v0 + LANES].T.astype(BF16)
        for ch in range(t // ATT_TILE):
            v_ref[0, pair, ch] = vbt[:, ch * ATT_TILE:(ch + 1) * ATT_TILE]

    @pl.when(i == 0)
    def _():
        halo_ref[...] = jnp.zeros_like(halo_ref)

    u = proj(C_U, C_Z)
    uext = jnp.concatenate([halo_ref[...], u], axis=0)
    halo_ref[...] = u[t - HALO:, :]
    avail = jnp.maximum(row - (PAD_FRONT - 1), 1).astype(F32)
    mixed = []
    for g, w in enumerate(POOL_WINDOWS):
        a = uext[:, g * LANES:(g + 1) * LANES]
        ug = a[HALO:, :]
        step = 1
        while step < w:
            a = a + pltpu.roll(a, step, axis=0)
            step *= 2
        inv_cnt = 1.0 / jnp.minimum(avail, float(w))
        mixed.append((a[HALO:, :] * inv_cnt - ug).astype(BF16))
    ymix = jnp.dot(jnp.concatenate(mixed, axis=1), wg_ref[...], preferred_element_type=F32) * ps_ref[...]
    zp = proj(C_Z, C_Q)
    pooled = (ymix * (zp * jax.nn.sigmoid(zp))).astype(BF16)
    y_pool = jnp.dot(pooled, wpu_ref[...], preferred_element_type=F32)
    pc_ref[0] = (jax.nn.sigmoid(proj(C_GP, C_GM)) * y_pool).astype(pc_ref.dtype)

    zm = proj(C_ZM, C_GP)
    sz_ref[0] = (zm * jax.nn.sigmoid(zm)).astype(sz_ref.dtype)
    sg_ref[0] = jax.nn.sigmoid(proj(C_GM, C_KR)).astype(sg_ref.dtype)


def _attn_kernel(bounded_ref, q_ref, knew_ref, vtnew_ref, h_ref, sz_ref, sg_ref, pc_ref, wmu_ref, wout_ref, out_ref,
                 k_ref, vt_ref, m_ref, l_ref, acc_ref, o_ref, *, layer):
    qi = pl.program_id(1)
    n_tiles = pl.num_programs(1) - 1
    tq = q_ref.shape[2]
    ck = vt_ref.shape[4]
    heads = range(q_ref.shape[1])

    def output_stage():
        om = (o_ref[...] * sz_ref[0].astype(F32)).astype(BF16)
        y_mla = jnp.dot(om, wmu_ref[...], preferred_element_type=F32)
        merged = (pc_ref[0].astype(F32) + sg_ref[0].astype(F32) * y_mla).astype(BF16)
        out_ref[0] = h_ref[0] + jnp.dot(merged, wout_ref[...], preferred_element_type=F32)

    def span_scores(c0, n):
        start = pl.multiple_of(c0 * ck, ck)
        return [lax.dot_general(k_ref[0, hh, pl.ds(start, n * ck), :], q_ref[0, hh], (((1,), (1,)), ((), ())),
                                preferred_element_type=F32)
                for hh in heads]

    def visible(c0, shape):
        kpos = c0 * ck + lax.broadcasted_iota(jnp.int32, shape, 0)
        qpos = qi * tq + lax.broadcasted_iota(jnp.int32, shape, 1)
        return (kpos <= qpos) | (qpos < PAD_FRONT)

    def weighted_values(c0, n, hh, p):
        pb = p.astype(BF16)
        lo = (hh % 2) * V_HEAD_DIM
        out = None
        for c in range(n):
            vt = vt_ref[0, hh // 2, c0 + c][lo:lo + V_HEAD_DIM, :]
            d = jnp.dot(vt, pb[c * ck:(c + 1) * ck, :], preferred_element_type=F32)
            out = d if out is None else out + d
        return out

    def step_bounded(c0, n, masked):
        scores = span_scores(c0, n)
        for hh in heads:
            p = jnp.exp2(scores[hh])
            if masked:
                p = jnp.where(visible(c0, p.shape), p, 0.0)
            l_ref[hh] += jnp.sum(p.reshape(n * ck // 8, 8, tq), axis=0)
            acc_ref[hh] += weighted_values(c0, n, hh, p)

    def step_general(c0, n, masked):
        scores = span_scores(c0, n)
        for hh in heads:
            s = scores[hh]
            if masked:
                s = jnp.where(visible(c0, s.shape), s, MASK_VALUE)
            m_prev = m_ref[hh]
            m_new = jnp.maximum(m_prev, jnp.max(s, axis=0, keepdims=True))
            alpha = jnp.exp2(m_prev - m_new)
            p = jnp.exp2(s - m_new)
            l_ref[hh, 0:1, :] = alpha * l_ref[hh, 0:1, :] + jnp.sum(p, axis=0, keepdims=True)
            acc_ref[hh] = alpha * acc_ref[hh] + weighted_values(c0, n, hh, p)
            m_ref[hh] = m_new

    def run(step, groups):
        m_ref[...] = jnp.full_like(m_ref, -jnp.inf)
        l_ref[...] = jnp.zeros_like(l_ref)
        acc_ref[...] = jnp.zeros_like(acc_ref)

        first, left = 0, qi
        for group in groups:
            def body(g, carry, first=first, group=group):
                step(first + g * group, group, False)
                return carry

            trips = left // group
            lax.fori_loop(0, trips, body, 0)
            first, left = first + trips * group, left - trips * group
        for r in range(groups[-1]):
            @pl.when(left == r)
            def _(r=r):
                output_stage()
                step(qi - r, r + 1, True)
                out_t = jnp.concatenate([acc_ref[hh] / jnp.sum(l_ref[hh], axis=0, keepdims=True) for hh in heads],
                                        axis=0)
                o_ref[...] = out_t.T

    @pl.when(qi == 0)
    def _():
        o_ref[...] = jnp.zeros_like(o_ref)

    @pl.when(qi == n_tiles)
    def _():
        output_stage()

    @pl.when(qi < n_tiles)
    def _():
        k_ref[0, :, pl.ds(pl.multiple_of(qi * ck, ck), ck), :] = knew_ref[0]
        vt_ref[0, :, qi] = vtnew_ref[0, :, 0]
        bounded = bounded_ref[layer] != 0

        @pl.when(bounded)
        def _():
            run(step_bounded, KV_GROUPS)

        @pl.when(jnp.logical_not(bounded))
        def _():
            run(step_general, KV_GROUPS[-1:])


def _const_spec(shape):
    return pl.BlockSpec(shape, lambda *_: (0,) * len(shape))


def _layer_spec(stacked, layer):
    rest = stacked.shape[1:]
    return pl.BlockSpec((None,) + rest, lambda *_: (layer,) + (0,) * len(rest))


def _row_spec(t, width):
    return pl.BlockSpec((1, t, width), lambda b, i: (b, i, 0))


def _rope_tables(pos_pad):
    b, lp = pos_pad.shape
    t = ROW_TILE
    half = QK_ROPE_DIM // 2
    inv_freq = ROPE_THETA ** (-jnp.arange(half, dtype=F32) / half)
    freq = _to_head_lanes(jnp.concatenate([jnp.zeros((QK_NOPE_DIM,), F32), inv_freq, inv_freq]))[None, :]
    out = jax.ShapeDtypeStruct((b, lp, LANES), F32)
    return pl.pallas_call(
        _rope_table_kernel,
        grid=(b, lp // t),
        in_specs=[_row_spec(t, 1), _const_spec((1, LANES))],
        out_specs=[_row_spec(t, LANES)] * 2,
        out_shape=[out] * 2,
        name="rope_tables",
    )(pos_pad[:, :, None], freq)


def _pre(h, tabs, wts, layer, meta=None):
    first_layer = meta is not None
    b = h.shape[0]
    lp = tabs[0].shape[1]
    t = ROW_TILE
    nch = t // ATT_TILE
    if first_layer:
        assert PAD_FRONT + N_META == ATT_TILE and PAD_BACK == 0
        token_block = lambda j: pl.BlockSpec((1, ATT_TILE, D_MODEL),
                                             lambda bb, i: (bb, jnp.maximum(nch * i - 1 + j, 0), 0))
        src = [meta] + [h] * nch
        src_specs = [_const_spec(meta.shape)] + [token_block(j) for j in range(nch)]
        extra_specs, extra_shapes = [_row_spec(t, D_MODEL)], [jax.ShapeDtypeStruct((b, lp, D_MODEL), F32)]
    else:
        src, src_specs, extra_specs, extra_shapes = [h], [_row_spec(t, D_MODEL)], [], []
    k_shape = jax.ShapeDtypeStruct((b, N_HEADS, lp, HEAD_PAD), BF16)
    k_spec = pl.BlockSpec((1, N_HEADS, t, HEAD_PAD), lambda bb, i: (bb, 0, i, 0))
    vt_shape = jax.ShapeDtypeStruct((b, N_HEADS // 2, lp // ATT_TILE, 2 * V_HEAD_DIM, ATT_TILE), BF16)
    vt_spec = pl.BlockSpec((1, N_HEADS // 2, nch, 2 * V_HEAD_DIM, ATT_TILE), lambda bb, i: (bb, 0, i, 0, 0))
    weights = [wts[name] for name in ("ng", "win", "wg", "ps", "wpu", "gqa", "gkva", "wqb", "wkvb", "gqn", "gkn")]
    return pl.pallas_call(
        functools.partial(_pre_kernel, first_layer=first_layer),
        grid=(b, lp // t),
        in_specs=(src_specs + [_row_spec(t, LANES)] * 2 + [_layer_spec(w, layer) for w in weights]
                  + [_const_spec(wts["ones"].shape)]),
        out_specs=[k_spec, k_spec, vt_spec, _row_spec(t, MLA_WIDTH), _row_spec(t, D_MODEL),
                   _row_spec(t, D_MODEL)] + extra_specs,
        out_shape=[k_shape, k_shape, vt_shape,
                   jax.ShapeDtypeStruct((b, lp, MLA_WIDTH), BF16),
                   jax.ShapeDtypeStruct((b, lp, D_MODEL), BF16),
                   jax.ShapeDtypeStruct((b, lp, D_MODEL), BF16)] + extra_shapes,
        scratch_shapes=[pltpu.VMEM((HALO, POOL_WIDTH), F32)],
        compiler_params=pltpu.CompilerParams(dimension_semantics=("parallel", "arbitrary"),
                                             vmem_limit_bytes=VMEM_LIMIT),
        name="pre",
    )(*src, *tabs, *weights, wts["ones"])


def _scores_bounded(q_gain, k_gain):
    limit = (QK_HEAD_DIM * jnp.max(jnp.abs(q_gain), axis=-1) * jnp.max(jnp.abs(k_gain), axis=-1)
             * (math.log2(math.e) / math.sqrt(QK_HEAD_DIM)) * 1.02)
    return (limit <= SCORE_BOUND).astype(jnp.int32)


def _attention(bounded, layer, qt, k, vt, h, sz, sg, pc, wts, final):
    b, nh, lp, _ = k.shape
    t = ATT_TILE
    n_tiles = lp // t
    assert PAD_FRONT + N_META == t
    tile = lambda i: jnp.minimum(i, n_tiles - 1)
    prev = lambda i: jnp.maximum(i - 1, 0)
    if final:
        out_spec = pl.BlockSpec((1, t, D_MODEL), lambda bb, i, f: (bb, jnp.maximum(i - 2, 0), 0))
        out_shape = jax.ShapeDtypeStruct((b, lp - t - PAD_BACK, D_MODEL), F32)
        aliases = {}
    else:
        out_spec = pl.BlockSpec((1, t, D_MODEL), lambda bb, i, f: (bb, prev(i), 0))
        out_shape = jax.ShapeDtypeStruct(h.shape, F32)
        aliases = {4: 0}
    single = pl.Buffered(1)
    rows = lambda width: pl.BlockSpec((1, t, width), lambda bb, i, f: (bb, prev(i), 0))
    weight = lambda w: pl.BlockSpec((None,) + w.shape[1:], lambda bb, i, f: (layer, 0, 0), pipeline_mode=single)
    grid_spec = pltpu.PrefetchScalarGridSpec(
        num_scalar_prefetch=1,
        grid=(b, n_tiles + 1),
        in_specs=[pl.BlockSpec((1, nh, t, HEAD_PAD), lambda bb, i, f: (bb, 0, tile(i), 0)),
                  pl.BlockSpec((1, nh, t, HEAD_PAD), lambda bb, i, f: (bb, 0, tile(i), 0)),
                  pl.BlockSpec((1, nh // 2, 1, 2 * V_HEAD_DIM, t), lambda bb, i, f: (bb, 0, tile(i), 0, 0)),
                  rows(D_MODEL), rows(MLA_WIDTH), rows(D_MODEL), rows(D_MODEL),
                  weight(wts["wmu"]), weight(wts["wout"])],
        out_specs=out_spec,
        scratch_shapes=[pltpu.VMEM((1, nh, lp, HEAD_PAD), BF16),
                        pltpu.VMEM((1, nh // 2, lp // t, 2 * V_HEAD_DIM, t), BF16),
                        pltpu.VMEM((nh, 1, t), F32),
                        pltpu.VMEM((nh, 8, t), F32),
                        pltpu.VMEM((nh, V_HEAD_DIM, t), F32),
                        pltpu.VMEM((t, nh * V_HEAD_DIM), F32)],
    )
    return pl.pallas_call(
        functools.partial(_attn_kernel, layer=layer),
        grid_spec=grid_spec,
        out_shape=out_shape,
        input_output_aliases=aliases,
        compiler_params=pltpu.CompilerParams(dimension_semantics=("parallel", "arbitrary"),
                                             vmem_limit_bytes=VMEM_LIMIT),
        name="attn",
    )(bounded, qt, k, vt, h, sz, sg, pc, wts["wmu"], wts["wout"])


def _to_head_lanes(a):
    n = a.shape[-1]
    half = QK_ROPE_DIM // 2
    split = LANES // 2 - half
    zeros = lambda w: jnp.zeros(a.shape[:-1] + (w,), a.dtype)
    rot1 = a[..., QK_NOPE_DIM:QK_NOPE_DIM + half] if n > QK_NOPE_DIM else zeros(half)
    rot2 = a[..., QK_NOPE_DIM + half:] if n > QK_NOPE_DIM else zeros(half)
    assert [HEAD_LANES[d] for d in (0, split, QK_NOPE_DIM, QK_NOPE_DIM + half)] == [half, LANES // 2 + half, 0,
                                                                                     LANES // 2]
    return jnp.concatenate([rot1, a[..., :split], rot2, a[..., split:QK_NOPE_DIM], zeros(HEAD_PAD - QK_HEAD_DIM)],
                           axis=-1)


def _prepare_weights(norm_gain, w_in, pool_w_group, pool_scale, pool_w_up, q_a_norm_gain, kv_a_norm_gain, w_q_b,
                     w_kv_b, q_norm_gain, k_norm_gain, mla_w_up, w_out):
    depth = w_in.shape[0]
    half = QK_ROPE_DIM // 2
    kr0 = C_KV + KV_LORA_RANK
    place = lambda a, at: jnp.pad(a, ((0, 0), (0, 0), (at, C_END - at - a.shape[-1])))
    win = (place(w_in[:, :, :kr0], 0) + place(w_in[:, :, kr0 + QK_ROPE_DIM:], C_ZM)
           + place(w_in[:, :, kr0:kr0 + half], C_KR + HEAD_LANES[QK_NOPE_DIM])
           + place(w_in[:, :, kr0 + half:kr0 + QK_ROPE_DIM], C_KR + HEAD_LANES[QK_NOPE_DIM + half])).astype(BF16)
    wqb = _to_head_lanes(w_q_b.reshape(depth, Q_LORA_RANK, N_HEADS, QK_HEAD_DIM)).reshape(depth, Q_LORA_RANK, -1)
    wkv = w_kv_b.reshape(depth, KV_LORA_RANK, N_HEADS, QK_NOPE_DIM + V_HEAD_DIM)
    wkvb = jnp.concatenate([_to_head_lanes(wkv[..., :QK_NOPE_DIM]).reshape(depth, KV_LORA_RANK, -1),
                            wkv[..., QK_NOPE_DIM:].reshape(depth, KV_LORA_RANK, -1)], axis=2)
    eye = jnp.eye(pool_w_group.shape[1], dtype=pool_w_group.dtype)
    wg = (pool_w_group[:, :, :, None, :] * eye[None, :, None, :, None]).reshape(depth, POOL_WIDTH, POOL_WIDTH)
    q_scale = math.log2(math.e) / math.sqrt(QK_HEAD_DIM)
    block = jnp.arange(2 * HEAD_PAD) // HEAD_PAD
    row = lambda g: g[:, None, :]
    return {
        "ones": (block[:, None] == block[None, :]).astype(BF16),
        "ng": row(norm_gain),
        "win": win,
        "wg": wg.astype(BF16),
        "ps": row(pool_scale),
        "wpu": pool_w_up.astype(BF16),
        "gqa": row(q_a_norm_gain),
        "gkva": row(kv_a_norm_gain),
        "wqb": wqb.astype(BF16),
        "wkvb": wkvb.astype(BF16),
        "gqn": row(jnp.tile(_to_head_lanes(q_norm_gain * q_scale), (1, 2))),
        "gkn": row(jnp.tile(_to_head_lanes(k_norm_gain), (1, 2))),
        "wmu": mla_w_up.astype(BF16),
        "wout": w_out.astype(BF16),
    }


def kernel(x, positions, meta_tokens, norm_gain, w_in, pool_w_group, pool_scale, pool_w_up, q_a_norm_gain,
           kv_a_norm_gain, w_q_b, w_kv_b, q_norm_gain, k_norm_gain, mla_w_up, w_out):
    b, seq, _ = x.shape
    lp = PAD_FRONT + N_META + seq + PAD_BACK
    assert lp % ROW_TILE == 0 and ROW_TILE % ATT_TILE == 0

    meta_pos = jnp.broadcast_to(jnp.arange(N_META, dtype=jnp.int32)[None], (b, N_META))
    pos = jnp.concatenate([jnp.zeros((b, PAD_FRONT), jnp.int32), meta_pos, positions + N_META,
                           jnp.zeros((b, PAD_BACK), jnp.int32)], axis=1)

    tabs = _rope_tables(pos)

    wts = _prepare_weights(norm_gain, w_in, pool_w_group, pool_scale, pool_w_up, q_a_norm_gain, kv_a_norm_gain,
                           w_q_b, w_kv_b, q_norm_gain, k_norm_gain, mla_w_up, w_out)
    bounded = _scores_bounded(q_norm_gain, k_norm_gain)
    q, k, v, sz, sg, pc, h = _pre(x, tabs, wts, 0, meta=meta_tokens.astype(x.dtype))
    for l in range(DEPTH):
        if l > 0:
            q, k, v, sz, sg, pc = _pre(h, tabs, wts, l)
        h = _attention(bounded, l, q, k, v, h, sz, sg, pc, wts, final=(l == DEPTH - 1))

    return h
```

```python
import functools
import math

import jax
import jax.numpy as jnp
from jax import lax
from jax.experimental import pallas as pl
from jax.experimental.pallas import tpu as pltpu

F32 = jnp.float32
BF16 = jnp.bfloat16

D_MODEL = 1024
DEPTH = 4
N_META = 16
POOL_WIDTH = 512
POOL_WINDOWS = (2, 4, 8, 16)
N_HEADS = 8
QK_NOPE_DIM = 64
QK_ROPE_DIM = 32
QK_HEAD_DIM = 96
V_HEAD_DIM = 64
MLA_WIDTH = 512
KV_LORA_RANK = 256
Q_LORA_RANK = 768
ROPE_THETA = 10000.0
NORM_EPS = 1e-6
MASK_VALUE = -1e30
PAD_KEY_SCORE = MASK_VALUE

LANES = 128
HEAD_PAD = LANES
PAD_FRONT = 240
PAD_BACK = 0
ROW_TILE = 768
ATT_TILE = 256
KV_GROUPS = (8, 4, 2)
SCORE_BOUND = 32.0
HALO = 16
VMEM_LIMIT = 56 * 1024 * 1024

C_U, C_Z, C_Q, C_KV, C_ZM, C_GP, C_GM, C_KR, C_END = 0, 512, 1024, 1792, 2048, 2560, 3584, 4608, 4736


def _rms(x, gain, n):
    inv = lax.rsqrt(jnp.sum(x * x, axis=-1, keepdims=True) * (1.0 / n) + NORM_EPS)
    return x * inv * gain


def _head_lane(d):
    half = QK_ROPE_DIM // 2
    if d >= QK_NOPE_DIM:
        r = d - QK_NOPE_DIM
        return r if r < half else LANES // 2 + (r - half)
    return half + d if d < LANES // 2 - half else LANES // 2 + half + (d - (LANES // 2 - half))


HEAD_LANES = [_head_lane(d) for d in range(QK_HEAD_DIM)]


def _head_norm_rope(x, ones_ref, gain, c, s):
    sq = x * x
    hi = sq.astype(BF16)
    lo = (sq - hi.astype(F32)).astype(BF16)
    ss = (jnp.dot(hi, ones_ref[...], preferred_element_type=F32)
          + jnp.dot(lo, ones_ref[...], preferred_element_type=F32))
    xn = x * lax.rsqrt(ss * (1.0 / QK_HEAD_DIM) + NORM_EPS) * gain
    out = []
    for j in range(x.shape[1] // LANES):
        blk = xn[:, j * LANES:(j + 1) * LANES]
        out.append(blk * c + pltpu.roll(blk, LANES // 2, axis=1) * s)
    return out


def _rope_table_kernel(pos_ref, freq_ref, c_ref, s_ref):
    ang = pos_ref[0].astype(F32) * freq_ref[...]
    lane = lax.broadcasted_iota(jnp.int32, ang.shape, 1)
    c_ref[0] = jnp.cos(ang)
    s_ref[0] = jnp.where(lane < LANES // 2, -jnp.sin(ang), jnp.sin(ang))


def _pre_kernel(*refs, first_layer):
    n_src = (1 + ROW_TILE // ATT_TILE) if first_layer else 1
    src, refs = refs[:n_src], refs[n_src:]
    (c_ref, s_ref, ng_ref, win_ref, wg_ref, ps_ref, wpu_ref, gqa_ref, gkva_ref, wqb_ref, wkvb_ref, gqn_ref, gkn_ref,
     ones_ref, q_ref, k_ref, v_ref, sz_ref, sg_ref, pc_ref, *rest) = refs
    halo_ref = rest[-1]
    i = pl.program_id(1)
    if first_layer:
        meta_ref, blocks = src[0], src[1:]
        lead = jnp.concatenate([jnp.zeros((PAD_FRONT, D_MODEL), F32), meta_ref[...]], axis=0)
        x = jnp.concatenate([jnp.where(i == 0, lead, blocks[0][0])] + [blk[0] for blk in blocks[1:]], axis=0)
        rest[0][0] = x
    else:
        x = src[0][0]
    t = x.shape[0]
    hn = _rms(x, ng_ref[...], D_MODEL).astype(BF16)

    def proj(lo, hi):
        return jnp.dot(hn, win_ref[:, lo:hi], preferred_element_type=F32)

    c, s = c_ref[0], s_ref[0]
    cqn = _rms(proj(C_Q, C_KV), gqa_ref[...], Q_LORA_RANK).astype(BF16)
    qp = jnp.dot(cqn, wqb_ref[...], preferred_element_type=F32)
    ckvn = _rms(proj(C_KV, C_ZM), gkva_ref[...], KV_LORA_RANK).astype(BF16)
    kv = jnp.dot(ckvn, wkvb_ref[...], preferred_element_type=F32)
    krp = proj(C_KR, C_END)
    krp2 = jnp.concatenate([krp, krp], axis=1)
    row = i * t + lax.broadcasted_iota(jnp.int32, (t, 1), 0)
    sentinel = lax.broadcasted_iota(jnp.int32, (1, HEAD_PAD), 1) == QK_HEAD_DIM
    q_mark = jnp.where(sentinel, 1.0, 0.0)
    k_mark = jnp.where(sentinel & (row < PAD_FRONT), PAD_KEY_SCORE, 0.0)
    width = 2 * HEAD_PAD
    for pair in range(N_HEADS // 2):
        lo = pair * width
        qs = _head_norm_rope(qp[:, lo:lo + width], ones_ref, gqn_ref[...], c, s)
        ks = _head_norm_rope(kv[:, lo:lo + width] + krp2, ones_ref, gkn_ref[...], c, s)
        for odd in range(2):
            q_ref[0, 2 * pair + odd] = (qs[odd] + q_mark).astype(BF16)
            k_ref[0, 2 * pair + odd] = (ks[odd] + k_mark).astype(BF16)
        v0 = N_HEADS * HEAD_PAD + pair * LANES
        vbt = kv[:, v0:v0 + LANES].T.astype(BF16)
        for ch in range(t // ATT_TILE):
            v_ref[0, pair, ch] = vbt[:, ch * ATT_TILE:(ch + 1) * ATT_TILE]

    @pl.when(i == 0)
    def _():
        halo_ref[...] = jnp.zeros_like(halo_ref)

    u = proj(C_U, C_Z)
    uext = jnp.concatenate([halo_ref[...], u], axis=0)
    halo_ref[...] = u[t - HALO:, :]
    avail = jnp.maximum(row - (PAD_FRONT - 1), 1).astype(F32)
    mixed = []
    for g, w in enumerate(POOL_WINDOWS):
        a = uext[:, g * LANES:(g + 1) * LANES]
        ug = a[HALO:, :]
        step = 1
        while step < w:
            a = a + pltpu.roll(a, step, axis=0)
            step *= 2
        inv_cnt = 1.0 / jnp.minimum(avail, float(w))
        mixed.append((a[HALO:, :] * inv_cnt - ug).astype(BF16))
    ymix = jnp.dot(jnp.concatenate(mixed, axis=1), wg_ref[...], preferred_element_type=F32) * ps_ref[...]
    zp = proj(C_Z, C_Q)
    pooled = (ymix * (zp * jax.nn.sigmoid(zp))).astype(BF16)
    y_pool = jnp.dot(pooled, wpu_ref[...], preferred_element_type=F32)
    pc_ref[0] = (jax.nn.sigmoid(proj(C_GP, C_GM)) * y_pool).astype(pc_ref.dtype)

    zm = proj(C_ZM, C_GP)
    sz_ref[0] = (zm * jax.nn.sigmoid(zm)).astype(sz_ref.dtype)
    sg_ref[0] = jax.nn.sigmoid(proj(C_GM, C_KR)).astype(sg_ref.dtype)


def _attn_kernel(bounded_ref, q_ref, knew_ref, vtnew_ref, h_ref, sz_ref, sg_ref, pc_ref, wmu_ref, wout_ref, out_ref,
                 k_ref, vt_ref, m_ref, l_ref, acc_ref, o_ref, *, layer):
    qi = pl.program_id(1)
    n_tiles = pl.num_programs(1) - 1
    tq = q_ref.shape[2]
    ck = vt_ref.shape[4]
    heads = range(q_ref.shape[1])

    def output_stage():
        om = (o_ref[...] * sz_ref[0].astype(F32)).astype(BF16)
        y_mla = jnp.dot(om, wmu_ref[...], preferred_element_type=F32)
        merged = (pc_ref[0].astype(F32) + sg_ref[0].astype(F32) * y_mla).astype(BF16)
        out_ref[0] = h_ref[0] + jnp.dot(merged, wout_ref[...], preferred_element_type=F32)

    def span_scores(c0, n):
        start = pl.multiple_of(c0 * ck, ck)
        return [lax.dot_general(k_ref[0, hh, pl.ds(start, n * ck), :], q_ref[0, hh], (((1,), (1,)), ((), ())),
                                preferred_element_type=F32)
                for hh in heads]

    def visible(c0, shape):
        kpos = c0 * ck + lax.broadcasted_iota(jnp.int32, shape, 0)
        qpos = qi * tq + lax.broadcasted_iota(jnp.int32, shape, 1)
        return (kpos <= qpos) | (qpos < PAD_FRONT)

    def weighted_values(c0, n, hh, p):
        pb = p.astype(BF16)
        lo = (hh % 2) * V_HEAD_DIM
        out = None
        for c in range(n):
            vt = vt_ref[0, hh // 2, c0 + c][lo:lo + V_HEAD_DIM, :]
            d = jnp.dot(vt, pb[c * ck:(c + 1) * ck, :], preferred_element_type=F32)
            out = d if out is None else out + d
        return out

    def step_bounded(c0, n, masked):
        scores = span_scores(c0, n)
        for hh in heads:
            p = jnp.exp2(scores[hh])
            if masked:
                p = jnp.where(visible(c0, p.shape), p, 0.0)
            l_ref[hh] += jnp.sum(p.reshape(n * ck // 8, 8, tq), axis=0)
            acc_ref[hh] += weighted_values(c0, n, hh, p)

    def step_general(c0, n, masked):
        scores = span_scores(c0, n)
        for hh in heads:
            s = scores[hh]
            if masked:
                s = jnp.where(visible(c0, s.shape), s, MASK_VALUE)
            m_prev = m_ref[hh]
            m_new = jnp.maximum(m_prev, jnp.max(s, axis=0, keepdims=True))
            alpha = jnp.exp2(m_prev - m_new)
            p = jnp.exp2(s - m_new)
            l_ref[hh, 0:1, :] = alpha * l_ref[hh, 0:1, :] + jnp.sum(p, axis=0, keepdims=True)
            acc_ref[hh] = alpha * acc_ref[hh] + weighted_values(c0, n, hh, p)
            m_ref[hh] = m_new

    def run(step, groups):
        m_ref[...] = jnp.full_like(m_ref, -jnp.inf)
        l_ref[...] = jnp.zeros_like(l_ref)
        acc_ref[...] = jnp.zeros_like(acc_ref)

        first, left = 0, qi
        for group in groups:
            def body(g, carry, first=first, group=group):
                step(first + g * group, group, False)
                return carry

            trips = left // group
            lax.fori_loop(0, trips, body, 0)
            first, left = first + trips * group, left - trips * group
        for r in range(groups[-1]):
            @pl.when(left == r)
            def _(r=r):
                output_stage()
                step(qi - r, r + 1, True)
                out_t = jnp.concatenate([acc_ref[hh] / jnp.sum(l_ref[hh], axis=0, keepdims=True) for hh in heads],
                                        axis=0)
                o_ref[...] = out_t.T

    @pl.when(qi == 0)
    def _():
        o_ref[...] = jnp.zeros_like(o_ref)

    @pl.when(qi == n_tiles)
    def _():
        output_stage()

    @pl.when(qi < n_tiles)
    def _():
        k_ref[0, :, pl.ds(pl.multiple_of(qi * ck, ck), ck), :] = knew_ref[0]
        vt_ref[0, :, qi] = vtnew_ref[0, :, 0]
        bounded = bounded_ref[layer] != 0

        @pl.when(bounded)
        def _():
            run(step_bounded, KV_GROUPS)

        @pl.when(jnp.logical_not(bounded))
        def _():
            run(step_general, KV_GROUPS[-1:])


def _const_spec(shape):
    return pl.BlockSpec(shape, lambda *_: (0,) * len(shape))


def _layer_spec(stacked, layer):
    rest = stacked.shape[1:]
    return pl.BlockSpec((None,) + rest, lambda *_: (layer,) + (0,) * len(rest))


def _row_spec(t, width):
    return pl.BlockSpec((1, t, width), lambda b, i: (b, i, 0))


def _rope_tables(pos_pad):
    b, lp = pos_pad.shape
    t = ROW_TILE
    half = QK_ROPE_DIM // 2
    inv_freq = ROPE_THETA ** (-jnp.arange(half, dtype=F32) / half)
    freq = _to_head_lanes(jnp.concatenate([jnp.zeros((QK_NOPE_DIM,), F32), inv_freq, inv_freq]))[None, :]
    out = jax.ShapeDtypeStruct((b, lp, LANES), F32)
    return pl.pallas_call(
        _rope_table_kernel,
        grid=(b, lp // t),
        in_specs=[_row_spec(t, 1), _const_spec((1, LANES))],
        out_specs=[_row_spec(t, LANES)] * 2,
        out_shape=[out] * 2,
        name="rope_tables",
    )(pos_pad[:, :, None], freq)


def _pre(h, tabs, wts, layer, meta=None):
    first_layer = meta is not None
    b = h.shape[0]
    lp = tabs[0].shape[1]
    t = ROW_TILE
    nch = t // ATT_TILE
    if first_layer:
        assert PAD_FRONT + N_META == ATT_TILE and PAD_BACK == 0
        token_block = lambda j: pl.BlockSpec((1, ATT_TILE, D_MODEL),
                                             lambda bb, i: (bb, jnp.maximum(nch * i - 1 + j, 0), 0))
        src = [meta] + [h] * nch
        src_specs = [_const_spec(meta.shape)] + [token_block(j) for j in range(nch)]
        extra_specs, extra_shapes = [_row_spec(t, D_MODEL)], [jax.ShapeDtypeStruct((b, lp, D_MODEL), F32)]
    else:
        src, src_specs, extra_specs, extra_shapes = [h], [_row_spec(t, D_MODEL)], [], []
    k_shape = jax.ShapeDtypeStruct((b, N_HEADS, lp, HEAD_PAD), BF16)
    k_spec = pl.BlockSpec((1, N_HEADS, t, HEAD_PAD), lambda bb, i: (bb, 0, i, 0))
    vt_shape = jax.ShapeDtypeStruct((b, N_HEADS // 2, lp // ATT_TILE, 2 * V_HEAD_DIM, ATT_TILE), BF16)
    vt_spec = pl.BlockSpec((1, N_HEADS // 2, nch, 2 * V_HEAD_DIM, ATT_TILE), lambda bb, i: (bb, 0, i, 0, 0))
    weights = [wts[name] for name in ("ng", "win", "wg", "ps", "wpu", "gqa", "gkva", "wqb", "wkvb", "gqn", "gkn")]
    return pl.pallas_call(
        functools.partial(_pre_kernel, first_layer=first_layer),
        grid=(b, lp // t),
        in_specs=(src_specs + [_row_spec(t, LANES)] * 2 + [_layer_spec(w, layer) for w in weights]
                  + [_const_spec(wts["ones"].shape)]),
        out_specs=[k_spec, k_spec, vt_spec, _row_spec(t, MLA_WIDTH), _row_spec(t, D_MODEL),
                   _row_spec(t, D_MODEL)] + extra_specs,
        out_shape=[k_shape, k_shape, vt_shape,
                   jax.ShapeDtypeStruct((b, lp, MLA_WIDTH), BF16),
                   jax.ShapeDtypeStruct((b, lp, D_MODEL), BF16),
                   jax.ShapeDtypeStruct((b, lp, D_MODEL), BF16)] + extra_shapes,
        scratch_shapes=[pltpu.VMEM((HALO, POOL_WIDTH), F32)],
        compiler_params=pltpu.CompilerParams(dimension_semantics=("parallel", "arbitrary"),
                                             vmem_limit_bytes=VMEM_LIMIT),
        name="pre",
    )(*src, *tabs, *weights, wts["ones"])


def _scores_bounded(q_gain, k_gain):
    limit = (QK_HEAD_DIM * jnp.max(jnp.abs(q_gain), axis=-1) * jnp.max(jnp.abs(k_gain), axis=-1)
             * (math.log2(math.e) / math.sqrt(QK_HEAD_DIM)) * 1.02)
    return (limit <= SCORE_BOUND).astype(jnp.int32)


def _attention(bounded, layer, qt, k, vt, h, sz, sg, pc, wts, final):
    b, nh, lp, _ = k.shape
    t = ATT_TILE
    n_tiles = lp // t
    assert PAD_FRONT + N_META == t
    tile = lambda i: jnp.minimum(i, n_tiles - 1)
    prev = lambda i: jnp.maximum(i - 1, 0)
    if final:
        out_spec = pl.BlockSpec((1, t, D_MODEL), lambda bb, i, f: (bb, jnp.maximum(i - 2, 0), 0))
        out_shape = jax.ShapeDtypeStruct((b, lp - t - PAD_BACK, D_MODEL), F32)
        aliases = {}
    else:
        out_spec = pl.BlockSpec((1, t, D_MODEL), lambda bb, i, f: (bb, prev(i), 0))
        out_shape = jax.ShapeDtypeStruct(h.shape, F32)
        aliases = {4: 0}
    single = pl.Buffered(1)
    rows = lambda width: pl.BlockSpec((1, t, width), lambda bb, i, f: (bb, prev(i), 0))
    weight = lambda w: pl.BlockSpec((None,) + w.shape[1:], lambda bb, i, f: (layer, 0, 0), pipeline_mode=single)
    grid_spec = pltpu.PrefetchScalarGridSpec(
        num_scalar_prefetch=1,
        grid=(b, n_tiles + 1),
        in_specs=[pl.BlockSpec((1, nh, t, HEAD_PAD), lambda bb, i, f: (bb, 0, tile(i), 0)),
                  pl.BlockSpec((1, nh, t, HEAD_PAD), lambda bb, i, f: (bb, 0, tile(i), 0)),
                  pl.BlockSpec((1, nh // 2, 1, 2 * V_HEAD_DIM, t), lambda bb, i, f: (bb, 0, tile(i), 0, 0)),
                  rows(D_MODEL), rows(MLA_WIDTH), rows(D_MODEL), rows(D_MODEL),
                  weight(wts["wmu"]), weight(wts["wout"])],
        out_specs=out_spec,
        scratch_shapes=[pltpu.VMEM((1, nh, lp, HEAD_PAD), BF16),
                        pltpu.VMEM((1, nh // 2, lp // t, 2 * V_HEAD_DIM, t), BF16),
                        pltpu.VMEM((nh, 1, t), F32),
                        pltpu.VMEM((nh, 8, t), F32),
                        pltpu.VMEM((nh, V_HEAD_DIM, t), F32),
                        pltpu.VMEM((t, nh * V_HEAD_DIM), F32)],
    )
    return pl.pallas_call(
        functools.partial(_attn_kernel, layer=layer),
        grid_spec=grid_spec,
        out_shape=out_shape,
        input_output_aliases=aliases,
        compiler_params=pltpu.CompilerParams(dimension_semantics=("parallel", "arbitrary"),
                                             vmem_limit_bytes=VMEM_LIMIT),
        name="attn",
    )(bounded, qt, k, vt, h, sz, sg, pc, wts["wmu"], wts["wout"])


def _to_head_lanes(a):
    n = a.shape[-1]
    half = QK_ROPE_DIM // 2
    split = LANES // 2 - half
    zeros = lambda w: jnp.zeros(a.shape[:-1] + (w,), a.dtype)
    rot1 = a[..., QK_NOPE_DIM:QK_NOPE_DIM + half] if n > QK_NOPE_DIM else zeros(half)
    rot2 = a[..., QK_NOPE_DIM + half:] if n > QK_NOPE_DIM else zeros(half)
    assert [HEAD_LANES[d] for d in (0, split, QK_NOPE_DIM, QK_NOPE_DIM + half)] == [half, LANES // 2 + half, 0,
                                                                                     LANES // 2]
    return jnp.concatenate([rot1, a[..., :split], rot2, a[..., split:QK_NOPE_DIM], zeros(HEAD_PAD - QK_HEAD_DIM)],
                           axis=-1)


def _prepare_weights(norm_gain, w_in, pool_w_group, pool_scale, pool_w_up, q_a_norm_gain, kv_a_norm_gain, w_q_b,
                     w_kv_b, q_norm_gain, k_norm_gain, mla_w_up, w_out):
    depth = w_in.shape[0]
    half = QK_ROPE_DIM // 2
    kr0 = C_KV + KV_LORA_RANK
    place = lambda a, at: jnp.pad(a, ((0, 0), (0, 0), (at, C_END - at - a.shape[-1])))
    same = lax.pad(w_in, jnp.zeros((), w_in.dtype), [(0, 0, 0), (0, 0, 0), (0, C_END - w_in.shape[-1], 0)])
    moved = lax.pad(w_in, jnp.zeros((), w_in.dtype),
                    [(0, 0, 0), (0, 0, 0), (-QK_ROPE_DIM, C_END - w_in.shape[-1] + QK_ROPE_DIM, 0)])
    col = lax.broadcasted_iota(jnp.int32, (1, 1, C_END), 2)
    win = (jnp.where(col < kr0, same, jnp.where(col < C_KR, moved, 0.0))
           + place(w_in[:, :, kr0:kr0 + half], C_KR + HEAD_LANES[QK_NOPE_DIM])
           + place(w_in[:, :, kr0 + half:kr0 + QK_ROPE_DIM], C_KR + HEAD_LANES[QK_NOPE_DIM + half])).astype(BF16)
    wqb = _to_head_lanes(w_q_b.reshape(depth, Q_LORA_RANK, N_HEADS, QK_HEAD_DIM)).reshape(depth, Q_LORA_RANK, -1)
    wkv = w_kv_b.reshape(depth, KV_LORA_RANK, N_HEADS, QK_NOPE_DIM + V_HEAD_DIM)
    wkvb = jnp.concatenate([_to_head_lanes(wkv[..., :QK_NOPE_DIM]).reshape(depth, KV_LORA_RANK, -1),
                            wkv[..., QK_NOPE_DIM:].reshape(depth, KV_LORA_RANK, -1)], axis=2)
    eye = jnp.eye(pool_w_group.shape[1], dtype=pool_w_group.dtype)
    wg = (pool_w_group[:, :, :, None, :] * eye[None, :, None, :, None]).reshape(depth, POOL_WIDTH, POOL_WIDTH)
    q_scale = math.log2(math.e) / math.sqrt(QK_HEAD_DIM)
    block = jnp.arange(2 * HEAD_PAD) // HEAD_PAD
    row = lambda g: g[:, None, :]
    return {
        "ones": (block[:, None] == block[None, :]).astype(BF16),
        "ng": row(norm_gain),
        "win": win,
        "wg": wg.astype(BF16),
        "ps": row(pool_scale),
        "wpu": pool_w_up.astype(BF16),
        "gqa": row(q_a_norm_gain),
        "gkva": row(kv_a_norm_gain),
        "wqb": wqb.astype(BF16),
        "wkvb": wkvb.astype(BF16),
        "gqn": row(jnp.tile(_to_head_lanes(q_norm_gain * q_scale), (1, 2))),
        "gkn": row(jnp.tile(_to_head_lanes(k_norm_gain), (1, 2))),
        "wmu": mla_w_up.astype(BF16),
        "wout": w_out.astype(BF16),
    }


def kernel(x, positions, meta_tokens, norm_gain, w_in, pool_w_group, pool_scale, pool_w_up, q_a_norm_gain,
           kv_a_norm_gain, w_q_b, w_kv_b, q_norm_gain, k_norm_gain, mla_w_up, w_out):
    b, seq, _ = x.shape
    lp = PAD_FRONT + N_META + seq + PAD_BACK
    assert lp % ROW_TILE == 0 and ROW_TILE % ATT_TILE == 0

    meta_pos = jnp.broadcast_to(jnp.arange(N_META, dtype=jnp.int32)[None], (b, N_META))
    pos = jnp.concatenate([jnp.zeros((b, PAD_FRONT), jnp.int32), meta_pos, positions + N_META,
                           jnp.zeros((b, PAD_BACK), jnp.int32)], axis=1)

    tabs = _rope_tables(pos)

    wts = _prepare_weights(norm_gain, w_in, pool_w_group, pool_scale, pool_w_up, q_a_norm_gain, kv_a_norm_gain,
                           w_q_b, w_kv_b, q_norm_gain, k_norm_gain, mla_w_up, w_out)
    bounded = _scores_bounded(q_norm_gain, k_norm_gain)
    q, k, v, sz, sg, pc, h = _pre(x, tabs, wts, 0, meta=meta_tokens.astype(x.dtype))
    for l in range(DEPTH):
        if l > 0:
            q, k, v, sz, sg, pc = _pre(h, tabs, wts, l)
        h = _attention(bounded, l, q, k, v, h, sz, sg, pc, wts, final=(l == DEPTH - 1))

    return h
```

```python
import functools
import math

import jax
import jax.numpy as jnp
from jax import lax
from jax.experimental import pallas as pl
from jax.experimental.pallas import tpu as pltpu

F32 = jnp.float32
BF16 = jnp.bfloat16

D_MODEL = 1024
DEPTH = 4
N_META = 16
POOL_WIDTH = 512
POOL_WINDOWS = (2, 4, 8, 16)
N_HEADS = 8
QK_NOPE_DIM = 64
QK_ROPE_DIM = 32
QK_HEAD_DIM = 96
V_HEAD_DIM = 64
MLA_WIDTH = 512
KV_LORA_RANK = 256
Q_LORA_RANK = 768
ROPE_THETA = 10000.0
NORM_EPS = 1e-6
MASK_VALUE = -1e30
PAD_KEY_SCORE = MASK_VALUE

LANES = 128
HEAD_PAD = LANES
PAD_FRONT = 240
PAD_BACK = 0
ROW_TILE = 768
ATT_TILE = 256
KV_GROUPS = (8, 4, 2)
SCORE_BOUND = 32.0
HALO = 16
VMEM_LIMIT = 56 * 1024 * 1024

C_U, C_Z, C_Q, C_KV, C_ZM, C_GP, C_GM, C_KR, C_END = 0, 512, 1024, 1792, 2048, 2560, 3584, 4608, 4736


def _rms(x, gain, n):
    inv = lax.rsqrt(jnp.sum(x * x, axis=-1, keepdims=True) * (1.0 / n) + NORM_EPS)
    return x * inv * gain


def _head_lane(d):
    half = QK_ROPE_DIM // 2
    if d >= QK_NOPE_DIM:
        r = d - QK_NOPE_DIM
        return r if r < half else LANES // 2 + (r - half)
    return half + d if d < LANES // 2 - half else LANES // 2 + half + (d - (LANES // 2 - half))


HEAD_LANES = [_head_lane(d) for d in range(QK_HEAD_DIM)]


def _head_norm_rope(x, ones_ref, gain, c, s):
    sq = x * x
    hi = sq.astype(BF16)
    lo = (sq - hi.astype(F32)).astype(BF16)
    ss = (jnp.dot(hi, ones_ref[...], preferred_element_type=F32)
          + jnp.dot(lo, ones_ref[...], preferred_element_type=F32))
    xn = x * lax.rsqrt(ss * (1.0 / QK_HEAD_DIM) + NORM_EPS) * gain
    out = []
    for j in range(x.shape[1] // LANES):
        blk = xn[:, j * LANES:(j + 1) * LANES]
        out.append(blk * c + pltpu.roll(blk, LANES // 2, axis=1) * s)
    return out


def _rope_table_kernel(pos_ref, freq_ref, c_ref, s_ref):
    ang = pos_ref[0].astype(F32) * freq_ref[...]
    lane = lax.broadcasted_iota(jnp.int32, ang.shape, 1)
    c_ref[0] = jnp.cos(ang)
    s_ref[0] = jnp.where(lane < LANES // 2, -jnp.sin(ang), jnp.sin(ang))


def _pre_kernel(*refs, first_layer):
    n_src = (1 + ROW_TILE // ATT_TILE) if first_layer else 1
    src, refs = refs[:n_src], refs[n_src:]
    (c_ref, s_ref, ng_ref, wlo_ref, whi_ref, wg_ref, ps_ref, wpu_ref, gqa_ref, gkva_ref, wqb_ref, wkvb_ref, gqn_ref,
     gkn_ref, ones_ref, q_ref, k_ref, v_ref, sz_ref, sg_ref, pc_ref, *rest) = refs
    halo_ref = rest[-1]
    i = pl.program_id(1)
    if first_layer:
        meta_ref, blocks = src[0], src[1:]
        lead = jnp.concatenate([jnp.zeros((PAD_FRONT, D_MODEL), F32), meta_ref[...]], axis=0)
        x = jnp.concatenate([jnp.where(i == 0, lead, blocks[0][0])] + [blk[0] for blk in blocks[1:]], axis=0)
        rest[0][0] = x
    else:
        x = src[0][0]
    t = x.shape[0]
    hn = _rms(x, ng_ref[...], D_MODEL).astype(BF16)

    def proj(lo, hi):
        w = wlo_ref[:, lo:hi] if hi <= C_ZM else whi_ref[:, lo - C_ZM:hi - C_ZM]
        return jnp.dot(hn, w, preferred_element_type=F32)

    c, s = c_ref[0], s_ref[0]
    cqn = _rms(proj(C_Q, C_KV), gqa_ref[...], Q_LORA_RANK).astype(BF16)
    qp = jnp.dot(cqn, wqb_ref[...], preferred_element_type=F32)
    ckvn = _rms(proj(C_KV, C_ZM), gkva_ref[...], KV_LORA_RANK).astype(BF16)
    kv = jnp.dot(ckvn, wkvb_ref[...], preferred_element_type=F32)
    krp = proj(C_KR, C_END)
    krp2 = jnp.concatenate([krp, krp], axis=1)
    row = i * t + lax.broadcasted_iota(jnp.int32, (t, 1), 0)
    sentinel = lax.broadcasted_iota(jnp.int32, (1, HEAD_PAD), 1) == QK_HEAD_DIM
    q_mark = jnp.where(sentinel, 1.0, 0.0)
    k_mark = jnp.where(sentinel & (row < PAD_FRONT), PAD_KEY_SCORE, 0.0)
    width = 2 * HEAD_PAD
    for pair in range(N_HEADS // 2):
        lo = pair * width
        qs = _head_norm_rope(qp[:, lo:lo + width], ones_ref, gqn_ref[...], c, s)
        ks = _head_norm_rope(kv[:, lo:lo + width] + krp2, ones_ref, gkn_ref[...], c, s)
        for odd in range(2):
            q_ref[0, 2 * pair + odd] = (qs[odd] + q_mark).astype(BF16)
            k_ref[0, 2 * pair + odd] = (ks[odd] + k_mark).astype(BF16)
        v0 = N_HEADS * HEAD_PAD + pair * LANES
        vbt = kv[:, v0:v0 + LANES].T.astype(BF16)
        for ch in range(t // ATT_TILE):
            v_ref[0, pair, ch] = vbt[:, ch * ATT_TILE:(ch + 1) * ATT_TILE]

    @pl.when(i == 0)
    def _():
        halo_ref[...] = jnp.zeros_like(halo_ref)

    u = proj(C_U, C_Z)
    uext = jnp.concatenate([halo_ref[...], u], axis=0)
    halo_ref[...] = u[t - HALO:, :]
    avail = jnp.maximum(row - (PAD_FRONT - 1), 1).astype(F32)
    mixed = []
    for g, w in enumerate(POOL_WINDOWS):
        a = uext[:, g * LANES:(g + 1) * LANES]
        ug = a[HALO:, :]
        step = 1
        while step < w:
            a = a + pltpu.roll(a, step, axis=0)
            step *= 2
        inv_cnt = 1.0 / jnp.minimum(avail, float(w))
        mixed.append((a[HALO:, :] * inv_cnt - ug).astype(BF16))
    ymix = jnp.dot(jnp.concatenate(mixed, axis=1), wg_ref[...], preferred_element_type=F32) * ps_ref[...]
    zp = proj(C_Z, C_Q)
    pooled = (ymix * (zp * jax.nn.sigmoid(zp))).astype(BF16)
    y_pool = jnp.dot(pooled, wpu_ref[...], preferred_element_type=F32)
    pc_ref[0] = (jax.nn.sigmoid(proj(C_GP, C_GM)) * y_pool).astype(pc_ref.dtype)

    zm = proj(C_ZM, C_GP)
    sz_ref[0] = (zm * jax.nn.sigmoid(zm)).astype(sz_ref.dtype)
    sg_ref[0] = jax.nn.sigmoid(proj(C_GM, C_KR)).astype(sg_ref.dtype)


def _attn_kernel(bounded_ref, q_ref, knew_ref, vtnew_ref, h_ref, sz_ref, sg_ref, pc_ref, wmu_ref, wout_ref, out_ref,
                 k_ref, vt_ref, m_ref, l_ref, acc_ref, o_ref, *, layer):
    qi = pl.program_id(1)
    n_tiles = pl.num_programs(1) - 1
    tq = q_ref.shape[2]
    ck = vt_ref.shape[4]
    heads = range(q_ref.shape[1])

    def output_stage():
        om = (o_ref[...] * sz_ref[0].astype(F32)).astype(BF16)
        y_mla = jnp.dot(om, wmu_ref[...], preferred_element_type=F32)
        merged = (pc_ref[0].astype(F32) + sg_ref[0].astype(F32) * y_mla).astype(BF16)
        out_ref[0] = h_ref[0] + jnp.dot(merged, wout_ref[...], preferred_element_type=F32)

    def span_scores(c0, n):
        start = pl.multiple_of(c0 * ck, ck)
        return [lax.dot_general(k_ref[0, hh, pl.ds(start, n * ck), :], q_ref[0, hh], (((1,), (1,)), ((), ())),
                                preferred_element_type=F32)
                for hh in heads]

    def visible(c0, shape):
        kpos = c0 * ck + lax.broadcasted_iota(jnp.int32, shape, 0)
        qpos = qi * tq + lax.broadcasted_iota(jnp.int32, shape, 1)
        return (kpos <= qpos) | (qpos < PAD_FRONT)

    def weighted_values(c0, n, hh, p):
        pb = p.astype(BF16)
        lo = (hh % 2) * V_HEAD_DIM
        out = None
        for c in range(n):
            vt = vt_ref[0, hh // 2, c0 + c][lo:lo + V_HEAD_DIM, :]
            d = jnp.dot(vt, pb[c * ck:(c + 1) * ck, :], preferred_element_type=F32)
            out = d if out is None else out + d
        return out

    def step_bounded(c0, n, masked):
        scores = span_scores(c0, n)
        for hh in heads:
            p = jnp.exp2(scores[hh])
            if masked:
                p = jnp.where(visible(c0, p.shape), p, 0.0)
            l_ref[hh] += jnp.sum(p.reshape(n * ck // 8, 8, tq), axis=0)
            acc_ref[hh] += weighted_values(c0, n, hh, p)

    def step_general(c0, n, masked):
        scores = span_scores(c0, n)
        for hh in heads:
            s = scores[hh]
            if masked:
                s = jnp.where(visible(c0, s.shape), s, MASK_VALUE)
            m_prev = m_ref[hh]
            m_new = jnp.maximum(m_prev, jnp.max(s, axis=0, keepdims=True))
            alpha = jnp.exp2(m_prev - m_new)
            p = jnp.exp2(s - m_new)
            l_ref[hh, 0:1, :] = alpha * l_ref[hh, 0:1, :] + jnp.sum(p, axis=0, keepdims=True)
            acc_ref[hh] = alpha * acc_ref[hh] + weighted_values(c0, n, hh, p)
            m_ref[hh] = m_new

    def run(step, groups):
        m_ref[...] = jnp.full_like(m_ref, -jnp.inf)
        l_ref[...] = jnp.zeros_like(l_ref)
        acc_ref[...] = jnp.zeros_like(acc_ref)

        first, left = 0, qi
        for group in groups:
            def body(g, carry, first=first, group=group):
                step(first + g * group, group, False)
                return carry

            trips = left // group
            lax.fori_loop(0, trips, body, 0)
            first, left = first + trips * group, left - trips * group
        for r in range(groups[-1]):
            @pl.when(left == r)
            def _(r=r):
                output_stage()
                step(qi - r, r + 1, True)
                out_t = jnp.concatenate([acc_ref[hh] / jnp.sum(l_ref[hh], axis=0, keepdims=True) for hh in heads],
                                        axis=0)
                o_ref[...] = out_t.T

    @pl.when(qi == 0)
    def _():
        o_ref[...] = jnp.zeros_like(o_ref)

    @pl.when(qi == n_tiles)
    def _():
        output_stage()

    @pl.when(qi < n_tiles)
    def _():
        k_ref[0, :, pl.ds(pl.multiple_of(qi * ck, ck), ck), :] = knew_ref[0]
        vt_ref[0, :, qi] = vtnew_ref[0, :, 0]
        bounded = bounded_ref[layer] != 0

        @pl.when(bounded)
        def _():
            run(step_bounded, KV_GROUPS)

        @pl.when(jnp.logical_not(bounded))
        def _():
            run(step_general, KV_GROUPS[-1:])


def _const_spec(shape):
    return pl.BlockSpec(shape, lambda *_: (0,) * len(shape))


def _layer_spec(stacked, layer):
    rest = stacked.shape[1:]
    return pl.BlockSpec((None,) + rest, lambda *_: (layer,) + (0,) * len(rest))


def _row_spec(t, width):
    return pl.BlockSpec((1, t, width), lambda b, i: (b, i, 0))


def _rope_tables(pos_pad):
    b, lp = pos_pad.shape
    t = ROW_TILE
    half = QK_ROPE_DIM // 2
    inv_freq = ROPE_THETA ** (-jnp.arange(half, dtype=F32) / half)
    freq = _to_head_lanes(jnp.concatenate([jnp.zeros((QK_NOPE_DIM,), F32), inv_freq, inv_freq]))[None, :]
    out = jax.ShapeDtypeStruct((b, lp, LANES), F32)
    return pl.pallas_call(
        _rope_table_kernel,
        grid=(b, lp // t),
        in_specs=[_row_spec(t, 1), _const_spec((1, LANES))],
        out_specs=[_row_spec(t, LANES)] * 2,
        out_shape=[out] * 2,
        name="rope_tables",
    )(pos_pad[:, :, None], freq)


def _pre(h, tabs, wts, layer, meta=None):
    first_layer = meta is not None
    b = h.shape[0]
    lp = tabs[0].shape[1]
    t = ROW_TILE
    nch = t // ATT_TILE
    if first_layer:
        assert PAD_FRONT + N_META == ATT_TILE and PAD_BACK == 0
        token_block = lambda j: pl.BlockSpec((1, ATT_TILE, D_MODEL),
                                             lambda bb, i: (bb, jnp.maximum(nch * i - 1 + j, 0), 0))
        src = [meta] + [h] * nch
        src_specs = [_const_spec(meta.shape)] + [token_block(j) for j in range(nch)]
        extra_specs, extra_shapes = [_row_spec(t, D_MODEL)], [jax.ShapeDtypeStruct((b, lp, D_MODEL), F32)]
    else:
        src, src_specs, extra_specs, extra_shapes = [h], [_row_spec(t, D_MODEL)], [], []
    k_shape = jax.ShapeDtypeStruct((b, N_HEADS, lp, HEAD_PAD), BF16)
    k_spec = pl.BlockSpec((1, N_HEADS, t, HEAD_PAD), lambda bb, i: (bb, 0, i, 0))
    vt_shape = jax.ShapeDtypeStruct((b, N_HEADS // 2, lp // ATT_TILE, 2 * V_HEAD_DIM, ATT_TILE), BF16)
    vt_spec = pl.BlockSpec((1, N_HEADS // 2, nch, 2 * V_HEAD_DIM, ATT_TILE), lambda bb, i: (bb, 0, i, 0, 0))
    weights = [wts[name] for name in ("ng", "win_lo", "win_hi", "wg", "ps", "wpu", "gqa", "gkva", "wqb", "wkvb",
                                      "gqn", "gkn")]
    return pl.pallas_call(
        functools.partial(_pre_kernel, first_layer=first_layer),
        grid=(b, lp // t),
        in_specs=(src_specs + [_row_spec(t, LANES)] * 2 + [_layer_spec(w, layer) for w in weights]
                  + [_const_spec(wts["ones"].shape)]),
        out_specs=[k_spec, k_spec, vt_spec, _row_spec(t, MLA_WIDTH), _row_spec(t, D_MODEL),
                   _row_spec(t, D_MODEL)] + extra_specs,
        out_shape=[k_shape, k_shape, vt_shape,
                   jax.ShapeDtypeStruct((b, lp, MLA_WIDTH), BF16),
                   jax.ShapeDtypeStruct((b, lp, D_MODEL), BF16),
                   jax.ShapeDtypeStruct((b, lp, D_MODEL), BF16)] + extra_shapes,
        scratch_shapes=[pltpu.VMEM((HALO, POOL_WIDTH), F32)],
        compiler_params=pltpu.CompilerParams(dimension_semantics=("parallel", "arbitrary"),
                                             vmem_limit_bytes=VMEM_LIMIT),
        name="pre",
    )(*src, *tabs, *weights, wts["ones"])


def _scores_bounded(q_gain, k_gain):
    limit = (QK_HEAD_DIM * jnp.max(jnp.abs(q_gain), axis=-1) * jnp.max(jnp.abs(k_gain), axis=-1)
             * (math.log2(math.e) / math.sqrt(QK_HEAD_DIM)) * 1.02)
    return (limit <= SCORE_BOUND).astype(jnp.int32)


def _attention(bounded, layer, qt, k, vt, h, sz, sg, pc, wts, final):
    b, nh, lp, _ = k.shape
    t = ATT_TILE
    n_tiles = lp // t
    assert PAD_FRONT + N_META == t
    tile = lambda i: jnp.minimum(i, n_tiles - 1)
    prev = lambda i: jnp.maximum(i - 1, 0)
    if final:
        out_spec = pl.BlockSpec((1, t, D_MODEL), lambda bb, i, f: (bb, jnp.maximum(i - 2, 0), 0))
        out_shape = jax.ShapeDtypeStruct((b, lp - t - PAD_BACK, D_MODEL), F32)
        aliases = {}
    else:
        out_spec = pl.BlockSpec((1, t, D_MODEL), lambda bb, i, f: (bb, prev(i), 0))
        out_shape = jax.ShapeDtypeStruct(h.shape, F32)
        aliases = {4: 0}
    single = pl.Buffered(1)
    rows = lambda width: pl.BlockSpec((1, t, width), lambda bb, i, f: (bb, prev(i), 0))
    weight = lambda w: pl.BlockSpec((None,) + w.shape[1:], lambda bb, i, f: (layer, 0, 0), pipeline_mode=single)
    grid_spec = pltpu.PrefetchScalarGridSpec(
        num_scalar_prefetch=1,
        grid=(b, n_tiles + 1),
        in_specs=[pl.BlockSpec((1, nh, t, HEAD_PAD), lambda bb, i, f: (bb, 0, tile(i), 0)),
                  pl.BlockSpec((1, nh, t, HEAD_PAD), lambda bb, i, f: (bb, 0, tile(i), 0)),
                  pl.BlockSpec((1, nh // 2, 1, 2 * V_HEAD_DIM, t), lambda bb, i, f: (bb, 0, tile(i), 0, 0)),
                  rows(D_MODEL), rows(MLA_WIDTH), rows(D_MODEL), rows(D_MODEL),
                  weight(wts["wmu"]), weight(wts["wout"])],
        out_specs=out_spec,
        scratch_shapes=[pltpu.VMEM((1, nh, lp, HEAD_PAD), BF16),
                        pltpu.VMEM((1, nh // 2, lp // t, 2 * V_HEAD_DIM, t), BF16),
                        pltpu.VMEM((nh, 1, t), F32),
                        pltpu.VMEM((nh, 8, t), F32),
                        pltpu.VMEM((nh, V_HEAD_DIM, t), F32),
                        pltpu.VMEM((t, nh * V_HEAD_DIM), F32)],
    )
    return pl.pallas_call(
        functools.partial(_attn_kernel, layer=layer),
        grid_spec=grid_spec,
        out_shape=out_shape,
        input_output_aliases=aliases,
        compiler_params=pltpu.CompilerParams(dimension_semantics=("parallel", "arbitrary"),
                                             vmem_limit_bytes=VMEM_LIMIT),
        name="attn",
    )(bounded, qt, k, vt, h, sz, sg, pc, wts["wmu"], wts["wout"])


def _to_head_lanes(a):
    n = a.shape[-1]
    half = QK_ROPE_DIM // 2
    split = LANES // 2 - half
    zeros = lambda w: jnp.zeros(a.shape[:-1] + (w,), a.dtype)
    rot1 = a[..., QK_NOPE_DIM:QK_NOPE_DIM + half] if n > QK_NOPE_DIM else zeros(half)
    rot2 = a[..., QK_NOPE_DIM + half:] if n > QK_NOPE_DIM else zeros(half)
    assert [HEAD_LANES[d] for d in (0, split, QK_NOPE_DIM, QK_NOPE_DIM + half)] == [half, LANES // 2 + half, 0,
                                                                                     LANES // 2]
    return jnp.concatenate([rot1, a[..., :split], rot2, a[..., split:QK_NOPE_DIM], zeros(HEAD_PAD - QK_HEAD_DIM)],
                           axis=-1)


def _prepare_weights(norm_gain, w_in, pool_w_group, pool_scale, pool_w_up, q_a_norm_gain, kv_a_norm_gain, w_q_b,
                     w_kv_b, q_norm_gain, k_norm_gain, mla_w_up, w_out):
    depth = w_in.shape[0]
    half = QK_ROPE_DIM // 2
    kr0 = C_ZM
    assert kr0 == C_KV + KV_LORA_RANK
    wide = C_END - C_ZM
    place = lambda a, at: jnp.pad(a, ((0, 0), (0, 0), (at, wide - at - a.shape[-1])))
    tail = lax.pad(w_in, jnp.zeros((), w_in.dtype),
                   [(0, 0, 0), (0, 0, 0), (-(kr0 + QK_ROPE_DIM), wide - (w_in.shape[-1] - kr0 - QK_ROPE_DIM), 0)])
    win_lo = w_in[:, :, :kr0].astype(BF16)
    win_hi = (tail + place(w_in[:, :, kr0:kr0 + half], C_KR - C_ZM + HEAD_LANES[QK_NOPE_DIM])
              + place(w_in[:, :, kr0 + half:kr0 + QK_ROPE_DIM],
                      C_KR - C_ZM + HEAD_LANES[QK_NOPE_DIM + half])).astype(BF16)
    wqb = _to_head_lanes(w_q_b.reshape(depth, Q_LORA_RANK, N_HEADS, QK_HEAD_DIM)).reshape(depth, Q_LORA_RANK, -1)
    wkv = w_kv_b.reshape(depth, KV_LORA_RANK, N_HEADS, QK_NOPE_DIM + V_HEAD_DIM)
    wkvb = jnp.concatenate([_to_head_lanes(wkv[..., :QK_NOPE_DIM]).reshape(depth, KV_LORA_RANK, -1),
                            wkv[..., QK_NOPE_DIM:].reshape(depth, KV_LORA_RANK, -1)], axis=2)
    eye = jnp.eye(pool_w_group.shape[1], dtype=pool_w_group.dtype)
    wg = (pool_w_group[:, :, :, None, :] * eye[None, :, None, :, None]).reshape(depth, POOL_WIDTH, POOL_WIDTH)
    q_scale = math.log2(math.e) / math.sqrt(QK_HEAD_DIM)
    block = jnp.arange(2 * HEAD_PAD) // HEAD_PAD
    row = lambda g: g[:, None, :]
    return {
        "ones": (block[:, None] == block[None, :]).astype(BF16),
        "ng": row(norm_gain),
        "win_lo": win_lo,
        "win_hi": win_hi,
        "wg": wg.astype(BF16),
        "ps": row(pool_scale),
        "wpu": pool_w_up.astype(BF16),
        "gqa": row(q_a_norm_gain),
        "gkva": row(kv_a_norm_gain),
        "wqb": wqb.astype(BF16),
        "wkvb": wkvb.astype(BF16),
        "gqn": row(jnp.tile(_to_head_lanes(q_norm_gain * q_scale), (1, 2))),
        "gkn": row(jnp.tile(_to_head_lanes(k_norm_gain), (1, 2))),
        "wmu": mla_w_up.astype(BF16),
        "wout": w_out.astype(BF16),
    }


def kernel(x, positions, meta_tokens, norm_gain, w_in, pool_w_group, pool_scale, pool_w_up, q_a_norm_gain,
           kv_a_norm_gain, w_q_b, w_kv_b, q_norm_gain, k_norm_gain, mla_w_up, w_out):
    b, seq, _ = x.shape
    lp = PAD_FRONT + N_META + seq + PAD_BACK
    assert lp % ROW_TILE == 0 and ROW_TILE % ATT_TILE == 0

    meta_pos = jnp.broadcast_to(jnp.arange(N_META, dtype=jnp.int32)[None], (b, N_META))
    pos = jnp.concatenate([jnp.zeros((b, PAD_FRONT), jnp.int32), meta_pos, positions + N_META,
                           jnp.zeros((b, PAD_BACK), jnp.int32)], axis=1)

    tabs = _rope_tables(pos)

    wts = _prepare_weights(norm_gain, w_in, pool_w_group, pool_scale, pool_w_up, q_a_norm_gain, kv_a_norm_gain,
                           w_q_b, w_kv_b, q_norm_gain, k_norm_gain, mla_w_up, w_out)
    bounded = _scores_bounded(q_norm_gain, k_norm_gain)
    q, k, v, sz, sg, pc, h = _pre(x, tabs, wts, 0, meta=meta_tokens.astype(x.dtype))
    for l in range(DEPTH):
        if l > 0:
            q, k, v, sz, sg, pc = _pre(h, tabs, wts, l)
        h = _attention(bounded, l, q, k, v, h, sz, sg, pc, wts, final=(l == DEPTH - 1))

    return h
```

```python
import functools
import math

import jax
import jax.numpy as jnp
from jax import lax
from jax.experimental import pallas as pl
from jax.experimental.pallas import tpu as pltpu

F32 = jnp.float32
BF16 = jnp.bfloat16

D_MODEL = 1024
DEPTH = 4
N_META = 16
POOL_WIDTH = 512
POOL_WINDOWS = (2, 4, 8, 16)
N_HEADS = 8
QK_NOPE_DIM = 64
QK_ROPE_DIM = 32
QK_HEAD_DIM = 96
V_HEAD_DIM = 64
MLA_WIDTH = 512
KV_LORA_RANK = 256
Q_LORA_RANK = 768
ROPE_THETA = 10000.0
NORM_EPS = 1e-6
MASK_VALUE = -1e30
PAD_KEY_SCORE = MASK_VALUE

LANES = 128
HEAD_PAD = LANES
PAD_FRONT = 240
PAD_BACK = 0
ROW_TILE = 768
ATT_TILE = 256
KV_GROUPS = (8, 4, 2)
SCORE_BOUND = 32.0
HALO = 16
VMEM_LIMIT = 56 * 1024 * 1024

C_U, C_Z, C_Q, C_KV, C_ZM, C_GP, C_GM, C_KR, C_END = 0, 512, 1024, 1792, 2048, 2560, 3584, 4608, 4736


def _rms(x, gain, n):
    inv = lax.rsqrt(jnp.sum(x * x, axis=-1, keepdims=True) * (1.0 / n) + NORM_EPS)
    return x * inv * gain


def _head_lane(d):
    half = QK_ROPE_DIM // 2
    if d >= QK_NOPE_DIM:
        r = d - QK_NOPE_DIM
        return r if r < half else LANES // 2 + (r - half)
    return half + d if d < LANES // 2 - half else LANES // 2 + half + (d - (LANES // 2 - half))


HEAD_LANES = [_head_lane(d) for d in range(QK_HEAD_DIM)]


def _head_norm_rope(x, ones_ref, gain, c, s):
    sq = x * x
    hi = sq.astype(BF16)
    lo = (sq - hi.astype(F32)).astype(BF16)
    ss = (jnp.dot(hi, ones_ref[...], preferred_element_type=F32)
          + jnp.dot(lo, ones_ref[...], preferred_element_type=F32))
    xn = x * lax.rsqrt(ss * (1.0 / QK_HEAD_DIM) + NORM_EPS) * gain
    out = []
    for j in range(x.shape[1] // LANES):
        blk = xn[:, j * LANES:(j + 1) * LANES]
        out.append(blk * c + pltpu.roll(blk, LANES // 2, axis=1) * s)
    return out


def _rope_tables(pos, freq):
    ang = pos.astype(F32) * freq
    lane = lax.broadcasted_iota(jnp.int32, ang.shape, 1)
    return jnp.cos(ang), jnp.where(lane < LANES // 2, -jnp.sin(ang), jnp.sin(ang))


def _pre_kernel(*refs, first_layer):
    n_src = (1 + ROW_TILE // ATT_TILE) if first_layer else 1
    src, refs = refs[:n_src], refs[n_src:]
    (ta_ref, tb_ref, ng_ref, win_ref, wg_ref, ps_ref, wpu_ref, gqa_ref, gkva_ref, wqb_ref, wkvb_ref, gqn_ref, gkn_ref,
     ones_ref, q_ref, k_ref, v_ref, sz_ref, sg_ref, pc_ref, *rest) = refs
    halo_ref = rest[-1]
    i = pl.program_id(1)
    if first_layer:
        meta_ref, blocks = src[0], src[1:]
        lead = jnp.concatenate([jnp.zeros((PAD_FRONT, D_MODEL), F32), meta_ref[...]], axis=0)
        x = jnp.concatenate([jnp.where(i == 0, lead, blocks[0][0])] + [blk[0] for blk in blocks[1:]], axis=0)
        rest[0][0] = x
        c, s = _rope_tables(ta_ref[0], tb_ref[...])
        rest[1][0], rest[2][0] = c, s
    else:
        x = src[0][0]
        c, s = ta_ref[0], tb_ref[0]
    t = x.shape[0]
    hn = _rms(x, ng_ref[...], D_MODEL).astype(BF16)

    def proj(lo, hi):
        return jnp.dot(hn, win_ref[:, lo:hi], preferred_element_type=F32)

    cqn = _rms(proj(C_Q, C_KV), gqa_ref[...], Q_LORA_RANK).astype(BF16)
    qp = jnp.dot(cqn, wqb_ref[...], preferred_element_type=F32)
    ckvn = _rms(proj(C_KV, C_ZM), gkva_ref[...], KV_LORA_RANK).astype(BF16)
    kv = jnp.dot(ckvn, wkvb_ref[...], preferred_element_type=F32)
    krp = proj(C_KR, C_END)
    krp2 = jnp.concatenate([krp, krp], axis=1)
    row = i * t + lax.broadcasted_iota(jnp.int32, (t, 1), 0)
    sentinel = lax.broadcasted_iota(jnp.int32, (1, HEAD_PAD), 1) == QK_HEAD_DIM
    q_mark = jnp.where(sentinel, 1.0, 0.0)
    k_mark = jnp.where(sentinel & (row < PAD_FRONT), PAD_KEY_SCORE, 0.0)
    width = 2 * HEAD_PAD
    for pair in range(N_HEADS // 2):
        lo = pair * width
        qs = _head_norm_rope(qp[:, lo:lo + width], ones_ref, gqn_ref[...], c, s)
        ks = _head_norm_rope(kv[:, lo:lo + width] + krp2, ones_ref, gkn_ref[...], c, s)
        for odd in range(2):
            q_ref[0, 2 * pair + odd] = (qs[odd] + q_mark).astype(BF16)
            k_ref[0, 2 * pair + odd] = (ks[odd] + k_mark).astype(BF16)
        v0 = N_HEADS * HEAD_PAD + pair * LANES
        vbt = kv[:, v0:v0 + LANES].T.astype(BF16)
        for ch in range(t // ATT_TILE):
            v_ref[0, pair, ch] = vbt[:, ch * ATT_TILE:(ch + 1) * ATT_TILE]

    @pl.when(i == 0)
    def _():
        halo_ref[...] = jnp.zeros_like(halo_ref)

    u = proj(C_U, C_Z)
    uext = jnp.concatenate([halo_ref[...], u], axis=0)
    halo_ref[...] = u[t - HALO:, :]
    avail = jnp.maximum(row - (PAD_FRONT - 1), 1).astype(F32)
    mixed = []
    for g, w in enumerate(POOL_WINDOWS):
        a = uext[:, g * LANES:(g + 1) * LANES]
        ug = a[HALO:, :]
        step = 1
        while step < w:
            a = a + pltpu.roll(a, step, axis=0)
            step *= 2
        inv_cnt = 1.0 / jnp.minimum(avail, float(w))
        mixed.append((a[HALO:, :] * inv_cnt - ug).astype(BF16))
    ymix = jnp.dot(jnp.concatenate(mixed, axis=1), wg_ref[...], preferred_element_type=F32) * ps_ref[...]
    zp = proj(C_Z, C_Q)
    pooled = (ymix * (zp * jax.nn.sigmoid(zp))).astype(BF16)
    y_pool = jnp.dot(pooled, wpu_ref[...], preferred_element_type=F32)
    pc_ref[0] = (jax.nn.sigmoid(proj(C_GP, C_GM)) * y_pool).astype(pc_ref.dtype)

    zm = proj(C_ZM, C_GP)
    sz_ref[0] = (zm * jax.nn.sigmoid(zm)).astype(sz_ref.dtype)
    sg_ref[0] = jax.nn.sigmoid(proj(C_GM, C_KR)).astype(sg_ref.dtype)


def _attn_kernel(bounded_ref, q_ref, knew_ref, vtnew_ref, h_ref, sz_ref, sg_ref, pc_ref, wmu_ref, wout_ref, out_ref,
                 k_ref, vt_ref, m_ref, l_ref, acc_ref, o_ref, *, layer):
    qi = pl.program_id(1)
    n_tiles = pl.num_programs(1) - 1
    tq = q_ref.shape[2]
    ck = vt_ref.shape[4]
    heads = range(q_ref.shape[1])

    def output_stage():
        om = (o_ref[...] * sz_ref[0].astype(F32)).astype(BF16)
        y_mla = jnp.dot(om, wmu_ref[...], preferred_element_type=F32)
        merged = (pc_ref[0].astype(F32) + sg_ref[0].astype(F32) * y_mla).astype(BF16)
        out_ref[0] = h_ref[0] + jnp.dot(merged, wout_ref[...], preferred_element_type=F32)

    def span_scores(c0, n):
        start = pl.multiple_of(c0 * ck, ck)
        return [lax.dot_general(k_ref[0, hh, pl.ds(start, n * ck), :], q_ref[0, hh], (((1,), (1,)), ((), ())),
                                preferred_element_type=F32)
                for hh in heads]

    def visible(c0, shape):
        kpos = c0 * ck + lax.broadcasted_iota(jnp.int32, shape, 0)
        qpos = qi * tq + lax.broadcasted_iota(jnp.int32, shape, 1)
        return (kpos <= qpos) | (qpos < PAD_FRONT)

    def weighted_values(c0, n, hh, p):
        pb = p.astype(BF16)
        lo = (hh % 2) * V_HEAD_DIM
        out = None
        for c in range(n):
            vt = vt_ref[0, hh // 2, c0 + c][lo:lo + V_HEAD_DIM, :]
            d = jnp.dot(vt, pb[c * ck:(c + 1) * ck, :], preferred_element_type=F32)
            out = d if out is None else out + d
        return out

    def step_bounded(c0, n, masked):
        scores = span_scores(c0, n)
        for hh in heads:
            p = jnp.exp2(scores[hh])
            if masked:
                p = jnp.where(visible(c0, p.shape), p, 0.0)
            l_ref[hh] += jnp.sum(p.reshape(n * ck // 8, 8, tq), axis=0)
            acc_ref[hh] += weighted_values(c0, n, hh, p)

    def step_general(c0, n, masked):
        scores = span_scores(c0, n)
        for hh in heads:
            s = scores[hh]
            if masked:
                s = jnp.where(visible(c0, s.shape), s, MASK_VALUE)
            m_prev = m_ref[hh]
            m_new = jnp.maximum(m_prev, jnp.max(s, axis=0, keepdims=True))
            alpha = jnp.exp2(m_prev - m_new)
            p = jnp.exp2(s - m_new)
            l_ref[hh, 0:1, :] = alpha * l_ref[hh, 0:1, :] + jnp.sum(p, axis=0, keepdims=True)
            acc_ref[hh] = alpha * acc_ref[hh] + weighted_values(c0, n, hh, p)
            m_ref[hh] = m_new

    def run(step, groups):
        m_ref[...] = jnp.full_like(m_ref, -jnp.inf)
        l_ref[...] = jnp.zeros_like(l_ref)
        acc_ref[...] = jnp.zeros_like(acc_ref)

        first, left = 0, qi
        for group in groups:
            def body(g, carry, first=first, group=group):
                step(first + g * group, group, False)
                return carry

            trips = left // group
            lax.fori_loop(0, trips, body, 0)
            first, left = first + trips * group, left - trips * group
        for r in range(groups[-1]):
            @pl.when(left == r)
            def _(r=r):
                output_stage()
                step(qi - r, r + 1, True)
                out_t = jnp.concatenate([acc_ref[hh] / jnp.sum(l_ref[hh], axis=0, keepdims=True) for hh in heads],
                                        axis=0)
                o_ref[...] = out_t.T

    @pl.when(qi == 0)
    def _():
        o_ref[...] = jnp.zeros_like(o_ref)

    @pl.when(qi == n_tiles)
    def _():
        output_stage()

    @pl.when(qi < n_tiles)
    def _():
        k_ref[0, :, pl.ds(pl.multiple_of(qi * ck, ck), ck), :] = knew_ref[0]
        vt_ref[0, :, qi] = vtnew_ref[0, :, 0]
        bounded = bounded_ref[layer] != 0

        @pl.when(bounded)
        def _():
            run(step_bounded, KV_GROUPS)

        @pl.when(jnp.logical_not(bounded))
        def _():
            run(step_general, KV_GROUPS[-1:])


def _const_spec(shape):
    return pl.BlockSpec(shape, lambda *_: (0,) * len(shape))


def _layer_spec(stacked, layer):
    rest = stacked.shape[1:]
    return pl.BlockSpec((None,) + rest, lambda *_: (layer,) + (0,) * len(rest))


def _row_spec(t, width):
    return pl.BlockSpec((1, t, width), lambda b, i: (b, i, 0))


def _lane_frequencies():
    half = QK_ROPE_DIM // 2
    inv_freq = ROPE_THETA ** (-jnp.arange(half, dtype=F32) / half)
    return _to_head_lanes(jnp.concatenate([jnp.zeros((QK_NOPE_DIM,), F32), inv_freq, inv_freq]))[None, :]


def _pre(h, tabs, wts, layer, meta=None):
    first_layer = meta is not None
    b = h.shape[0]
    lp = tabs[0].shape[1]
    t = ROW_TILE
    nch = t // ATT_TILE
    tab_specs = [_row_spec(t, LANES)] * 2
    if first_layer:
        tab_specs = [_row_spec(t, 1), _const_spec((1, LANES))]
        assert PAD_FRONT + N_META == ATT_TILE and PAD_BACK == 0
        token_block = lambda j: pl.BlockSpec((1, ATT_TILE, D_MODEL),
                                             lambda bb, i: (bb, jnp.maximum(nch * i - 1 + j, 0), 0))
        src = [meta] + [h] * nch
        src_specs = [_const_spec(meta.shape)] + [token_block(j) for j in range(nch)]
        extra_specs = [_row_spec(t, D_MODEL)] + [_row_spec(t, LANES)] * 2
        extra_shapes = ([jax.ShapeDtypeStruct((b, lp, D_MODEL), F32)]
                        + [jax.ShapeDtypeStruct((b, lp, LANES), F32)] * 2)
    else:
        src, src_specs, extra_specs, extra_shapes = [h], [_row_spec(t, D_MODEL)], [], []
    k_shape = jax.ShapeDtypeStruct((b, N_HEADS, lp, HEAD_PAD), BF16)
    k_spec = pl.BlockSpec((1, N_HEADS, t, HEAD_PAD), lambda bb, i: (bb, 0, i, 0))
    vt_shape = jax.ShapeDtypeStruct((b, N_HEADS // 2, lp // ATT_TILE, 2 * V_HEAD_DIM, ATT_TILE), BF16)
    vt_spec = pl.BlockSpec((1, N_HEADS // 2, nch, 2 * V_HEAD_DIM, ATT_TILE), lambda bb, i: (bb, 0, i, 0, 0))
    weights = [wts[name] for name in ("ng", "win", "wg", "ps", "wpu", "gqa", "gkva", "wqb", "wkvb", "gqn", "gkn")]
    return pl.pallas_call(
        functools.partial(_pre_kernel, first_layer=first_layer),
        grid=(b, lp // t),
        in_specs=(src_specs + tab_specs + [_layer_spec(w, layer) for w in weights]
                  + [_const_spec(wts["ones"].shape)]),
        out_specs=[k_spec, k_spec, vt_spec, _row_spec(t, MLA_WIDTH), _row_spec(t, D_MODEL),
                   _row_spec(t, D_MODEL)] + extra_specs,
        out_shape=[k_shape, k_shape, vt_shape,
                   jax.ShapeDtypeStruct((b, lp, MLA_WIDTH), BF16),
                   jax.ShapeDtypeStruct((b, lp, D_MODEL), BF16),
                   jax.ShapeDtypeStruct((b, lp, D_MODEL), BF16)] + extra_shapes,
        scratch_shapes=[pltpu.VMEM((HALO, POOL_WIDTH), F32)],
        compiler_params=pltpu.CompilerParams(dimension_semantics=("parallel", "arbitrary"),
                                             vmem_limit_bytes=VMEM_LIMIT),
        name="pre",
    )(*src, *tabs, *weights, wts["ones"])


def _scores_bounded(q_gain, k_gain):
    limit = (QK_HEAD_DIM * jnp.max(jnp.abs(q_gain), axis=-1) * jnp.max(jnp.abs(k_gain), axis=-1)
             * (math.log2(math.e) / math.sqrt(QK_HEAD_DIM)) * 1.02)
    return (limit <= SCORE_BOUND).astype(jnp.int32)


def _attention(bounded, layer, qt, k, vt, h, sz, sg, pc, wts, final):
    b, nh, lp, _ = k.shape
    t = ATT_TILE
    n_tiles = lp // t
    assert PAD_FRONT + N_META == t
    tile = lambda i: jnp.minimum(i, n_tiles - 1)
    prev = lambda i: jnp.maximum(i - 1, 0)
    if final:
        out_spec = pl.BlockSpec((1, t, D_MODEL), lambda bb, i, f: (bb, jnp.maximum(i - 2, 0), 0))
        out_shape = jax.ShapeDtypeStruct((b, lp - t - PAD_BACK, D_MODEL), F32)
        aliases = {}
    else:
        out_spec = pl.BlockSpec((1, t, D_MODEL), lambda bb, i, f: (bb, prev(i), 0))
        out_shape = jax.ShapeDtypeStruct(h.shape, F32)
        aliases = {4: 0}
    single = pl.Buffered(1)
    rows = lambda width: pl.BlockSpec((1, t, width), lambda bb, i, f: (bb, prev(i), 0))
    weight = lambda w: pl.BlockSpec((None,) + w.shape[1:], lambda bb, i, f: (layer, 0, 0), pipeline_mode=single)
    grid_spec = pltpu.PrefetchScalarGridSpec(
        num_scalar_prefetch=1,
        grid=(b, n_tiles + 1),
        in_specs=[pl.BlockSpec((1, nh, t, HEAD_PAD), lambda bb, i, f: (bb, 0, tile(i), 0)),
                  pl.BlockSpec((1, nh, t, HEAD_PAD), lambda bb, i, f: (bb, 0, tile(i), 0)),
                  pl.BlockSpec((1, nh // 2, 1, 2 * V_HEAD_DIM, t), lambda bb, i, f: (bb, 0, tile(i), 0, 0)),
                  rows(D_MODEL), rows(MLA_WIDTH), rows(D_MODEL), rows(D_MODEL),
                  weight(wts["wmu"]), weight(wts["wout"])],
        out_specs=out_spec,
        scratch_shapes=[pltpu.VMEM((1, nh, lp, HEAD_PAD), BF16),
                        pltpu.VMEM((1, nh // 2, lp // t, 2 * V_HEAD_DIM, t), BF16),
                        pltpu.VMEM((nh, 1, t), F32),
                        pltpu.VMEM((nh, 8, t), F32),
                        pltpu.VMEM((nh, V_HEAD_DIM, t), F32),
                        pltpu.VMEM((t, nh * V_HEAD_DIM), F32)],
    )
    return pl.pallas_call(
        functools.partial(_attn_kernel, layer=layer),
        grid_spec=grid_spec,
        out_shape=out_shape,
        input_output_aliases=aliases,
        compiler_params=pltpu.CompilerParams(dimension_semantics=("parallel", "arbitrary"),
                                             vmem_limit_bytes=VMEM_LIMIT),
        name="attn",
    )(bounded, qt, k, vt, h, sz, sg, pc, wts["wmu"], wts["wout"])


def _to_head_lanes(a):
    n = a.shape[-1]
    half = QK_ROPE_DIM // 2
    split = LANES // 2 - half
    zeros = lambda w: jnp.zeros(a.shape[:-1] + (w,), a.dtype)
    rot1 = a[..., QK_NOPE_DIM:QK_NOPE_DIM + half] if n > QK_NOPE_DIM else zeros(half)
    rot2 = a[..., QK_NOPE_DIM + half:] if n > QK_NOPE_DIM else zeros(half)
    assert [HEAD_LANES[d] for d in (0, split, QK_NOPE_DIM, QK_NOPE_DIM + half)] == [half, LANES // 2 + half, 0,
                                                                                     LANES // 2]
    return jnp.concatenate([rot1, a[..., :split], rot2, a[..., split:QK_NOPE_DIM], zeros(HEAD_PAD - QK_HEAD_DIM)],
                           axis=-1)


def _prepare_weights(norm_gain, w_in, pool_w_group, pool_scale, pool_w_up, q_a_norm_gain, kv_a_norm_gain, w_q_b,
                     w_kv_b, q_norm_gain, k_norm_gain, mla_w_up, w_out):
    depth = w_in.shape[0]
    half = QK_ROPE_DIM // 2
    kr0 = C_KV + KV_LORA_RANK
    place = lambda a, at: jnp.pad(a, ((0, 0), (0, 0), (at, C_END - at - a.shape[-1])))
    same = lax.pad(w_in, jnp.zeros((), w_in.dtype), [(0, 0, 0), (0, 0, 0), (0, C_END - w_in.shape[-1], 0)])
    moved = lax.pad(w_in, jnp.zeros((), w_in.dtype),
                    [(0, 0, 0), (0, 0, 0), (-QK_ROPE_DIM, C_END - w_in.shape[-1] + QK_ROPE_DIM, 0)])
    col = lax.broadcasted_iota(jnp.int32, (1, 1, C_END), 2)
    win = (jnp.where(col < kr0, same, jnp.where(col < C_KR, moved, 0.0))
           + place(w_in[:, :, kr0:kr0 + half], C_KR + HEAD_LANES[QK_NOPE_DIM])
           + place(w_in[:, :, kr0 + half:kr0 + QK_ROPE_DIM], C_KR + HEAD_LANES[QK_NOPE_DIM + half])).astype(BF16)
    wqb = _to_head_lanes(w_q_b.reshape(depth, Q_LORA_RANK, N_HEADS, QK_HEAD_DIM)).reshape(depth, Q_LORA_RANK, -1)
    wkv = w_kv_b.reshape(depth, KV_LORA_RANK, N_HEADS, QK_NOPE_DIM + V_HEAD_DIM)
    wkvb = jnp.concatenate([_to_head_lanes(wkv[..., :QK_NOPE_DIM]).reshape(depth, KV_LORA_RANK, -1),
                            wkv[..., QK_NOPE_DIM:].reshape(depth, KV_LORA_RANK, -1)], axis=2)
    eye = jnp.eye(pool_w_group.shape[1], dtype=pool_w_group.dtype)
    wg = (pool_w_group[:, :, :, None, :] * eye[None, :, None, :, None]).reshape(depth, POOL_WIDTH, POOL_WIDTH)
    q_scale = math.log2(math.e) / math.sqrt(QK_HEAD_DIM)
    block = jnp.arange(2 * HEAD_PAD) // HEAD_PAD
    row = lambda g: g[:, None, :]
    return {
        "ones": (block[:, None] == block[None, :]).astype(BF16),
        "ng": row(norm_gain),
        "win": win,
        "wg": wg.astype(BF16),
        "ps": row(pool_scale),
        "wpu": pool_w_up.astype(BF16),
        "gqa": row(q_a_norm_gain),
        "gkva": row(kv_a_norm_gain),
        "wqb": wqb.astype(BF16),
        "wkvb": wkvb.astype(BF16),
        "gqn": row(jnp.tile(_to_head_lanes(q_norm_gain * q_scale), (1, 2))),
        "gkn": row(jnp.tile(_to_head_lanes(k_norm_gain), (1, 2))),
        "wmu": mla_w_up.astype(BF16),
        "wout": w_out.astype(BF16),
    }


def kernel(x, positions, meta_tokens, norm_gain, w_in, pool_w_group, pool_scale, pool_w_up, q_a_norm_gain,
           kv_a_norm_gain, w_q_b, w_kv_b, q_norm_gain, k_norm_gain, mla_w_up, w_out):
    b, seq, _ = x.shape
    lp = PAD_FRONT + N_META + seq + PAD_BACK
    assert lp % ROW_TILE == 0 and ROW_TILE % ATT_TILE == 0

    meta_pos = jnp.broadcast_to(jnp.arange(N_META, dtype=jnp.int32)[None], (b, N_META))
    pos = jnp.concatenate([jnp.zeros((b, PAD_FRONT), jnp.int32), meta_pos, positions + N_META,
                           jnp.zeros((b, PAD_BACK), jnp.int32)], axis=1)

    wts = _prepare_weights(norm_gain, w_in, pool_w_group, pool_scale, pool_w_up, q_a_norm_gain, kv_a_norm_gain,
                           w_q_b, w_kv_b, q_norm_gain, k_norm_gain, mla_w_up, w_out)
    bounded = _scores_bounded(q_norm_gain, k_norm_gain)
    q, k, v, sz, sg, pc, h, *tabs = _pre(x, (pos[:, :, None], _lane_frequencies()), wts, 0,
                                         meta=meta_tokens.astype(x.dtype))
    for l in range(DEPTH):
        if l > 0:
            q, k, v, sz, sg, pc = _pre(h, tabs, wts, l)
        h = _attention(bounded, l, q, k, v, h, sz, sg, pc, wts, final=(l == DEPTH - 1))

    return h
```
